```python
import math
import jax, jax.numpy as jnp
from jax import lax
import numpy as np

D_MODEL = 1024
BATCH = 1
SEQ = 16384
DEPTH = 4
DEC_BATCH = 16
DEC_SEQ = 2048
PAST_LEN = 128

GRID_W = 64
Q_BLOCK = 128
ROPE_THETA = 10000.0
LN_EPS = 1e-5
RMS_EPS = 1e-6
MLA_HEADS = 6
MLA_Q_LORA = 256
MLA_KV_LORA = 128
MLA_NOPE = 64
MLA_ROPE = 32
MLA_V = 64
MLA_WIDTH = MLA_HEADS * MLA_V
CONV_CH = 256
CONV_K = 31
GQA_HEADS = 6
GQA_KV_HEADS = 2
GQA_HD = 64
GQA_WIDTH = GQA_HEADS * GQA_HD
D_MIX = MLA_WIDTH + CONV_CH + GQA_WIDTH
SPLITS = (MLA_Q_LORA, MLA_KV_LORA, MLA_ROPE, 2 * CONV_CH,
          GQA_HEADS * GQA_HD, GQA_KV_HEADS * GQA_HD, GQA_KV_HEADS * GQA_HD)
D_IN = sum(SPLITS)
MEM_TOKENS = 256
XA_HEADS = 4
XA_HD = D_MODEL // XA_HEADS
N_EXPERTS = 16
N_GROUPS = 4
EXPERTS_PER_GROUP = N_EXPERTS // N_GROUPS
TOP_K = 2
D_FF_EXPERT = 512
MOE_BLOCK = 128
DN_ALPHA = (2 * DEPTH) ** 0.25
DN_BETA = (8 * DEPTH) ** -0.25

kernel_name = 'hybrid_mla_conv_axialgqa_moe_encoder'

F32 = jnp.float32


def layer_norm(x, g, b):
    xf = x.astype(F32)
    mu = jnp.mean(xf, -1, keepdims=True)
    xc = xf - mu
    var = jnp.mean(xc * xc, -1, keepdims=True)
    return (xc * lax.rsqrt(var + LN_EPS) * g.astype(F32) + b.astype(F32)).astype(x.dtype)


def rms_norm(x, g):
    xf = x.astype(F32)
    y = xf * lax.rsqrt(jnp.mean(xf * xf, -1, keepdims=True) + RMS_EPS)
    return (y * g.astype(F32)).astype(x.dtype)


def rope_cos_sin(pos, dim):
    inv = 1.0 / (ROPE_THETA ** (jnp.arange(0, dim, 2, dtype=F32) / dim))
    ang = pos[:, None] * inv[None, :]
    return jnp.cos(ang), jnp.sin(ang)


def apply_rope(x, cos, sin):
    half = x.shape[-1] // 2
    xf = x.astype(F32)
    x1, x2 = xf[..., :half], xf[..., half:]
    shape = (1, cos.shape[0]) + (1,) * (x.ndim - 3) + (half,)
    c = cos.reshape(shape)
    s = sin.reshape(shape)
    return jnp.concatenate([x1 * c - x2 * s, x1 * s + x2 * c], -1).astype(x.dtype)


def apply_axial_rope(x, cs_row, cs_col):
    half = x.shape[-1] // 2
    return jnp.concatenate([apply_rope(x[..., :half], cs_row[0], cs_row[1]),
                            apply_rope(x[..., half:], cs_col[0], cs_col[1])], -1)


def grid_positions(S):
    rows = S // GRID_W
    row = jnp.repeat(jnp.arange(rows, dtype=F32), GRID_W)
    col = jnp.tile(jnp.arange(GRID_W, dtype=F32), rows)
    return row, col


def blocked_attention(q, k, v, scale):
    B, S, Hk, G, Dk = q.shape
    nb = S // Q_BLOCK
    qb = q.reshape(B, nb, Q_BLOCK, Hk, G, Dk).transpose(1, 0, 2, 3, 4, 5)

    def one_block(qblk):
        s = jnp.einsum('bqhgd,bkhd->bhgqk', qblk, k).astype(F32) * scale
        p = jax.nn.softmax(s, axis=-1).astype(v.dtype)
        return jnp.einsum('bhgqk,bkhd->bqhgd', p, v)

    o = lax.map(one_block, qb)
    return o.transpose(1, 0, 2, 3, 4, 5).reshape(B, S, Hk, G, v.shape[-1])


def depthwise_conv(x, w, b):
    out = lax.conv_general_dilated(
        x, w[:, None, :].astype(x.dtype), window_strides=(1,),
        padding=[(CONV_K // 2, CONV_K // 2)],
        dimension_numbers=('NWC', 'WIO', 'NWC'),
        feature_group_count=x.shape[-1])
    return out + b.astype(x.dtype)


def token_mixer(h, cs1d, cs_row, cs_col, w_in, q_norm, w_uq, kv_norm, w_ukv,
                conv_w, conv_b, conv_ln_g, conv_ln_b, gq_norm, gk_norm, w_o):
    B, S, _ = h.shape
    z = h @ w_in
    cut = [int(c) for c in np.cumsum(SPLITS)[:-1]]
    c_q, c_kv, k_rope, glu_in, q_c, k_c, v_c = jnp.split(z, cut, axis=-1)

    q = (rms_norm(c_q, q_norm) @ w_uq).reshape(B, S, MLA_HEADS, MLA_NOPE + MLA_ROPE)
    q = jnp.concatenate([q[..., :MLA_NOPE], apply_rope(q[..., MLA_NOPE:], cs1d[0], cs1d[1])], -1)
    kv = (rms_norm(c_kv, kv_norm) @ w_ukv).reshape(B, S, MLA_HEADS, MLA_NOPE + MLA_V)
    k_r = apply_rope(k_rope, cs1d[0], cs1d[1])
    k = jnp.concatenate([kv[..., :MLA_NOPE],
                         jnp.broadcast_to(k_r[:, :, None, :], (B, S, MLA_HEADS, MLA_ROPE))], -1)
    v = kv[..., MLA_NOPE:]
    o_a = blocked_attention(q[:, :, :, None, :], k, v,
                            (MLA_NOPE + MLA_ROPE) ** -0.5).reshape(B, S, MLA_WIDTH)

    a, g = jnp.split(glu_in, 2, axis=-1)
    u = depthwise_conv(a * jax.nn.sigmoid(g), conv_w, conv_b)
    o_b = jax.nn.silu(layer_norm(u, conv_ln_g, conv_ln_b))

    qc = rms_norm(q_c.reshape(B, S, GQA_HEADS, GQA_HD), gq_norm)
    kc = rms_norm(k_c.reshape(B, S, GQA_KV_HEADS, GQA_HD), gk_norm)
    qc = apply_axial_rope(qc, cs_row, cs_col)
    kc = apply_axial_rope(kc, cs_row, cs_col)
    vc = v_c.reshape(B, S, GQA_KV_HEADS, GQA_HD)
    qc = qc.reshape(B, S, GQA_KV_HEADS, GQA_HEADS // GQA_KV_HEADS, GQA_HD)
    o_c = blocked_attention(qc, kc, vc, GQA_HD ** -0.5).reshape(B, S, GQA_WIDTH)

    return jnp.concatenate([o_a, o_b, o_c], -1) @ w_o


def cross_attention(h, mem, w_cq, w_ckv, w_co):
    B, S, _ = h.shape
    M = mem.shape[1]
    q = (h @ w_cq).reshape(B, S, XA_HEADS, XA_HD)
    k, v = jnp.split(mem @ w_ckv, 2, axis=-1)
    k = k.reshape(B, M, XA_HEADS, XA_HD)
    v = v.reshape(B, M, XA_HEADS, XA_HD)
    s = jnp.einsum('bqhd,bkhd->bhqk', q, k).astype(F32) * (XA_HD ** -0.5)
    p = jax.nn.softmax(s, axis=-1).astype(v.dtype)
    o = jnp.einsum('bhqk,bkhd->bqhd', p, v).reshape(B, S, D_MODEL)
    return o @ w_co


def moe_ffn(h, w_router, router_bias, w_gate, w_up, w_down):
    B, S, D = h.shape
    x = h.reshape(-1, D)
    N = x.shape[0]
    scores = jax.nn.sigmoid((x @ w_router).astype(F32))
    sel = scores + router_bias.astype(F32)
    group_score = lax.top_k(sel.reshape(N, N_GROUPS, EXPERTS_PER_GROUP), TOP_K)[0].sum(-1)
    grp = jnp.argmax(group_score, axis=-1)
    in_group = (jnp.arange(N_EXPERTS) // EXPERTS_PER_GROUP)[None, :] == grp[:, None]
    _, idx = lax.top_k(jnp.where(in_group, sel, -jnp.inf), TOP_K)
    gate = jnp.take_along_axis(scores, idx, axis=1)
    gate = gate / jnp.sum(gate, -1, keepdims=True)

    M = N * TOP_K
    P = M + N_EXPERTS * MOE_BLOCK
    nb = P // MOE_BLOCK
    flat_e = idx.reshape(-1).astype(jnp.int32)
    order = jnp.argsort(flat_e)
    sorted_e = flat_e[order]
    counts = jnp.bincount(flat_e, length=N_EXPERTS).astype(jnp.int32)
    padded = (counts + MOE_BLOCK - 1) // MOE_BLOCK * MOE_BLOCK
    pad_end = jnp.cumsum(padded)
    pad_start = pad_end - padded
    start = jnp.cumsum(counts) - counts
    dest = pad_start[sorted_e] + jnp.arange(M, dtype=jnp.int32) - start[sorted_e]
    slot_token = jnp.full((P,), N, jnp.int32).at[dest].set((order // TOP_K).astype(jnp.int32))
    block_e = jnp.minimum(jnp.searchsorted(pad_end, jnp.arange(nb, dtype=jnp.int32) * MOE_BLOCK,
                                           side='right'), N_EXPERTS - 1)
    x_pad = jnp.concatenate([x, jnp.zeros((1, D), x.dtype)], 0)
    xb = x_pad[slot_token].reshape(nb, MOE_BLOCK, D)

    def expert_block(args):
        xblk, e = args
        return (jax.nn.silu(xblk @ w_gate[e]) * (xblk @ w_up[e])) @ w_down[e]

    yb = lax.map(expert_block, (xb, block_e)).reshape(P, D)
    y_assign = jnp.zeros((M, D), yb.dtype).at[order].set(yb[dest])
    out = jnp.sum(y_assign.reshape(N, TOP_K, D) * gate[..., None].astype(yb.dtype), axis=1)
    return out.reshape(B, S, D)


def encoder_trunk(x, mem, ln_in_g, ln_in_b, w_in, mla_q_norm, mla_w_uq, mla_kv_norm, mla_w_ukv,
                  conv_w, conv_b, conv_ln_g, conv_ln_b, gqa_q_norm, gqa_k_norm, w_o,
                  ln1_g, ln1_b, w_cq, w_ckv, w_co, ln2_g, ln2_b,
                  w_router, router_bias, w_gate, w_up, w_down, ln3_g, ln3_b):
    B, S, _ = x.shape
    cs1d = rope_cos_sin(jnp.arange(S, dtype=F32), MLA_ROPE)
    row, col = grid_positions(S)
    cs_row = rope_cos_sin(row, GQA_HD // 2)
    cs_col = rope_cos_sin(col, GQA_HD // 2)
    h = layer_norm(x, ln_in_g, ln_in_b)
    for l in range(DEPTH):
        mix = token_mixer(h, cs1d, cs_row, cs_col, w_in[l], mla_q_norm[l], mla_w_uq[l],
                          mla_kv_norm[l], mla_w_ukv[l], conv_w[l], conv_b[l], conv_ln_g[l],
                          conv_ln_b[l], gqa_q_norm[l], gqa_k_norm[l], w_o[l])
        h = layer_norm(DN_ALPHA * h + mix, ln1_g[l], ln1_b[l])
        h = layer_norm(DN_ALPHA * h + cross_attention(h, mem, w_cq[l], w_ckv[l], w_co[l]),
                       ln2_g[l], ln2_b[l])
        h = layer_norm(DN_ALPHA * h + moe_ffn(h, w_router, router_bias, w_gate[l], w_up[l], w_down[l]),
                       ln3_g[l], ln3_b[l])
    return h


def setup_inputs(seed: int = 0) -> dict:
    key = jax.random.key(seed)
    ks = iter(jax.random.split(key, 40))

    def nrm(shape, scale):
        return jax.random.normal(next(ks), shape, F32) * scale

    def gain(shape):
        return 1.0 + 0.02 * jax.random.normal(next(ks), shape, F32)

    def bias(shape):
        return 0.02 * jax.random.normal(next(ks), shape, F32)

    L, D, E, F = DEPTH, D_MODEL, N_EXPERTS, D_FF_EXPERT
    return {
        'x_prompt': nrm((BATCH, SEQ, D), 1.0),
        'x_sample': nrm((DEC_BATCH, DEC_SEQ, D), 1.0),
        'mem_prompt': nrm((BATCH, MEM_TOKENS, D), 1.0),
        'mem_sample': nrm((DEC_BATCH, MEM_TOKENS, D), 1.0),
        'ln_in_g': gain((D,)),
        'ln_in_b': bias((D,)),
        'w_in': nrm((L, D, D_IN), D ** -0.5),
        'mla_q_norm': gain((L, MLA_Q_LORA)),
        'mla_w_uq': nrm((L, MLA_Q_LORA, MLA_HEADS * (MLA_NOPE + MLA_ROPE)), MLA_Q_LORA ** -0.5),
        'mla_kv_norm': gain((L, MLA_KV_LORA)),
        'mla_w_ukv': nrm((L, MLA_KV_LORA, MLA_HEADS * (MLA_NOPE + MLA_V)), MLA_KV_LORA ** -0.5),
        'conv_w': nrm((L, CONV_K, CONV_CH), CONV_K ** -0.5),
        'conv_b': bias((L, CONV_CH)),
        'conv_ln_g': gain((L, CONV_CH)),
        'conv_ln_b': bias((L, CONV_CH)),
        'gqa_q_norm': gain((L, GQA_HD)),
        'gqa_k_norm': gain((L, GQA_HD)),
        'w_o': nrm((L, D_MIX, D), DN_BETA * D_MIX ** -0.5),
        'ln1_g': gain((L, D)),
        'ln1_b': bias((L, D)),
        'w_cq': nrm((L, D, D), D ** -0.5),
        'w_ckv': nrm((L, D, 2 * D), D ** -0.5),
        'w_co': nrm((L, D, D), DN_BETA * D ** -0.5),
        'ln2_g': gain((L, D)),
        'ln2_b': bias((L, D)),
        'w_router': nrm((D, E), D ** -0.5),
        'router_bias': nrm((E,), 0.01),
        'w_gate': nrm((L, E, D, F), D ** -0.5),
        'w_up': nrm((L, E, D, F), D ** -0.5),
        'w_down': nrm((L, E, F, D), DN_BETA * F ** -0.5),
        'ln3_g': gain((L, D)),
        'ln3_b': bias((L, D)),
    }


def reference(x_prompt, x_sample, mem_prompt, mem_sample, ln_in_g, ln_in_b, w_in,
              mla_q_norm, mla_w_uq, mla_kv_norm, mla_w_ukv, conv_w, conv_b, conv_ln_g, conv_ln_b,
              gqa_q_norm, gqa_k_norm, w_o, ln1_g, ln1_b, w_cq, w_ckv, w_co, ln2_g, ln2_b,
              w_router, router_bias, w_gate, w_up, w_down, ln3_g, ln3_b):
    weights = (ln_in_g, ln_in_b, w_in, mla_q_norm, mla_w_uq, mla_kv_norm, mla_w_ukv,
               conv_w, conv_b, conv_ln_g, conv_ln_b, gqa_q_norm, gqa_k_norm, w_o,
               ln1_g, ln1_b, w_cq, w_ckv, w_co, ln2_g, ln2_b,
               w_router, router_bias, w_gate, w_up, w_down, ln3_g, ln3_b)
    y_prompt = encoder_trunk(x_prompt, mem_prompt, *weights)
    y_sample = encoder_trunk(x_sample, mem_sample, *weights)
    return (y_prompt, y_sample)
```

```python
import functools

import numpy as np
import jax
import jax.numpy as jnp
from jax import lax
from jax.experimental import pallas as pl
from jax.experimental.pallas import tpu as pltpu

F32 = jnp.float32
BF16 = jnp.bfloat16

D_MODEL = 1024
DEPTH = 4
GRID_W = 64
ROPE_THETA = 10000.0
LN_EPS = 1e-5
RMS_EPS = 1e-6
MLA_HEADS = 6
MLA_Q_LORA = 256
MLA_KV_LORA = 128
MLA_NOPE = 64
MLA_ROPE = 32
MLA_V = 64
CONV_CH = 256
CONV_K = 31
GQA_HEADS = 6
GQA_KV_HEADS = 2
GQA_HD = 64
SPLITS = (MLA_Q_LORA, MLA_KV_LORA, MLA_ROPE, 2 * CONV_CH,
          GQA_HEADS * GQA_HD, GQA_KV_HEADS * GQA_HD, GQA_KV_HEADS * GQA_HD)
XA_HEADS = 4
XA_HD = D_MODEL // XA_HEADS
N_EXPERTS = 16
N_GROUPS = 4
EXPERTS_PER_GROUP = N_EXPERTS // N_GROUPS
D_FF_EXPERT = 512
DN_ALPHA = (2 * DEPTH) ** 0.25
MLA_SCALE = (MLA_NOPE + MLA_ROPE) ** -0.5
GQA_SCALE = GQA_HD ** -0.5
XA_SCALE = XA_HD ** -0.5

LANE = 128
HALF = LANE // 2
VMEM_LIMIT = 56 * 1024 * 1024

_OFF_CQ = 0
_OFF_CKV = 256
_OFF_KR = 384
_OFF_KRR = 512
_OFF_GA = 640
_OFF_GG = 896
_OFF_QC = 1152
_OFF_QCR = 1920
_OFF_KC = 2688
_OFF_KCR = 2816
_OFF_VC = 2944
_W_EXT = 3072
_GQA_BLOCK_HEADS = (0, 3, 1, 4, 2, 5)


def _cparams(*sem):
    return pltpu.CompilerParams(dimension_semantics=sem, vmem_limit_bytes=VMEM_LIMIT)


def _row_tile(n, want):
    t = min(n, want)
    while n % t:
        t //= 2
    return t


def _ln_rows(x, g, b):
    mu = jnp.mean(x, axis=-1, keepdims=True)
    xc = x - mu
    var = jnp.mean(xc * xc, axis=-1, keepdims=True)
    return xc * lax.rsqrt(var + LN_EPS) * g + b


def _take_cols(w, idx, sign=None):
    w_aug = jnp.concatenate([w, jnp.zeros((w.shape[0], 1), w.dtype)], axis=1)
    idx = np.where(idx < 0, w.shape[1], idx)
    out = w_aug[:, idx]
    if sign is not None:
        out = out * jnp.asarray(sign, w.dtype)[None, :]
    return out


def _rot_src(d, half):
    r = d % (2 * half)
    base = d - r
    if r < half:
        return base + r + half, -1.0
    return base + r - half, 1.0


def _w_ext_index():
    cut = np.concatenate([[0], np.cumsum(SPLITS)])
    c_cq, c_ckv, c_kr, c_glu, c_qc, c_kc, c_vc = [int(c) for c in cut[:-1]]
    idx = -np.ones((_W_EXT,), np.int64)
    sgn = np.ones((_W_EXT,), np.float32)
    idx[_OFF_CQ:_OFF_CQ + 256] = c_cq + np.arange(256)
    idx[_OFF_CKV:_OFF_CKV + 128] = c_ckv + np.arange(128)
    for d in range(MLA_ROPE):
        idx[_OFF_KR + d] = c_kr + d
        s, sg = _rot_src(d, MLA_ROPE // 2)
        idx[_OFF_KRR + d] = c_kr + s
        sgn[_OFF_KRR + d] = sg
    idx[_OFF_GA:_OFF_GA + 256] = c_glu + np.arange(256)
    idx[_OFF_GG:_OFF_GG + 256] = c_glu + 256 + np.arange(256)
    for blk, hq in enumerate(_GQA_BLOCK_HEADS):
        lane0 = 0 if hq < GQA_HEADS // GQA_KV_HEADS else HALF
        for d in range(GQA_HD):
            idx[_OFF_QC + blk * LANE + lane0 + d] = c_qc + hq * GQA_HD + d
            s, sg = _rot_src(d, GQA_HD // 4)
            idx[_OFF_QCR + blk * LANE + lane0 + d] = c_qc + hq * GQA_HD + s
            sgn[_OFF_QCR + blk * LANE + lane0 + d] = sg
    for c in range(GQA_KV_HEADS * GQA_HD):
        hk, d = divmod(c, GQA_HD)
        idx[_OFF_KC + c] = c_kc + c
        s, sg = _rot_src(d, GQA_HD // 4)
        idx[_OFF_KCR + c] = c_kc + hk * GQA_HD + s
        sgn[_OFF_KCR + c] = sg
    idx[_OFF_VC:_OFF_VC + 128] = c_vc + np.arange(128)
    return idx, sgn


def _w_uq_index():
    per = MLA_NOPE + MLA_ROPE
    idx = -np.ones((MLA_HEADS * 256 + MLA_HEADS * LANE,), np.int64)
    sgn = np.ones(idx.shape, np.float32)
    for h in range(MLA_HEADS):
        nope0 = h * 256 + (HALF if h % 2 == 0 else 0)
        idx[nope0:nope0 + MLA_NOPE] = h * per + np.arange(MLA_NOPE)
        for d in range(MLA_ROPE):
            idx[h * 256 + LANE + d] = h * per + MLA_NOPE + d
            s, sg = _rot_src(d, MLA_ROPE // 2)
            idx[MLA_HEADS * 256 + h * LANE + d] = h * per + MLA_NOPE + s
            sgn[MLA_HEADS * 256 + h * LANE + d] = sg
    return idx, sgn


def _w_ukv_index():
    per = MLA_NOPE + MLA_V
    idx = np.zeros((MLA_HEADS * per,), np.int64)
    for h in range(MLA_HEADS):
        nat = h * per + np.arange(per)
        idx[h * per:(h + 1) * per] = np.roll(nat, MLA_V) if h % 2 == 0 else nat
    return idx


def _w_oc_index():
    idx = np.zeros((GQA_HEADS * GQA_HD,), np.int64)
    for blk, hq in enumerate(_GQA_BLOCK_HEADS):
        c0 = (blk // 2) * LANE + (blk % 2) * HALF
        idx[c0:c0 + GQA_HD] = hq * GQA_HD + np.arange(GQA_HD)
    return idx


def _perm_gain(g, half):
    src = np.array([_rot_src(d, half)[0] for d in range(g.shape[-1])])
    return g[src]


def _rope_tables(S):
    def cs(pos, dim):
        inv = 1.0 / (ROPE_THETA ** (jnp.arange(0, dim, 2, dtype=F32) / dim))
        ang = pos[:, None] * inv[None, :]
        return jnp.cos(ang), jnp.sin(ang)

    c1, s1 = cs(jnp.arange(S, dtype=F32), MLA_ROPE)
    rows = S // GRID_W
    row = jnp.repeat(jnp.arange(rows, dtype=F32), GRID_W)
    col = jnp.tile(jnp.arange(GRID_W, dtype=F32), rows)
    cr, sr = cs(row, GQA_HD // 2)
    cc, sc = cs(col, GQA_HD // 2)
    zpad = jnp.zeros((S, LANE - MLA_ROPE), F32)
    cm = jnp.concatenate([c1, c1, zpad], axis=1)
    sm = jnp.concatenate([s1, s1, zpad], axis=1)
    cg = jnp.concatenate([cr, cr, cc, cc] * 2, axis=1)
    sg = jnp.concatenate([sr, sr, sc, sc] * 2, axis=1)
    return jnp.concatenate([cm, sm, cg, sg], axis=1)


def _ln_kernel(x_ref, g_ref, b_ref, o_ref):
    o_ref[...] = _ln_rows(x_ref[...], g_ref[...], b_ref[...])


def _layer_norm_rows(x, g, b):
    n, d = x.shape
    tm = _row_tile(n, 1024)
    return pl.pallas_call(
        _ln_kernel,
        grid=(n // tm,),
        in_specs=[pl.BlockSpec((tm, d), lambda i: (i, 0)),
                  pl.BlockSpec((1, d), lambda i: (0, 0)),
                  pl.BlockSpec((1, d), lambda i: (0, 0))],
        out_specs=pl.BlockSpec((tm, d), lambda i: (i, 0)),
        out_shape=jax.ShapeDtypeStruct((n, d), F32),
        compiler_params=_cparams("parallel"),
    )(x, g.reshape(1, d), b.reshape(1, d))


def _rms_scale(ss, width):
    return lax.rsqrt(ss * (1.0 / width) + RMS_EPS)


def _pre_kernel(h_ref, tab_ref, wext_ref, wuq_ref, wukv_ref, qn_ref, kvn_ref,
                gq_ref, gqp_ref, gk_ref, gkp_ref,
                qa_ref, kva_ref, glu_ref, qg_ref, kvg_ref):
    hb = h_ref[0].astype(BF16)
    z = jnp.dot(hb, wext_ref[...], preferred_element_type=F32)
    cm = tab_ref[:, 0:LANE]
    sm = tab_ref[:, LANE:2 * LANE]
    cg = tab_ref[:, 2 * LANE:3 * LANE]
    sg = tab_ref[:, 3 * LANE:4 * LANE]

    cq = z[:, _OFF_CQ:_OFF_CQ + MLA_Q_LORA]
    cqn = cq * _rms_scale(jnp.sum(cq * cq, axis=-1, keepdims=True), MLA_Q_LORA) * qn_ref[...]
    a = jnp.dot(cqn.astype(BF16), wuq_ref[...], preferred_element_type=F32)
    rot0 = MLA_HEADS * 256
    for h in range(MLA_HEADS):
        c0 = h * 256
        qa_ref[0, :, c0:c0 + LANE] = (a[:, c0:c0 + LANE] * MLA_SCALE).astype(BF16)
        roped = a[:, c0 + LANE:c0 + 256] * cm + a[:, rot0 + h * LANE:rot0 + (h + 1) * LANE] * sm
        qa_ref[0, :, c0 + LANE:c0 + 256] = (roped * MLA_SCALE).astype(BF16)

    ckv = z[:, _OFF_CKV:_OFF_CKV + MLA_KV_LORA]
    ckvn = ckv * _rms_scale(jnp.sum(ckv * ckv, axis=-1, keepdims=True), MLA_KV_LORA) * kvn_ref[...]
    kvm = jnp.dot(ckvn.astype(BF16), wukv_ref[...], preferred_element_type=F32)
    kr = (z[:, _OFF_KR:_OFF_KR + LANE] * cm + z[:, _OFF_KRR:_OFF_KRR + LANE] * sm).astype(BF16)
    for h in range(MLA_HEADS):
        kva_ref[0, :, h * 256:h * 256 + LANE] = kvm[:, h * LANE:(h + 1) * LANE].astype(BF16)
        kva_ref[0, :, h * 256 + LANE:(h + 1) * 256] = kr

    glu_ref[0] = z[:, _OFF_GA:_OFF_GA + CONV_CH] * jax.nn.sigmoid(z[:, _OFF_GG:_OFF_GG + CONV_CH])

    gq = gq_ref[...]
    gqp = gqp_ref[...]
    for blk in range(GQA_HEADS):
        zq = z[:, _OFF_QC + blk * LANE:_OFF_QC + (blk + 1) * LANE]
        zr = z[:, _OFF_QCR + blk * LANE:_OFF_QCR + (blk + 1) * LANE]
        r = _rms_scale(jnp.sum(zq * zq, axis=-1, keepdims=True), GQA_HD)
        qg_ref[0, :, blk * LANE:(blk + 1) * LANE] = (
            (zq * gq * cg + zr * gqp * sg) * (r * GQA_SCALE)).astype(BF16)

    zk = z[:, _OFF_KC:_OFF_KC + LANE]
    zkr = z[:, _OFF_KCR:_OFF_KCR + LANE]
    low = lax.broadcasted_iota(jnp.int32, zk.shape, 1) < HALF
    sq = zk * zk
    r0 = _rms_scale(jnp.sum(jnp.where(low, sq, 0.0), axis=-1, keepdims=True), GQA_HD)
    r1 = _rms_scale(jnp.sum(jnp.where(low, 0.0, sq), axis=-1, keepdims=True), GQA_HD)
    kvg_ref[0, :, 0:LANE] = ((zk * gk_ref[...] * cg + zkr * gkp_ref[...] * sg)
                             * jnp.where(low, r0, r1)).astype(BF16)
    kvg_ref[0, :, LANE:2 * LANE] = z[:, _OFF_VC:_OFF_VC + LANE].astype(BF16)


def _mixer_pre(h, tab, lw):
    B, S, D = h.shape
    tm = _row_tile(S, 512)
    const = lambda b, i: (0, 0)
    row3 = lambda b, i: (b, i, 0)
    outs = [(MLA_HEADS * 256, BF16), (MLA_HEADS * 256, BF16), (CONV_CH, F32),
            (GQA_HEADS * LANE, BF16), (2 * LANE, BF16)]
    return pl.pallas_call(
        _pre_kernel,
        grid=(B, S // tm),
        in_specs=[pl.BlockSpec((1, tm, D), row3),
                  pl.BlockSpec((tm, 4 * LANE), lambda b, i: (i, 0)),
                  pl.BlockSpec(lw["w_ext"].shape, const),
                  pl.BlockSpec(lw["w_uq"].shape, const),
                  pl.BlockSpec(lw["w_ukv"].shape, const),
                  pl.BlockSpec((1, MLA_Q_LORA), const),
                  pl.BlockSpec((1, MLA_KV_LORA), const),
                  pl.BlockSpec((1, LANE), const),
                  pl.BlockSpec((1, LANE), const),
                  pl.BlockSpec((1, LANE), const),
                  pl.BlockSpec((1, LANE), const)],
        out_specs=[pl.BlockSpec((1, tm, w), row3) for w, _ in outs],
        out_shape=[jax.ShapeDtypeStruct((B, S, w), dt) for w, dt in outs],
        compiler_params=_cparams("parallel", "parallel"),
    )(h, tab, lw["w_ext"], lw["w_uq"], lw["w_ukv"], lw["qn"], lw["kvn"],
      lw["gq"], lw["gqp"], lw["gk"], lw["gkp"])


def _attn_kernel(q_ref, kv_ref, o_ref, m_sc, l_sc, acc_sc, *, qw, k_off, v_off, kw):
    j = pl.program_id(3)

    @pl.when(j == 0)
    def _():
        m_sc[...] = jnp.full(m_sc.shape, -jnp.inf, F32)
        l_sc[...] = jnp.zeros(l_sc.shape, F32)
        acc_sc[...] = jnp.zeros(acc_sc.shape, F32)

    for i in range(2):
        q = q_ref[0, :, i * qw:(i + 1) * qw]
        k = kv_ref[0, :, k_off[i]:k_off[i] + kw]
        v = kv_ref[0, :, v_off[i]:v_off[i] + LANE]
        s = lax.dot_general(q, k, (((1,), (1,)), ((), ())), preferred_element_type=F32)
        m_prev = m_sc[i]
        m_new = jnp.maximum(m_prev, jnp.max(s, axis=1, keepdims=True))
        alpha = jnp.exp(m_prev - m_new)
        p = jnp.exp(s - m_new)
        l_sc[i] = alpha * l_sc[i] + jnp.sum(p, axis=1, keepdims=True)
        acc_sc[i] = alpha * acc_sc[i] + jnp.dot(p.astype(BF16), v, preferred_element_type=F32)
        m_sc[i] = m_new

    @pl.when(j == pl.num_programs(3) - 1)
    def _():
        o0 = acc_sc[0] / l_sc[0]
        o1 = acc_sc[1] / l_sc[1]
        low = lax.broadcasted_iota(jnp.int32, o0.shape, 1) < HALF
        o_ref[0] = jnp.where(low, o0, o1).astype(o_ref.dtype)


def _attention(q, kv, *, qw, kvw_block, kv_block_of_pair, k_off, v_off, kw, tq_want, tk_want):
    B, S, _ = q.shape
    n_pairs = 3
    tq = _row_tile(S, tq_want)
    tk = _row_tile(S, tk_want)
    kern = functools.partial(_attn_kernel, qw=qw, k_off=k_off, v_off=v_off, kw=kw)
    return pl.pallas_call(
        kern,
        grid=(B, n_pairs, S // tq, S // tk),
        in_specs=[pl.BlockSpec((1, tq, 2 * qw), lambda b, p, i, j: (b, i, p)),
                  pl.BlockSpec((1, tk, kvw_block), lambda b, p, i, j: (b, j, kv_block_of_pair(p)))],
        out_specs=pl.BlockSpec((1, tq, LANE), lambda b, p, i, j: (b, i, p)),
        out_shape=jax.ShapeDtypeStruct((B, S, n_pairs * LANE), BF16),
        scratch_shapes=[pltpu.VMEM((2, tq, 1), F32),
                        pltpu.VMEM((2, tq, 1), F32),
                        pltpu.VMEM((2, tq, LANE), F32)],
        compiler_params=_cparams("parallel", "parallel", "parallel", "arbitrary"),
    )(q, kv)


def _mla_attention(qa, kva):
    return _attention(qa, kva, qw=256, kvw_block=512, kv_block_of_pair=lambda p: p,
                      k_off=(0, 256), v_off=(0, 256), kw=256, tq_want=512, tk_want=1024)


def _gqa_attention(qg, kvg):
    return _attention(qg, kvg, qw=LANE, kvw_block=2 * LANE, kv_block_of_pair=lambda p: 0,
                      k_off=(0, 0), v_off=(LANE, LANE), kw=LANE, tq_want=512, tk_want=1024)


_HALO = 16


def _conv_kernel(prev_ref, cur_ref, next_ref, w_ref, b_ref, g_ref, beta_ref, o_ref, xe_sc):
    i = pl.program_id(1)
    tc = cur_ref.shape[1]
    first = i == 0
    last = i == pl.num_programs(1) - 1
    xe_sc[0:_HALO, :] = jnp.where(first, 0.0, prev_ref[0])
    xe_sc[_HALO:_HALO + tc, :] = cur_ref[0]
    xe_sc[_HALO + tc:_HALO + tc + _HALO, :] = jnp.where(last, 0.0, next_ref[0])
    acc = jnp.zeros((tc, CONV_CH), F32)
    for t in range(CONV_K):
        off = _HALO - CONV_K // 2 + t
        acc = acc + xe_sc[off:off + tc, :] * w_ref[t:t + 1, :]
    u = acc + b_ref[...]
    y = _ln_rows(u, g_ref[...], beta_ref[...])
    o_ref[0] = (y * jax.nn.sigmoid(y)).astype(o_ref.dtype)


def _conv_module(glu, lw):
    B, S, C = glu.shape
    tc = _row_tile(S, 512)
    nh = tc // _HALO
    n_halo_blocks = S // _HALO
    const = lambda b, i: (0, 0)
    return pl.pallas_call(
        _conv_kernel,
        grid=(B, S // tc),
        in_specs=[pl.BlockSpec((1, _HALO, C), lambda b, i: (b, jnp.maximum(i * nh - 1, 0), 0)),
                  pl.BlockSpec((1, tc, C), lambda b, i: (b, i, 0)),
                  pl.BlockSpec((1, _HALO, C), lambda b, i: (b, jnp.minimum((i + 1) * nh, n_halo_blocks - 1), 0)),
                  pl.BlockSpec((CONV_K, C), const),
                  pl.BlockSpec((1, C), const),
                  pl.BlockSpec((1, C), const),
                  pl.BlockSpec((1, C), const)],
        out_specs=pl.BlockSpec((1, tc, C), lambda b, i: (b, i, 0)),
        out_shape=jax.ShapeDtypeStruct((B, S, C), BF16),
        scratch_shapes=[pltpu.VMEM((tc + 2 * _HALO, C), F32)],
        compiler_params=_cparams("parallel", "parallel"),
    )(glu, glu, glu, lw["conv_w"], lw["conv_b"], lw["conv_ln_g"], lw["conv_ln_b"])


def _oproj_kernel(h_ref, oa_ref, ob_ref, oc_ref, wa_ref, wb_ref, wc_ref, g_ref, b_ref, o_ref):
    mix = jnp.dot(oa_ref[...], wa_ref[...], preferred_element_type=F32)
    mix = mix + jnp.dot(ob_ref[...], wb_ref[...], preferred_element_type=F32)
    mix = mix + jnp.dot(oc_ref[...], wc_ref[...], preferred_element_type=F32)
    o_ref[...] = _ln_rows(DN_ALPHA * h_ref[...] + mix, g_ref[...], b_ref[...])


def _out_proj(h, oa, ob, oc, lw):
    n, d = h.shape
    tm = _row_tile(n, 512)
    const = lambda i: (0, 0)
    row = lambda i: (i, 0)
    return pl.pallas_call(
        _oproj_kernel,
        grid=(n // tm,),
        in_specs=[pl.BlockSpec((tm, d), row),
                  pl.BlockSpec((tm, oa.shape[1]), row),
                  pl.BlockSpec((tm, ob.shape[1]), row),
                  pl.BlockSpec((tm, oc.shape[1]), row),
                  pl.BlockSpec(lw["wo_a"].shape, const),
                  pl.BlockSpec(lw["wo_b"].shape, const),
                  pl.BlockSpec(lw["wo_c"].shape, const),
                  pl.BlockSpec((1, d), const),
                  pl.BlockSpec((1, d), const)],
        out_specs=pl.BlockSpec((tm, d), row),
        out_shape=jax.ShapeDtypeStruct((n, d), F32),
        compiler_params=_cparams("parallel"),
    )(h, oa, ob, oc, lw["wo_a"], lw["wo_b"], lw["wo_c"], lw["ln1_g"], lw["ln1_b"])


def _memkv_kernel(x_ref, w_ref, o_ref):
    o_ref[...] = jnp.dot(x_ref[...].astype(BF16), w_ref[...], preferred_element_type=F32).astype(o_ref.dtype)


def _mem_kv(mem2d, w_ckv):
    n, d = mem2d.shape
    tm = _row_tile(n, 256)
    return pl.pallas_call(
        _memkv_kernel,
        grid=(n // tm,),
        in_specs=[pl.BlockSpec((tm, d), lambda i: (i, 0)),
                  pl.BlockSpec(w_ckv.shape, lambda i: (0, 0))],
        out_specs=pl.BlockSpec((tm, w_ckv.shape[1]), lambda i: (i, 0)),
        out_shape=jax.ShapeDtypeStruct((n, w_ckv.shape[1]), BF16),
        compiler_params=_cparams("parallel"),
    )(mem2d, w_ckv)


def _xattn_kernel(h_ref, k_ref, v_ref, wq_ref, wo_ref, g_ref, b_ref, o_ref):
    h = h_ref[0]
    q = (jnp.dot(h.astype(BF16), wq_ref[...], preferred_element_type=F32) * XA_SCALE).astype(BF16)
    heads = []
    for hd in range(XA_HEADS):
        sl = slice(hd * XA_HD, (hd + 1) * XA_HD)
        s = lax.dot_general(q[:, sl], k_ref[0, :, sl], (((1,), (1,)), ((), ())),
                            preferred_element_type=F32)
        p = jnp.exp(s - jnp.max(s, axis=1, keepdims=True))
        o = jnp.dot(p.astype(BF16), v_ref[0, :, sl], preferred_element_type=F32)
        heads.append((o / jnp.sum(p, axis=1, keepdims=True)).astype(BF16))
    o = jnp.concatenate(heads, axis=1)
    y = jnp.dot(o, wo_ref[...], preferred_element_type=F32)
    o_ref[0] = _ln_rows(DN_ALPHA * h + y, g_ref[...], b_ref[...])


def _cross_attention(h, kvmem, lw):
    B, S, D = h.shape
    M = kvmem.shape[1]
    tm = _row_tile(S, 512)
    const = lambda b, i: (0, 0)
    return pl.pallas_call(
        _xattn_kernel,
        grid=(B, S // tm),
        in_specs=[pl.BlockSpec((1, tm, D), lambda b, i: (b, i, 0)),
                  pl.BlockSpec((1, M, D), lambda b, i: (b, 0, 0)),
                  pl.BlockSpec((1, M, D), lambda b, i: (b, 0, 1)),
                  pl.BlockSpec((D, D), const),
                  pl.BlockSpec((D, D), const),
                  pl.BlockSpec((1, D), const),
                  pl.BlockSpec((1, D), const)],
        out_specs=pl.BlockSpec((1, tm, D), lambda b, i: (b, i, 0)),
        out_shape=jax.ShapeDtypeStruct((B, S, D), F32),
        compiler_params=_cparams("parallel", "parallel"),
    )(h, kvmem, kvmem, lw["w_cq"], lw["w_co"], lw["ln2_g"], lw["ln2_b"])


MOE_TILE = 1024
MOE_ROWS = 128


def _router_kernel(x_ref, wr_ref, bias_ref, gate_ref, pos_ref, cnt_ref, sel_sc):
    T = x_ref.shape[0]
    logits = lax.dot_general(wr_ref[...], x_ref[...], (((1,), (1,)), ((), ())),
                             precision=lax.Precision.HIGHEST, preferred_element_type=F32)
    scores = jax.nn.sigmoid(logits)
    sel = scores + bias_ref[...]
    sc = [scores[e:e + 1, :] for e in range(N_EXPERTS)]
    se = [sel[e:e + 1, :] for e in range(N_EXPERTS)]

    gs = []
    for g in range(N_GROUPS):
        m = [se[g * EXPERTS_PER_GROUP + k] for k in range(EXPERTS_PER_GROUP)]
        best = None
        for a in range(EXPERTS_PER_GROUP):
            for b in range(a + 1, EXPERTS_PER_GROUP):
                pair = m[a] + m[b]
                best = pair if best is None else jnp.maximum(best, pair)
        gs.append(best)
    in_group = []
    for g in range(N_GROUPS):
        ok = None
        for o in range(N_GROUPS):
            if o == g:
                continue
            c = (gs[g] > gs[o]) if o < g else (gs[g] >= gs[o])
            ok = c if ok is None else (ok & c)
        in_group.append(ok)
    chosen = []
    for e in range(N_EXPERTS):
        g = e // EXPERTS_PER_GROUP
        beaten = jnp.zeros((1, T), jnp.int32)
        for o in range(g * EXPERTS_PER_GROUP, (g + 1) * EXPERTS_PER_GROUP):
            if o == e:
                continue
            c = (se[o] > se[e]) if o > e else (se[o] >= se[e])
            beaten = beaten + c.astype(jnp.int32)
        chosen.append(in_group[g] & (beaten < 2))
    denom = jnp.zeros((1, T), F32)
    for e in range(N_EXPERTS):
        denom = denom + jnp.where(chosen[e], sc[e], 0.0)
    for e in range(N_EXPERTS):
        gate_ref[e:e + 1, :] = jnp.where(chosen[e], sc[e] / denom, 0.0)
        sel_sc[e:e + 1, :] = chosen[e].astype(F32)

    selm = sel_sc[...]
    earlier = (lax.broadcasted_iota(jnp.int32, (T, T), 0) < lax.broadcasted_iota(jnp.int32, (T, T), 1))
    rank = jnp.dot(selm.astype(BF16), earlier.astype(BF16), preferred_element_type=F32)
    pos_ref[...] = jnp.where(selm > 0.0, rank, -1.0)
    cnt = jnp.sum(selm, axis=1, keepdims=True)
    cnt_ref[0] = jnp.broadcast_to(cnt, (N_EXPERTS, LANE)).astype(jnp.int32)


def _router(x, w_rt, bias):
    n, d = x.shape
    T = _row_tile(n, MOE_TILE)
    nt = n // T
    gate, pos, cnt = pl.pallas_call(
        _router_kernel,
        grid=(nt,),
        in_specs=[pl.BlockSpec((T, d), lambda i: (i, 0)),
                  pl.BlockSpec((N_EXPERTS, d), lambda i: (0, 0)),
                  pl.BlockSpec((N_EXPERTS, 1), lambda i: (0, 0))],
        out_specs=[pl.BlockSpec((N_EXPERTS, T), lambda i: (0, i)),
                   pl.BlockSpec((N_EXPERTS, T), lambda i: (0, i)),
                   pl.BlockSpec((1, N_EXPERTS, LANE), lambda i: (i, 0, 0))],
        out_shape=[jax.ShapeDtypeStruct((N_EXPERTS, n), F32),
                   jax.ShapeDtypeStruct((N_EXPERTS, n), F32),
                   jax.ShapeDtypeStruct((nt, N_EXPERTS, LANE), jnp.int32)],
        scratch_shapes=[pltpu.VMEM((N_EXPERTS, T), F32)],
        compiler_params=_cparams("parallel"),
    )(x, w_rt, bias)
    return gate, pos, cnt[:, :, 0].reshape(-1)


def _moe_kernel(cnt_ref, h_ref, gate_ref, pos_ref, wg_ref, wu_ref, wd_ref, g_ref, b_ref, o_ref, xb_sc, acc_sc):
    i = pl.program_id(0)
    e = pl.program_id(1)
    T = h_ref.shape[0]

    @pl.when(e == 0)
    def _():
        xb_sc[...] = h_ref[...].astype(BF16)
        acc_sc[...] = jnp.zeros(acc_sc.shape, F32)

    n_rows = cnt_ref[i * N_EXPERTS + e]
    n_blocks = (n_rows + MOE_ROWS - 1) // MOE_ROWS
    pos_row = pos_ref[pl.ds(e, 1), :].astype(jnp.int32)
    gate_row = gate_ref[pl.ds(e, 1), :]
    slot = lax.broadcasted_iota(jnp.int32, (MOE_ROWS, T), 0)

    def block(jb, carry):
        hit = (pos_row - jb * MOE_ROWS) == slot
        onehot = hit.astype(BF16)
        xg = jnp.dot(onehot, xb_sc[...], preferred_element_type=F32).astype(BF16)
        hg = jnp.dot(xg, wg_ref[0], preferred_element_type=F32)
        hu = jnp.dot(xg, wu_ref[0], preferred_element_type=F32)
        act = (hg * jax.nn.sigmoid(hg) * hu).astype(BF16)
        y = jnp.dot(act, wd_ref[0], preferred_element_type=F32)
        g_rows = jnp.sum(jnp.where(hit, gate_row, 0.0), axis=1, keepdims=True)
        ys = (y * g_rows).astype(BF16)
        acc_sc[...] += lax.dot_general(onehot, ys, (((0,), (0,)), ((), ())), preferred_element_type=F32)
        return carry

    lax.fori_loop(0, n_blocks, block, 0)

    @pl.when(e == N_EXPERTS - 1)
    def _():
        o_ref[...] = _ln_rows(DN_ALPHA * h_ref[...] + acc_sc[...], g_ref[...], b_ref[...])


def _moe(h, gate, pos, cnt, lw):
    n, d = h.shape
    T = _row_tile(n, MOE_TILE)
    f = D_FF_EXPERT
    grid_spec = pltpu.PrefetchScalarGridSpec(
        num_scalar_prefetch=1,
        grid=(n // T, N_EXPERTS),
        in_specs=[pl.BlockSpec((T, d), lambda i, e, c: (i, 0)),
                  pl.BlockSpec((N_EXPERTS, T), lambda i, e, c: (0, i)),
                  pl.BlockSpec((N_EXPERTS, T), lambda i, e, c: (0, i)),
                  pl.BlockSpec((1, d, f), lambda i, e, c: (e, 0, 0)),
                  pl.BlockSpec((1, d, f), lambda i, e, c: (e, 0, 0)),
                  pl.BlockSpec((1, f, d), lambda i, e, c: (e, 0, 0)),
                  pl.BlockSpec((1, d), lambda i, e, c: (0, 0)),
                  pl.BlockSpec((1, d), lambda i, e, c: (0, 0))],
        out_specs=pl.BlockSpec((T, d), lambda i, e, c: (i, 0)),
        scratch_shapes=[pltpu.VMEM((T, d), BF16), pltpu.VMEM((T, d), F32)],
    )
    return pl.pallas_call(
        _moe_kernel,
        grid_spec=grid_spec,
        out_shape=jax.ShapeDtypeStruct((n, d), F32),
        compiler_params=_cparams("parallel", "arbitrary"),
    )(cnt, h, gate, pos, lw["w_gate"], lw["w_up"], lw["w_down"], lw["ln3_g"], lw["ln3_b"])


def _prep_layer(l, w):
    ext_idx, ext_sgn = _w_ext_index()
    uq_idx, uq_sgn = _w_uq_index()
    mla_w = MLA_HEADS * MLA_V
    gq = w["gqa_q_norm"][l]
    gk = w["gqa_k_norm"][l]
    row = lambda v: v.reshape(1, -1).astype(F32)
    return {
        "w_ext": _take_cols(w["w_in"][l], ext_idx, ext_sgn).astype(BF16),
        "w_uq": _take_cols(w["mla_w_uq"][l], uq_idx, uq_sgn).astype(BF16),
        "w_ukv": w["mla_w_ukv"][l][:, _w_ukv_index()].astype(BF16),
        "qn": row(w["mla_q_norm"][l]),
        "kvn": row(w["mla_kv_norm"][l]),
        "gq": row(jnp.tile(gq, 2)),
        "gqp": row(jnp.tile(_perm_gain(gq, GQA_HD // 4), 2)),
        "gk": row(jnp.tile(gk, 2)),
        "gkp": row(jnp.tile(_perm_gain(gk, GQA_HD // 4), 2)),
        "conv_w": w["conv_w"][l].astype(F32),
        "conv_b": row(w["conv_b"][l]),
        "conv_ln_g": row(w["conv_ln_g"][l]),
        "conv_ln_b": row(w["conv_ln_b"][l]),
        "wo_a": w["w_o"][l][:mla_w].astype(BF16),
        "wo_b": w["w_o"][l][mla_w:mla_w + CONV_CH].astype(BF16),
        "wo_c": w["w_o"][l][mla_w + CONV_CH:][_w_oc_index()].astype(BF16),
        "ln1_g": row(w["ln1_g"][l]), "ln1_b": row(w["ln1_b"][l]),
        "w_cq": w["w_cq"][l].astype(BF16),
        "w_ckv": w["w_ckv"][l].astype(BF16),
        "w_co": w["w_co"][l].astype(BF16),
        "ln2_g": row(w["ln2_g"][l]), "ln2_b": row(w["ln2_b"][l]),
        "w_gate": w["w_gate"][l].astype(BF16),
        "w_up": w["w_up"][l].astype(BF16),
        "w_down": w["w_down"][l].astype(BF16),
        "ln3_g": row(w["ln3_g"][l]), "ln3_b": row(w["ln3_b"][l]),
    }


def _trunk(x, mem, w, layers):
    B, S, D = x.shape
    n = B * S
    tab = _rope_tables(S)
    w_rt = w["w_router"].T.astype(F32)
    r_bias = w["router_bias"].reshape(N_EXPERTS, 1).astype(F32)
    h = _layer_norm_rows(x.reshape(n, D), w["ln_in_g"], w["ln_in_b"])
    mem2d = mem.reshape(B * mem.shape[1], D)
    for lw in layers:
        qa, kva, glu, qg, kvg = _mixer_pre(h.reshape(B, S, D), tab, lw)
        oa = _mla_attention(qa, kva)
        ob = _conv_module(glu, lw)
        oc = _gqa_attention(qg, kvg)
        h = _out_proj(h, oa.reshape(n, -1), ob.reshape(n, -1), oc.reshape(n, -1), lw)
        kvmem = _mem_kv(mem2d, lw["w_ckv"]).reshape(B, mem.shape[1], 2 * D)
        h = _cross_attention(h.reshape(B, S, D), kvmem, lw).reshape(n, D)
        gate, pos, cnt = _router(h, w_rt, r_bias)
        h = _moe(h, gate, pos, cnt, lw)
    return h.reshape(B, S, D)


def kernel(x_prompt, x_sample, mem_prompt, mem_sample, ln_in_g, ln_in_b, w_in, mla_q_norm, mla_w_uq,
           mla_kv_norm, mla_w_ukv, conv_w, conv_b, conv_ln_g, conv_ln_b, gqa_q_norm, gqa_k_norm, w_o,
           ln1_g, ln1_b, w_cq, w_ckv, w_co, ln2_g, ln2_b, w_router, router_bias, w_gate, w_up, w_down,
           ln3_g, ln3_b):
    w = dict(ln_in_g=ln_in_g, ln_in_b=ln_in_b, w_in=w_in, mla_q_norm=mla_q_norm, mla_w_uq=mla_w_uq,
             mla_kv_norm=mla_kv_norm, mla_w_ukv=mla_w_ukv, conv_w=conv_w, conv_b=conv_b,
             conv_ln_g=conv_ln_g, conv_ln_b=conv_ln_b, gqa_q_norm=gqa_q_norm, gqa_k_norm=gqa_k_norm,
             w_o=w_o, ln1_g=ln1_g, ln1_b=ln1_b, w_cq=w_cq, w_ckv=w_ckv, w_co=w_co, ln2_g=ln2_g,
             ln2_b=ln2_b, w_router=w_router, router_bias=router_bias, w_gate=w_gate, w_up=w_up,
             w_down=w_down, ln3_g=ln3_g, ln3_b=ln3_b)
    layers = [_prep_layer(l, w) for l in range(w_in.shape[0])]
    y_prompt = _trunk(x_prompt, mem_prompt, w, layers)
    y_sample = _trunk(x_sample, mem_sample, w, layers)
    return (y_prompt, y_sample)
```

```python
import functools

import numpy as np
import jax
import jax.numpy as jnp
from jax import lax
from jax.experimental import pallas as pl
from jax.experimental.pallas import tpu as pltpu

F32 = jnp.float32
BF16 = jnp.bfloat16

D_MODEL = 1024
DEPTH = 4
GRID_W = 64
ROPE_THETA = 10000.0
LN_EPS = 1e-5
RMS_EPS = 1e-6
MLA_HEADS = 6
MLA_Q_LORA = 256
MLA_KV_LORA = 128
MLA_NOPE = 64
MLA_ROPE = 32
MLA_V = 64
CONV_CH = 256
CONV_K = 31
GQA_HEADS = 6
GQA_KV_HEADS = 2
GQA_HD = 64
SPLITS = (MLA_Q_LORA, MLA_KV_LORA, MLA_ROPE, 2 * CONV_CH,
          GQA_HEADS * GQA_HD, GQA_KV_HEADS * GQA_HD, GQA_KV_HEADS * GQA_HD)
XA_HEADS = 4
XA_HD = D_MODEL // XA_HEADS
N_EXPERTS = 16
N_GROUPS = 4
EXPERTS_PER_GROUP = N_EXPERTS // N_GROUPS
D_FF_EXPERT = 512
DN_ALPHA = (2 * DEPTH) ** 0.25
LOG2E = 1.4426950408889634
MLA_SCALE = (MLA_NOPE + MLA_ROPE) ** -0.5 * LOG2E
GQA_SCALE = GQA_HD ** -0.5 * LOG2E
XA_SCALE = XA_HD ** -0.5

LANE = 128
HALF = LANE // 2
VMEM_LIMIT = 56 * 1024 * 1024

_OFF_CQ = 0
_OFF_CKV = 256
_OFF_KR = 384
_OFF_KRR = 512
_OFF_GA = 640
_OFF_GG = 896
_OFF_QC = 1152
_OFF_QCR = 1920
_OFF_KC = 2688
_OFF_KCR = 2816
_OFF_VC = 2944
_W_EXT = 3072
_GQA_BLOCK_HEADS = (0, 3, 1, 4, 2, 5)


def _cparams(*sem):
    return pltpu.CompilerParams(dimension_semantics=sem, vmem_limit_bytes=VMEM_LIMIT)


def _row_tile(n, want):
    t = min(n, want)
    while n % t:
        t //= 2
    return t


def _ln_rows(x, g, b):
    mu = jnp.mean(x, axis=-1, keepdims=True)
    xc = x - mu
    var = jnp.mean(xc * xc, axis=-1, keepdims=True)
    return xc * lax.rsqrt(var + LN_EPS) * g + b


def _take_cols(w, idx, sign=None):
    w_aug = jnp.concatenate([w, jnp.zeros((w.shape[0], 1), w.dtype)], axis=1)
    idx = np.where(idx < 0, w.shape[1], idx)
    out = w_aug[:, idx]
    if sign is not None:
        out = out * jnp.asarray(sign, w.dtype)[None, :]
    return out


def _rot_src(d, half):
    r = d % (2 * half)
    base = d - r
    if r < half:
        return base + r + half, -1.0
    return base + r - half, 1.0


def _w_ext_index():
    cut = np.concatenate([[0], np.cumsum(SPLITS)])
    c_cq, c_ckv, c_kr, c_glu, c_qc, c_kc, c_vc = [int(c) for c in cut[:-1]]
    idx = -np.ones((_W_EXT,), np.int64)
    sgn = np.ones((_W_EXT,), np.float32)
    idx[_OFF_CQ:_OFF_CQ + 256] = c_cq + np.arange(256)
    idx[_OFF_CKV:_OFF_CKV + 128] = c_ckv + np.arange(128)
    for d in range(MLA_ROPE):
        idx[_OFF_KR + d] = c_kr + d
        s, sg = _rot_src(d, MLA_ROPE // 2)
        idx[_OFF_KRR + d] = c_kr + s
        sgn[_OFF_KRR + d] = sg
    idx[_OFF_GA:_OFF_GA + 256] = c_glu + np.arange(256)
    idx[_OFF_GG:_OFF_GG + 256] = c_glu + 256 + np.arange(256)
    for blk, hq in enumerate(_GQA_BLOCK_HEADS):
        lane0 = 0 if hq < GQA_HEADS // GQA_KV_HEADS else HALF
        for d in range(GQA_HD):
            idx[_OFF_QC + blk * LANE + lane0 + d] = c_qc + hq * GQA_HD + d
            s, sg = _rot_src(d, GQA_HD // 4)
            idx[_OFF_QCR + blk * LANE + lane0 + d] = c_qc + hq * GQA_HD + s
            sgn[_OFF_QCR + blk * LANE + lane0 + d] = sg
    for c in range(GQA_KV_HEADS * GQA_HD):
        hk, d = divmod(c, GQA_HD)
        idx[_OFF_KC + c] = c_kc + c
        s, sg = _rot_src(d, GQA_HD // 4)
        idx[_OFF_KCR + c] = c_kc + hk * GQA_HD + s
        sgn[_OFF_KCR + c] = sg
    idx[_OFF_VC:_OFF_VC + 128] = c_vc + np.arange(128)
    return idx, sgn


def _w_uq_index():
    per = MLA_NOPE + MLA_ROPE
    idx = -np.ones((MLA_HEADS * 256 + MLA_HEADS * LANE,), np.int64)
    sgn = np.ones(idx.shape, np.float32)
    for h in range(MLA_HEADS):
        nope0 = h * 256 + (HALF if h % 2 == 0 else 0)
        idx[nope0:nope0 + MLA_NOPE] = h * per + np.arange(MLA_NOPE)
        for d in range(MLA_ROPE):
            idx[h * 256 + LANE + d] = h * per + MLA_NOPE + d
            s, sg = _rot_src(d, MLA_ROPE // 2)
            idx[MLA_HEADS * 256 + h * LANE + d] = h * per + MLA_NOPE + s
            sgn[MLA_HEADS * 256 + h * LANE + d] = sg
    return idx, sgn


def _w_ukv_index():
    per = MLA_NOPE + MLA_V
    idx = np.zeros((MLA_HEADS * per,), np.int64)
    for h in range(MLA_HEADS):
        nat = h * per + np.arange(per)
        idx[h * per:(h + 1) * per] = np.roll(nat, MLA_V) if h % 2 == 0 else nat
    return idx


def _w_oc_index():
    idx = np.zeros((GQA_HEADS * GQA_HD,), np.int64)
    for blk, hq in enumerate(_GQA_BLOCK_HEADS):
        c0 = (blk // 2) * LANE + (blk % 2) * HALF
        idx[c0:c0 + GQA_HD] = hq * GQA_HD + np.arange(GQA_HD)
    return idx


def _perm_gain(g, half):
    src = np.array([_rot_src(d, half)[0] for d in range(g.shape[-1])])
    return g[src]


def _rope_tables(S):
    def cs(pos, dim):
        inv = 1.0 / (ROPE_THETA ** (jnp.arange(0, dim, 2, dtype=F32) / dim))
        ang = pos[:, None] * inv[None, :]
        return jnp.cos(ang), jnp.sin(ang)

    c1, s1 = cs(jnp.arange(S, dtype=F32), MLA_ROPE)
    rows = S // GRID_W
    row = jnp.repeat(jnp.arange(rows, dtype=F32), GRID_W)
    col = jnp.tile(jnp.arange(GRID_W, dtype=F32), rows)
    cr, sr = cs(row, GQA_HD // 2)
    cc, sc = cs(col, GQA_HD // 2)
    zpad = jnp.zeros((S, LANE - MLA_ROPE), F32)
    cm = jnp.concatenate([c1, c1, zpad], axis=1)
    sm = jnp.concatenate([s1, s1, zpad], axis=1)
    cg = jnp.concatenate([cr, cr, cc, cc] * 2, axis=1)
    sg = jnp.concatenate([sr, sr, sc, sc] * 2, axis=1)
    return jnp.concatenate([cm, sm, cg, sg], axis=1)


def _ln_kernel(x_ref, g_ref, b_ref, o_ref):
    o_ref[...] = _ln_rows(x_ref[...], g_ref[...], b_ref[...])


def _layer_norm_rows(x, g, b):
    n, d = x.shape
    tm = _row_tile(n, 1024)
    return pl.pallas_call(
        _ln_kernel,
        grid=(n // tm,),
        in_specs=[pl.BlockSpec((tm, d), lambda i: (i, 0)),
                  pl.BlockSpec((1, d), lambda i: (0, 0)),
                  pl.BlockSpec((1, d), lambda i: (0, 0))],
        out_specs=pl.BlockSpec((tm, d), lambda i: (i, 0)),
        out_shape=jax.ShapeDtypeStruct((n, d), F32),
        compiler_params=_cparams("parallel"),
    )(x, g.reshape(1, d), b.reshape(1, d))


def _rms_scale(ss, width):
    return lax.rsqrt(ss * (1.0 / width) + RMS_EPS)


def _pre_kernel(h_ref, tab_ref, wext_ref, wuq_ref, wukv_ref, qn_ref, kvn_ref,
                gq_ref, gqp_ref, gk_ref, gkp_ref,
                qa_ref, kva_ref, glu_ref, qg_ref, kvg_ref):
    hb = h_ref[0].astype(BF16)
    z = jnp.dot(hb, wext_ref[...], preferred_element_type=F32)
    cm = tab_ref[:, 0:LANE]
    sm = tab_ref[:, LANE:2 * LANE]
    cg = tab_ref[:, 2 * LANE:3 * LANE]
    sg = tab_ref[:, 3 * LANE:4 * LANE]

    cq = z[:, _OFF_CQ:_OFF_CQ + MLA_Q_LORA]
    cqn = cq * _rms_scale(jnp.sum(cq * cq, axis=-1, keepdims=True), MLA_Q_LORA) * qn_ref[...]
    a = jnp.dot(cqn.astype(BF16), wuq_ref[...], preferred_element_type=F32)
    rot0 = MLA_HEADS * 256
    for h in range(MLA_HEADS):
        c0 = h * 256
        qa_ref[0, :, c0:c0 + LANE] = (a[:, c0:c0 + LANE] * MLA_SCALE).astype(BF16)
        roped = a[:, c0 + LANE:c0 + 256] * cm + a[:, rot0 + h * LANE:rot0 + (h + 1) * LANE] * sm
        qa_ref[0, :, c0 + LANE:c0 + 256] = (roped * MLA_SCALE).astype(BF16)

    ckv = z[:, _OFF_CKV:_OFF_CKV + MLA_KV_LORA]
    ckvn = ckv * _rms_scale(jnp.sum(ckv * ckv, axis=-1, keepdims=True), MLA_KV_LORA) * kvn_ref[...]
    kvm = jnp.dot(ckvn.astype(BF16), wukv_ref[...], preferred_element_type=F32)
    kr = (z[:, _OFF_KR:_OFF_KR + LANE] * cm + z[:, _OFF_KRR:_OFF_KRR + LANE] * sm).astype(BF16)
    for h in range(MLA_HEADS):
        kva_ref[0, :, h * 256:h * 256 + LANE] = kvm[:, h * LANE:(h + 1) * LANE].astype(BF16)
        kva_ref[0, :, h * 256 + LANE:(h + 1) * 256] = kr

    glu_ref[0] = z[:, _OFF_GA:_OFF_GA + CONV_CH] * jax.nn.sigmoid(z[:, _OFF_GG:_OFF_GG + CONV_CH])

    gq = gq_ref[...]
    gqp = gqp_ref[...]
    for blk in range(GQA_HEADS):
        zq = z[:, _OFF_QC + blk * LANE:_OFF_QC + (blk + 1) * LANE]
        zr = z[:, _OFF_QCR + blk * LANE:_OFF_QCR + (blk + 1) * LANE]
        r = _rms_scale(jnp.sum(zq * zq, axis=-1, keepdims=True), GQA_HD)
        qg_ref[0, :, blk * LANE:(blk + 1) * LANE] = (
            (zq * gq * cg + zr * gqp * sg) * (r * GQA_SCALE)).astype(BF16)

    zk = z[:, _OFF_KC:_OFF_KC + LANE]
    zkr = z[:, _OFF_KCR:_OFF_KCR + LANE]
    low = lax.broadcasted_iota(jnp.int32, zk.shape, 1) < HALF
    sq = zk * zk
    r0 = _rms_scale(jnp.sum(jnp.where(low, sq, 0.0), axis=-1, keepdims=True), GQA_HD)
    r1 = _rms_scale(jnp.sum(jnp.where(low, 0.0, sq), axis=-1, keepdims=True), GQA_HD)
    kvg_ref[0, :, 0:LANE] = ((zk * gk_ref[...] * cg + zkr * gkp_ref[...] * sg)
                             * jnp.where(low, r0, r1)).astype(BF16)
    kvg_ref[0, :, LANE:2 * LANE] = z[:, _OFF_VC:_OFF_VC + LANE].astype(BF16)


def _mixer_pre(h, tab, lw):
    B, S, D = h.shape
    tm = _row_tile(S, 512)
    const = lambda b, i: (0, 0)
    row3 = lambda b, i: (b, i, 0)
    outs = [(MLA_HEADS * 256, BF16), (MLA_HEADS * 256, BF16), (CONV_CH, F32),
            (GQA_HEADS * LANE, BF16), (2 * LANE, BF16)]
    return pl.pallas_call(
        _pre_kernel,
        grid=(B, S // tm),
        in_specs=[pl.BlockSpec((1, tm, D), row3),
                  pl.BlockSpec((tm, 4 * LANE), lambda b, i: (i, 0)),
                  pl.BlockSpec(lw["w_ext"].shape, const),
                  pl.BlockSpec(lw["w_uq"].shape, const),
                  pl.BlockSpec(lw["w_ukv"].shape, const),
                  pl.BlockSpec((1, MLA_Q_LORA), const),
                  pl.BlockSpec((1, MLA_KV_LORA), const),
                  pl.BlockSpec((1, LANE), const),
                  pl.BlockSpec((1, LANE), const),
                  pl.BlockSpec((1, LANE), const),
                  pl.BlockSpec((1, LANE), const)],
        out_specs=[pl.BlockSpec((1, tm, w), row3) for w, _ in outs],
        out_shape=[jax.ShapeDtypeStruct((B, S, w), dt) for w, dt in outs],
        compiler_params=_cparams("parallel", "parallel"),
    )(h, tab, lw["w_ext"], lw["w_uq"], lw["w_ukv"], lw["qn"], lw["kvn"],
      lw["gq"], lw["gqp"], lw["gk"], lw["gkp"])


def _attn_kernel(q_ref, kv_ref, o_ref, m_sc, l_sc, acc_sc, s_sc, *, qw, k_off, v_off, kw, tkc):
    j = pl.program_id(3)
    tk = kv_ref.shape[1]
    units = [(i, c) for c in range(tk // tkc) for i in range(2)]
    n_col = tkc // LANE

    @pl.when(j == 0)
    def _():
        m_sc[...] = jnp.full(m_sc.shape, -jnp.inf, F32)
        l_sc[...] = jnp.zeros(l_sc.shape, F32)
        acc_sc[...] = jnp.zeros(acc_sc.shape, F32)

    def scores(u):
        i, c = units[u]
        q = q_ref[0, :, i * qw:(i + 1) * qw]
        k = kv_ref[0, c * tkc:(c + 1) * tkc, k_off[i]:k_off[i] + kw]
        return lax.dot_general(q, k, (((1,), (1,)), ((), ())), preferred_element_type=F32)

    s_sc[0] = scores(0)
    for u, (i, c) in enumerate(units):
        if u + 1 < len(units):
            s_sc[(u + 1) % 2] = scores(u + 1)
        cols = [s_sc[u % 2, :, cb * LANE:(cb + 1) * LANE] for cb in range(n_col)]
        mx = cols[0]
        for x in cols[1:]:
            mx = jnp.maximum(mx, x)
        m_prev = m_sc[i]
        m_new = jnp.maximum(m_prev, jnp.max(mx, axis=1, keepdims=True))
        alpha = jnp.exp2(m_prev - m_new)
        ps = [jnp.exp2(x - m_new) for x in cols]
        lsum = ps[0]
        for x in ps[1:]:
            lsum = lsum + x
        l_sc[i] = alpha * l_sc[i] + lsum
        p = jnp.concatenate([x.astype(BF16) for x in ps], axis=1)
        v = kv_ref[0, c * tkc:(c + 1) * tkc, v_off[i]:v_off[i] + LANE]
        acc_sc[i] = alpha * acc_sc[i] + jnp.dot(p, v, preferred_element_type=F32)
        m_sc[i] = m_new

    @pl.when(j == pl.num_programs(3) - 1)
    def _():
        o0 = acc_sc[0] / jnp.sum(l_sc[0], axis=1, keepdims=True)
        o1 = acc_sc[1] / jnp.sum(l_sc[1], axis=1, keepdims=True)
        low = lax.broadcasted_iota(jnp.int32, o0.shape, 1) < HALF
        o_ref[0] = jnp.where(low, o0, o1).astype(o_ref.dtype)


ATTN_TQ = 512
ATTN_TK = 2048
ATTN_TKC = 512


def _attention(q, kv, *, qw, kvw_block, kv_block_of_pair, k_off, v_off, kw):
    B, S, _ = q.shape
    n_pairs = 3
    tq = _row_tile(S, ATTN_TQ)
    tk = _row_tile(S, ATTN_TK)
    tkc = _row_tile(tk, ATTN_TKC)
    kern = functools.partial(_attn_kernel, qw=qw, k_off=k_off, v_off=v_off, kw=kw, tkc=tkc)
    return pl.pallas_call(
        kern,
        grid=(B, n_pairs, S // tq, S // tk),
        in_specs=[pl.BlockSpec((1, tq, 2 * qw), lambda b, p, i, j: (b, i, p)),
                  pl.BlockSpec((1, tk, kvw_block), lambda b, p, i, j: (b, j, kv_block_of_pair(p)))],
        out_specs=pl.BlockSpec((1, tq, LANE), lambda b, p, i, j: (b, i, p)),
        out_shape=jax.ShapeDtypeStruct((B, S, n_pairs * LANE), BF16),
        scratch_shapes=[pltpu.VMEM((2, tq, LANE), F32),
                        pltpu.VMEM((2, tq, LANE), F32),
                        pltpu.VMEM((2, tq, LANE), F32),
                        pltpu.VMEM((2, tq, tkc), F32)],
        compiler_params=_cparams("parallel", "parallel", "parallel", "arbitrary"),
    )(q, kv)


def _mla_attention(qa, kva):
    return _attention(qa, kva, qw=256, kvw_block=512, kv_block_of_pair=lambda p: p,
                      k_off=(0, 256), v_off=(0, 256), kw=256)


def _gqa_attention(qg, kvg):
    return _attention(qg, kvg, qw=LANE, kvw_block=2 * LANE, kv_block_of_pair=lambda p: 0,
                      k_off=(0, 0), v_off=(LANE, LANE), kw=LANE)


_HALO = 16


def _conv_kernel(prev_ref, cur_ref, next_ref, w_ref, b_ref, g_ref, beta_ref, o_ref, xe_sc):
    i = pl.program_id(1)
    tc = cur_ref.shape[1]
    first = i == 0
    last = i == pl.num_programs(1) - 1
    xe_sc[0:_HALO, :] = jnp.where(first, 0.0, prev_ref[0])
    xe_sc[_HALO:_HALO + tc, :] = cur_ref[0]
    xe_sc[_HALO + tc:_HALO + tc + _HALO, :] = jnp.where(last, 0.0, next_ref[0])
    acc = jnp.zeros((tc, CONV_CH), F32)
    for t in range(CONV_K):
        off = _HALO - CONV_K // 2 + t
        acc = acc + xe_sc[off:off + tc, :] * w_ref[t:t + 1, :]
    u = acc + b_ref[...]
    y = _ln_rows(u, g_ref[...], beta_ref[...])
    o_ref[0] = (y * jax.nn.sigmoid(y)).astype(o_ref.dtype)


def _conv_module(glu, lw):
    B, S, C = glu.shape
    tc = _row_tile(S, 512)
    nh = tc // _HALO
    n_halo_blocks = S // _HALO
    const = lambda b, i: (0, 0)
    return pl.pallas_call(
        _conv_kernel,
        grid=(B, S // tc),
        in_specs=[pl.BlockSpec((1, _HALO, C), lambda b, i: (b, jnp.maximum(i * nh - 1, 0), 0)),
                  pl.BlockSpec((1, tc, C), lambda b, i: (b, i, 0)),
                  pl.BlockSpec((1, _HALO, C), lambda b, i: (b, jnp.minimum((i + 1) * nh, n_halo_blocks - 1), 0)),
                  pl.BlockSpec((CONV_K, C), const),
                  pl.BlockSpec((1, C), const),
                  pl.BlockSpec((1, C), const),
                  pl.BlockSpec((1, C), const)],
        out_specs=pl.BlockSpec((1, tc, C), lambda b, i: (b, i, 0)),
        out_shape=jax.ShapeDtypeStruct((B, S, C), BF16),
        scratch_shapes=[pltpu.VMEM((tc + 2 * _HALO, C), F32)],
        compiler_params=_cparams("parallel", "parallel"),
    )(glu, glu, glu, lw["conv_w"], lw["conv_b"], lw["conv_ln_g"], lw["conv_ln_b"])


def _oproj_kernel(h_ref, oa_ref, ob_ref, oc_ref, wa_ref, wb_ref, wc_ref, g_ref, b_ref, o_ref):
    mix = jnp.dot(oa_ref[...], wa_ref[...], preferred_element_type=F32)
    mix = mix + jnp.dot(ob_ref[...], wb_ref[...], preferred_element_type=F32)
    mix = mix + jnp.dot(oc_ref[...], wc_ref[...], preferred_element_type=F32)
    o_ref[...] = _ln_rows(DN_ALPHA * h_ref[...] + mix, g_ref[...], b_ref[...])


def _out_proj(h, oa, ob, oc, lw):
    n, d = h.shape
    tm = _row_tile(n, 512)
    const = lambda i: (0, 0)
    row = lambda i: (i, 0)
    return pl.pallas_call(
        _oproj_kernel,
        grid=(n // tm,),
        in_specs=[pl.BlockSpec((tm, d), row),
                  pl.BlockSpec((tm, oa.shape[1]), row),
                  pl.BlockSpec((tm, ob.shape[1]), row),
                  pl.BlockSpec((tm, oc.shape[1]), row),
                  pl.BlockSpec(lw["wo_a"].shape, const),
                  pl.BlockSpec(lw["wo_b"].shape, const),
                  pl.BlockSpec(lw["wo_c"].shape, const),
                  pl.BlockSpec((1, d), const),
                  pl.BlockSpec((1, d), const)],
        out_specs=pl.BlockSpec((tm, d), row),
        out_shape=jax.ShapeDtypeStruct((n, d), F32),
        compiler_params=_cparams("parallel"),
    )(h, oa, ob, oc, lw["wo_a"], lw["wo_b"], lw["wo_c"], lw["ln1_g"], lw["ln1_b"])


def _memkv_kernel(x_ref, w_ref, o_ref):
    o_ref[...] = jnp.dot(x_ref[...].astype(BF16), w_ref[...], preferred_element_type=F32).astype(o_ref.dtype)


def _mem_kv(mem2d, w_ckv):
    n, d = mem2d.shape
    tm = _row_tile(n, 256)
    return pl.pallas_call(
        _memkv_kernel,
        grid=(n // tm,),
        in_specs=[pl.BlockSpec((tm, d), lambda i: (i, 0)),
                  pl.BlockSpec(w_ckv.shape, lambda i: (0, 0))],
        out_specs=pl.BlockSpec((tm, w_ckv.shape[1]), lambda i: (i, 0)),
        out_shape=jax.ShapeDtypeStruct((n, w_ckv.shape[1]), BF16),
        compiler_params=_cparams("parallel"),
    )(mem2d, w_ckv)


def _xattn_kernel(h_ref, k_ref, v_ref, wq_ref, wo_ref, g_ref, b_ref, o_ref):
    h = h_ref[0]
    q = (jnp.dot(h.astype(BF16), wq_ref[...], preferred_element_type=F32) * XA_SCALE).astype(BF16)
    heads = []
    for hd in range(XA_HEADS):
        sl = slice(hd * XA_HD, (hd + 1) * XA_HD)
        s = lax.dot_general(q[:, sl], k_ref[0, :, sl], (((1,), (1,)), ((), ())),
                            preferred_element_type=F32)
        p = jnp.exp(s - jnp.max(s, axis=1, keepdims=True))
        o = jnp.dot(p.astype(BF16), v_ref[0, :, sl], preferred_element_type=F32)
        heads.append((o / jnp.sum(p, axis=1, keepdims=True)).astype(BF16))
    o = jnp.concatenate(heads, axis=1)
    y = jnp.dot(o, wo_ref[...], preferred_element_type=F32)
    o_ref[0] = _ln_rows(DN_ALPHA * h + y, g_ref[...], b_ref[...])


def _cross_attention(h, kvmem, lw):
    B, S, D = h.shape
    M = kvmem.shape[1]
    tm = _row_tile(S, 512)
    const = lambda b, i: (0, 0)
    return pl.pallas_call(
        _xattn_kernel,
        grid=(B, S // tm),
        in_specs=[pl.BlockSpec((1, tm, D), lambda b, i: (b, i, 0)),
                  pl.BlockSpec((1, M, D), lambda b, i: (b, 0, 0)),
                  pl.BlockSpec((1, M, D), lambda b, i: (b, 0, 1)),
                  pl.BlockSpec((D, D), const),
                  pl.BlockSpec((D, D), const),
                  pl.BlockSpec((1, D), const),
                  pl.BlockSpec((1, D), const)],
        out_specs=pl.BlockSpec((1, tm, D), lambda b, i: (b, i, 0)),
        out_shape=jax.ShapeDtypeStruct((B, S, D), F32),
        compiler_params=_cparams("parallel", "parallel"),
    )(h, kvmem, kvmem, lw["w_cq"], lw["w_co"], lw["ln2_g"], lw["ln2_b"])


MOE_TILE = 1024
MOE_ROWS = 128


def _router_kernel(x_ref, wr_ref, bias_ref, gate_ref, pos_ref, cnt_ref, sel_sc):
    T = x_ref.shape[0]
    logits = lax.dot_general(wr_ref[...], x_ref[...], (((1,), (1,)), ((), ())),
                             precision=lax.Precision.HIGHEST, preferred_element_type=F32)
    scores = jax.nn.sigmoid(logits)
    sel = scores + bias_ref[...]
    sc = [scores[e:e + 1, :] for e in range(N_EXPERTS)]
    se = [sel[e:e + 1, :] for e in range(N_EXPERTS)]

    gs = []
    for g in range(N_GROUPS):
        m = [se[g * EXPERTS_PER_GROUP + k] for k in range(EXPERTS_PER_GROUP)]
        best = None
        for a in range(EXPERTS_PER_GROUP):
            for b in range(a + 1, EXPERTS_PER_GROUP):
                pair = m[a] + m[b]
                best = pair if best is None else jnp.maximum(best, pair)
        gs.append(best)
    in_group = []
    for g in range(N_GROUPS):
        ok = None
        for o in range(N_GROUPS):
            if o == g:
                continue
            c = (gs[g] > gs[o]) if o < g else (gs[g] >= gs[o])
            ok = c if ok is None else (ok & c)
        in_group.append(ok)
    chosen = []
    for e in range(N_EXPERTS):
        g = e // EXPERTS_PER_GROUP
        beaten = jnp.zeros((1, T), jnp.int32)
        for o in range(g * EXPERTS_PER_GROUP, (g + 1) * EXPERTS_PER_GROUP):
            if o == e:
                continue
            c = (se[o] > se[e]) if o > e else (se[o] >= se[e])
            beaten = beaten + c.astype(jnp.int32)
        chosen.append(in_group[g] & (beaten < 2))
    denom = jnp.zeros((1, T), F32)
    for e in range(N_EXPERTS):
        denom = denom + jnp.where(chosen[e], sc[e], 0.0)
    for e in range(N_EXPERTS):
        gate_ref[e:e + 1, :] = jnp.where(chosen[e], sc[e] / denom, 0.0)
        sel_sc[e:e + 1, :] = chosen[e].astype(F32)

    selm = sel_sc[...]
    earlier = (lax.broadcasted_iota(jnp.int32, (T, T), 0) < lax.broadcasted_iota(jnp.int32, (T, T), 1))
    rank = jnp.dot(selm.astype(BF16), earlier.astype(BF16), preferred_element_type=F32)
    pos_ref[...] = jnp.where(selm > 0.0, rank, -1.0)
    cnt = jnp.sum(selm, axis=1, keepdims=True)
    cnt_ref[0] = jnp.broadcast_to(cnt, (N_EXPERTS, LANE)).astype(jnp.int32)


def _router(x, w_rt, bias):
    n, d = x.shape
    T = _row_tile(n, MOE_TILE)
    nt = n // T
    gate, pos, cnt = pl.pallas_call(
        _router_kernel,
        grid=(nt,),
        in_specs=[pl.BlockSpec((T, d), lambda i: (i, 0)),
                  pl.BlockSpec((N_EXPERTS, d), lambda i: (0, 0)),
                  pl.BlockSpec((N_EXPERTS, 1), lambda i: (0, 0))],
        out_specs=[pl.BlockSpec((N_EXPERTS, T), lambda i: (0, i)),
                   pl.BlockSpec((N_EXPERTS, T), lambda i: (0, i)),
                   pl.BlockSpec((1, N_EXPERTS, LANE), lambda i: (i, 0, 0))],
        out_shape=[jax.ShapeDtypeStruct((N_EXPERTS, n), F32),
                   jax.ShapeDtypeStruct((N_EXPERTS, n), F32),
                   jax.ShapeDtypeStruct((nt, N_EXPERTS, LANE), jnp.int32)],
        scratch_shapes=[pltpu.VMEM((N_EXPERTS, T), F32)],
        compiler_params=_cparams("parallel"),
    )(x, w_rt, bias)
    return gate, pos, cnt[:, :, 0].reshape(-1)


def _moe_kernel(cnt_ref, h_ref, gate_ref, pos_ref, wg_ref, wu_ref, wd_ref, g_ref, b_ref, o_ref, xb_sc, acc_sc):
    i = pl.program_id(0)
    e = pl.program_id(1)
    T = h_ref.shape[0]

    @pl.when(e == 0)
    def _():
        xb_sc[...] = h_ref[...].astype(BF16)
        acc_sc[...] = jnp.zeros(acc_sc.shape, F32)

    n_rows = cnt_ref[i * N_EXPERTS + e]
    n_blocks = (n_rows + MOE_ROWS - 1) // MOE_ROWS
    pos_row = pos_ref[pl.ds(e, 1), :].astype(jnp.int32)
    gate_row = gate_ref[pl.ds(e, 1), :]
    slot = lax.broadcasted_iota(jnp.int32, (MOE_ROWS, T), 0)

    def block(jb, carry):
        hit = (pos_row - jb * MOE_ROWS) == slot
        onehot = hit.astype(BF16)
        xg = jnp.dot(onehot, xb_sc[...], preferred_element_type=F32).astype(BF16)
        hg = jnp.dot(xg, wg_ref[0], preferred_element_type=F32)
        hu = jnp.dot(xg, wu_ref[0], preferred_element_type=F32)
        act = (hg * jax.nn.sigmoid(hg) * hu).astype(BF16)
        y = jnp.dot(act, wd_ref[0], preferred_element_type=F32)
        g_rows = jnp.sum(jnp.where(hit, gate_row, 0.0), axis=1, keepdims=True)
        ys = (y * g_rows).astype(BF16)
        acc_sc[...] += lax.dot_general(onehot, ys, (((0,), (0,)), ((), ())), preferred_element_type=F32)
        return carry

    lax.fori_loop(0, n_blocks, block, 0)

    @pl.when(e == N_EXPERTS - 1)
    def _():
        o_ref[...] = _ln_rows(DN_ALPHA * h_ref[...] + acc_sc[...], g_ref[...], b_ref[...])


def _moe(h, gate, pos, cnt, lw):
    n, d = h.shape
    T = _row_tile(n, MOE_TILE)
    f = D_FF_EXPERT
    grid_spec = pltpu.PrefetchScalarGridSpec(
        num_scalar_prefetch=1,
        grid=(n // T, N_EXPERTS),
        in_specs=[pl.BlockSpec((T, d), lambda i, e, c: (i, 0)),
                  pl.BlockSpec((N_EXPERTS, T), lambda i, e, c: (0, i)),
                  pl.BlockSpec((N_EXPERTS, T), lambda i, e, c: (0, i)),
                  pl.BlockSpec((1, d, f), lambda i, e, c: (e, 0, 0)),
                  pl.BlockSpec((1, d, f), lambda i, e, c: (e, 0, 0)),
                  pl.BlockSpec((1, f, d), lambda i, e, c: (e, 0, 0)),
                  pl.BlockSpec((1, d), lambda i, e, c: (0, 0)),
                  pl.BlockSpec((1, d), lambda i, e, c: (0, 0))],
        out_specs=pl.BlockSpec((T, d), lambda i, e, c: (i, 0)),
        scratch_shapes=[pltpu.VMEM((T, d), BF16), pltpu.VMEM((T, d), F32)],
    )
    return pl.pallas_call(
        _moe_kernel,
        grid_spec=grid_spec,
        out_shape=jax.ShapeDtypeStruct((n, d), F32),
        compiler_params=_cparams("parallel", "arbitrary"),
    )(cnt, h, gate, pos, lw["w_gate"], lw["w_up"], lw["w_down"], lw["ln3_g"], lw["ln3_b"])


def _prep_layer(l, w):
    ext_idx, ext_sgn = _w_ext_index()
    uq_idx, uq_sgn = _w_uq_index()
    mla_w = MLA_HEADS * MLA_V
    gq = w["gqa_q_norm"][l]
    gk = w["gqa_k_norm"][l]
    row = lambda v: v.reshape(1, -1).astype(F32)
    return {
        "w_ext": _take_cols(w["w_in"][l], ext_idx, ext_sgn).astype(BF16),
        "w_uq": _take_cols(w["mla_w_uq"][l], uq_idx, uq_sgn).astype(BF16),
        "w_ukv": w["mla_w_ukv"][l][:, _w_ukv_index()].astype(BF16),
        "qn": row(w["mla_q_norm"][l]),
        "kvn": row(w["mla_kv_norm"][l]),
        "gq": row(jnp.tile(gq, 2)),
        "gqp": row(jnp.tile(_perm_gain(gq, GQA_HD // 4), 2)),
        "gk": row(jnp.tile(gk, 2)),
        "gkp": row(jnp.tile(_perm_gain(gk, GQA_HD // 4), 2)),
        "conv_w": w["conv_w"][l].astype(F32),
        "conv_b": row(w["conv_b"][l]),
        "conv_ln_g": row(w["conv_ln_g"][l]),
        "conv_ln_b": row(w["conv_ln_b"][l]),
        "wo_a": w["w_o"][l][:mla_w].astype(BF16),
        "wo_b": w["w_o"][l][mla_w:mla_w + CONV_CH].astype(BF16),
        "wo_c": w["w_o"][l][mla_w + CONV_CH:][_w_oc_index()].astype(BF16),
        "ln1_g": row(w["ln1_g"][l]), "ln1_b": row(w["ln1_b"][l]),
        "w_cq": w["w_cq"][l].astype(BF16),
        "w_ckv": w["w_ckv"][l].astype(BF16),
        "w_co": w["w_co"][l].astype(BF16),
        "ln2_g": row(w["ln2_g"][l]), "ln2_b": row(w["ln2_b"][l]),
        "w_gate": w["w_gate"][l].astype(BF16),
        "w_up": w["w_up"][l].astype(BF16),
        "w_down": w["w_down"][l].astype(BF16),
        "ln3_g": row(w["ln3_g"][l]), "ln3_b": row(w["ln3_b"][l]),
    }


def _trunk(x, mem, w, layers):
    B, S, D = x.shape
    n = B * S
    tab = _rope_tables(S)
    w_rt = w["w_router"].T.astype(F32)
    r_bias = w["router_bias"].reshape(N_EXPERTS, 1).astype(F32)
    h = _layer_norm_rows(x.reshape(n, D), w["ln_in_g"], w["ln_in_b"])
    mem2d = mem.reshape(B * mem.shape[1], D)
    for lw in layers:
        qa, kva, glu, qg, kvg = _mixer_pre(h.reshape(B, S, D), tab, lw)
        oa = _mla_attention(qa, kva)
        ob = _conv_module(glu, lw)
        oc = _gqa_attention(qg, kvg)
        h = _out_proj(h, oa.reshape(n, -1), ob.reshape(n, -1), oc.reshape(n, -1), lw)
        kvmem = _mem_kv(mem2d, lw["w_ckv"]).reshape(B, mem.shape[1], 2 * D)
        h = _cross_attention(h.reshape(B, S, D), kvmem, lw).reshape(n, D)
        gate, pos, cnt = _router(h, w_rt, r_bias)
        h = _moe(h, gate, pos, cnt, lw)
    return h.reshape(B, S, D)


def kernel(x_prompt, x_sample, mem_prompt, mem_sample, ln_in_g, ln_in_b, w_in, mla_q_norm, mla_w_uq,
           mla_kv_norm, mla_w_ukv, conv_w, conv_b, conv_ln_g, conv_ln_b, gqa_q_norm, gqa_k_norm, w_o,
           ln1_g, ln1_b, w_cq, w_ckv, w_co, ln2_g, ln2_b, w_router, router_bias, w_gate, w_up, w_down,
           ln3_g, ln3_b):
    w = dict(ln_in_g=ln_in_g, ln_in_b=ln_in_b, w_in=w_in, mla_q_norm=mla_q_norm, mla_w_uq=mla_w_uq,
             mla_kv_norm=mla_kv_norm, mla_w_ukv=mla_w_ukv, conv_w=conv_w, conv_b=conv_b,
             conv_ln_g=conv_ln_g, conv_ln_b=conv_ln_b, gqa_q_norm=gqa_q_norm, gqa_k_norm=gqa_k_norm,
             w_o=w_o, ln1_g=ln1_g, ln1_b=ln1_b, w_cq=w_cq, w_ckv=w_ckv, w_co=w_co, ln2_g=ln2_g,
             ln2_b=ln2_b, w_router=w_router, router_bias=router_bias, w_gate=w_gate, w_up=w_up,
             w_down=w_down, ln3_g=ln3_g, ln3_b=ln3_b)
    layers = [_prep_layer(l, w) for l in range(w_in.shape[0])]
    y_prompt = _trunk(x_prompt, mem_prompt, w, layers)
    y_sample = _trunk(x_sample, mem_sample, w, layers)
    return (y_prompt, y_sample)
```

```python
import functools

import numpy as np
import jax
import jax.numpy as jnp
from jax import lax
from jax.experimental import pallas as pl
from jax.experimental.pallas import tpu as pltpu

F32 = jnp.float32
BF16 = jnp.bfloat16

D_MODEL = 1024
DEPTH = 4
GRID_W = 64
ROPE_THETA = 10000.0
LN_EPS = 1e-5
RMS_EPS = 1e-6
MLA_HEADS = 6
MLA_Q_LORA = 256
MLA_KV_LORA = 128
MLA_NOPE = 64
MLA_ROPE = 32
MLA_V = 64
CONV_CH = 256
CONV_K = 31
GQA_HEADS = 6
GQA_KV_HEADS = 2
GQA_HD = 64
SPLITS = (MLA_Q_LORA, MLA_KV_LORA, MLA_ROPE, 2 * CONV_CH,
          GQA_HEADS * GQA_HD, GQA_KV_HEADS * GQA_HD, GQA_KV_HEADS * GQA_HD)
XA_HEADS = 4
XA_HD = D_MODEL // XA_HEADS
N_EXPERTS = 16
N_GROUPS = 4
EXPERTS_PER_GROUP = N_EXPERTS // N_GROUPS
D_FF_EXPERT = 512
DN_ALPHA = (2 * DEPTH) ** 0.25
LOG2E = 1.4426950408889634
MLA_SCALE = (MLA_NOPE + MLA_ROPE) ** -0.5 * LOG2E
GQA_SCALE = GQA_HD ** -0.5 * LOG2E
XA_SCALE = XA_HD ** -0.5

LANE = 128
HALF = LANE // 2
VMEM_LIMIT = 56 * 1024 * 1024

_OFF_CQ = 0
_OFF_CKV = 256
_OFF_KR = 384
_OFF_KRR = 512
_OFF_GA = 640
_OFF_GG = 896
_OFF_QC = 1152
_OFF_QCR = 1920
_OFF_KC = 2688
_OFF_KCR = 2816
_OFF_VC = 2944
_W_EXT = 3072
_GQA_BLOCK_HEADS = (0, 3, 1, 4, 2, 5)


def _cparams(*sem):
    return pltpu.CompilerParams(dimension_semantics=sem, vmem_limit_bytes=VMEM_LIMIT)


def _row_tile(n, want):
    t = min(n, want)
    while n % t:
        t //= 2
    return t


def _ln_rows(x, g, b):
    mu = jnp.mean(x, axis=-1, keepdims=True)
    xc = x - mu
    var = jnp.mean(xc * xc, axis=-1, keepdims=True)
    return xc * lax.rsqrt(var + LN_EPS) * g + b


def _take_cols(w, idx, sign=None):
    w_aug = jnp.concatenate([w, jnp.zeros((w.shape[0], 1), w.dtype)], axis=1)
    idx = np.where(idx < 0, w.shape[1], idx)
    out = w_aug[:, idx]
    if sign is not None:
        out = out * jnp.asarray(sign, w.dtype)[None, :]
    return out


def _rot_src(d, half):
    r = d % (2 * half)
    base = d - r
    if r < half:
        return base + r + half, -1.0
    return base + r - half, 1.0


def _w_ext_index():
    cut = np.concatenate([[0], np.cumsum(SPLITS)])
    c_cq, c_ckv, c_kr, c_glu, c_qc, c_kc, c_vc = [int(c) for c in cut[:-1]]
    idx = -np.ones((_W_EXT,), np.int64)
    sgn = np.ones((_W_EXT,), np.float32)
    idx[_OFF_CQ:_OFF_CQ + 256] = c_cq + np.arange(256)
    idx[_OFF_CKV:_OFF_CKV + 128] = c_ckv + np.arange(128)
    for d in range(MLA_ROPE):
        idx[_OFF_KR + d] = c_kr + d
        s, sg = _rot_src(d, MLA_ROPE // 2)
        idx[_OFF_KRR + d] = c_kr + s
        sgn[_OFF_KRR + d] = sg
    idx[_OFF_GA:_OFF_GA + 256] = c_glu + np.arange(256)
    idx[_OFF_GG:_OFF_GG + 256] = c_glu + 256 + np.arange(256)
    for blk, hq in enumerate(_GQA_BLOCK_HEADS):
        lane0 = 0 if hq < GQA_HEADS // GQA_KV_HEADS else HALF
        for d in range(GQA_HD):
            idx[_OFF_QC + blk * LANE + lane0 + d] = c_qc + hq * GQA_HD + d
            s, sg = _rot_src(d, GQA_HD // 4)
            idx[_OFF_QCR + blk * LANE + lane0 + d] = c_qc + hq * GQA_HD + s
            sgn[_OFF_QCR + blk * LANE + lane0 + d] = sg
    for c in range(GQA_KV_HEADS * GQA_HD):
        hk, d = divmod(c, GQA_HD)
        idx[_OFF_KC + c] = c_kc + c
        s, sg = _rot_src(d, GQA_HD // 4)
        idx[_OFF_KCR + c] = c_kc + hk * GQA_HD + s
        sgn[_OFF_KCR + c] = sg
    idx[_OFF_VC:_OFF_VC + 128] = c_vc + np.arange(128)
    return idx, sgn


def _w_uq_index():
    per = MLA_NOPE + MLA_ROPE
    idx = -np.ones((MLA_HEADS * 256 + MLA_HEADS * LANE,), np.int64)
    sgn = np.ones(idx.shape, np.float32)
    for h in range(MLA_HEADS):
        nope0 = h * 256 + (HALF if h % 2 == 0 else 0)
        idx[nope0:nope0 + MLA_NOPE] = h * per + np.arange(MLA_NOPE)
        for d in range(MLA_ROPE):
            idx[h * 256 + LANE + d] = h * per + MLA_NOPE + d
            s, sg = _rot_src(d, MLA_ROPE // 2)
            idx[MLA_HEADS * 256 + h * LANE + d] = h * per + MLA_NOPE + s
            sgn[MLA_HEADS * 256 + h * LANE + d] = sg
    return idx, sgn


def _w_ukv_index():
    per = MLA_NOPE + MLA_V
    idx = np.zeros((MLA_HEADS * per,), np.int64)
    for h in range(MLA_HEADS):
        nat = h * per + np.arange(per)
        idx[h * per:(h + 1) * per] = np.roll(nat, MLA_V) if h % 2 == 0 else nat
    return idx


def _w_oc_index():
    idx = np.zeros((GQA_HEADS * GQA_HD,), np.int64)
    for blk, hq in enumerate(_GQA_BLOCK_HEADS):
        c0 = (blk // 2) * LANE + (blk % 2) * HALF
        idx[c0:c0 + GQA_HD] = hq * GQA_HD + np.arange(GQA_HD)
    return idx


def _perm_gain(g, half):
    src = np.array([_rot_src(d, half)[0] for d in range(g.shape[-1])])
    return g[src]


def _rope_tables(S):
    def cs(pos, dim):
        inv = 1.0 / (ROPE_THETA ** (jnp.arange(0, dim, 2, dtype=F32) / dim))
        ang = pos[:, None] * inv[None, :]
        return jnp.cos(ang), jnp.sin(ang)

    c1, s1 = cs(jnp.arange(S, dtype=F32), MLA_ROPE)
    rows = S // GRID_W
    row = jnp.repeat(jnp.arange(rows, dtype=F32), GRID_W)
    col = jnp.tile(jnp.arange(GRID_W, dtype=F32), rows)
    cr, sr = cs(row, GQA_HD // 2)
    cc, sc = cs(col, GQA_HD // 2)
    zpad = jnp.zeros((S, LANE - MLA_ROPE), F32)
    cm = jnp.concatenate([c1, c1, zpad], axis=1)
    sm = jnp.concatenate([s1, s1, zpad], axis=1)
    cg = jnp.concatenate([cr, cr, cc, cc] * 2, axis=1)
    sg = jnp.concatenate([sr, sr, sc, sc] * 2, axis=1)
    return jnp.concatenate([cm, sm, cg, sg], axis=1)


def _ln_kernel(x_ref, g_ref, b_ref, o_ref):
    o_ref[...] = _ln_rows(x_ref[...], g_ref[...], b_ref[...])


def _layer_norm_rows(x, g, b):
    n, d = x.shape
    tm = _row_tile(n, 1024)
    return pl.pallas_call(
        _ln_kernel,
        grid=(n // tm,),
        in_specs=[pl.BlockSpec((tm, d), lambda i: (i, 0)),
                  pl.BlockSpec((1, d), lambda i: (0, 0)),
                  pl.BlockSpec((1, d), lambda i: (0, 0))],
        out_specs=pl.BlockSpec((tm, d), lambda i: (i, 0)),
        out_shape=jax.ShapeDtypeStruct((n, d), F32),
        compiler_params=_cparams("parallel"),
    )(x, g.reshape(1, d), b.reshape(1, d))


def _rms_scale(ss, width):
    return lax.rsqrt(ss * (1.0 / width) + RMS_EPS)


def _pre_kernel(h_ref, tab_ref, wext_ref, wuq_ref, wukv_ref, qn_ref, kvn_ref,
                gq_ref, gqp_ref, gk_ref, gkp_ref,
                qa_ref, kva_ref, glu_ref, qg_ref, kvg_ref):
    hb = h_ref[0].astype(BF16)
    z = jnp.dot(hb, wext_ref[...], preferred_element_type=F32)
    cm = tab_ref[:, 0:LANE]
    sm = tab_ref[:, LANE:2 * LANE]
    cg = tab_ref[:, 2 * LANE:3 * LANE]
    sg = tab_ref[:, 3 * LANE:4 * LANE]

    cq = z[:, _OFF_CQ:_OFF_CQ + MLA_Q_LORA]
    cqn = cq * _rms_scale(jnp.sum(cq * cq, axis=-1, keepdims=True), MLA_Q_LORA) * qn_ref[...]
    a = jnp.dot(cqn.astype(BF16), wuq_ref[...], preferred_element_type=F32)
    rot0 = MLA_HEADS * 256
    for h in range(MLA_HEADS):
        c0 = h * 256
        qa_ref[0, :, c0:c0 + LANE] = (a[:, c0:c0 + LANE] * MLA_SCALE).astype(BF16)
        roped = a[:, c0 + LANE:c0 + 256] * cm + a[:, rot0 + h * LANE:rot0 + (h + 1) * LANE] * sm
        qa_ref[0, :, c0 + LANE:c0 + 256] = (roped * MLA_SCALE).astype(BF16)

    ckv = z[:, _OFF_CKV:_OFF_CKV + MLA_KV_LORA]
    ckvn = ckv * _rms_scale(jnp.sum(ckv * ckv, axis=-1, keepdims=True), MLA_KV_LORA) * kvn_ref[...]
    kvm = jnp.dot(ckvn.astype(BF16), wukv_ref[...], preferred_element_type=F32)
    kr = (z[:, _OFF_KR:_OFF_KR + LANE] * cm + z[:, _OFF_KRR:_OFF_KRR + LANE] * sm).astype(BF16)
    for h in range(MLA_HEADS):
        kva_ref[0, :, h * 256:h * 256 + LANE] = kvm[:, h * LANE:(h + 1) * LANE].astype(BF16)
        kva_ref[0, :, h * 256 + LANE:(h + 1) * 256] = kr

    glu_ref[0] = z[:, _OFF_GA:_OFF_GA + CONV_CH] * jax.nn.sigmoid(z[:, _OFF_GG:_OFF_GG + CONV_CH])

    gq = gq_ref[...]
    gqp = gqp_ref[...]
    for blk in range(GQA_HEADS):
        zq = z[:, _OFF_QC + blk * LANE:_OFF_QC + (blk + 1) * LANE]
        zr = z[:, _OFF_QCR + blk * LANE:_OFF_QCR + (blk + 1) * LANE]
        r = _rms_scale(jnp.sum(zq * zq, axis=-1, keepdims=True), GQA_HD)
        qg_ref[0, :, blk * LANE:(blk + 1) * LANE] = (
            (zq * gq * cg + zr * gqp * sg) * (r * GQA_SCALE)).astype(BF16)

    zk = z[:, _OFF_KC:_OFF_KC + LANE]
    zkr = z[:, _OFF_KCR:_OFF_KCR + LANE]
    low = lax.broadcasted_iota(jnp.int32, zk.shape, 1) < HALF
    sq = zk * zk
    r0 = _rms_scale(jnp.sum(jnp.where(low, sq, 0.0), axis=-1, keepdims=True), GQA_HD)
    r1 = _rms_scale(jnp.sum(jnp.where(low, 0.0, sq), axis=-1, keepdims=True), GQA_HD)
    kvg_ref[0, :, 0:LANE] = ((zk * gk_ref[...] * cg + zkr * gkp_ref[...] * sg)
                             * jnp.where(low, r0, r1)).astype(BF16)
    kvg_ref[0, :, LANE:2 * LANE] = z[:, _OFF_VC:_OFF_VC + LANE].astype(BF16)


def _mixer_pre(h, tab, lw):
    B, S, D = h.shape
    tm = _row_tile(S, 512)
    const = lambda b, i: (0, 0)
    row3 = lambda b, i: (b, i, 0)
    outs = [(MLA_HEADS * 256, BF16), (MLA_HEADS * 256, BF16), (CONV_CH, F32),
            (GQA_HEADS * LANE, BF16), (2 * LANE, BF16)]
    return pl.pallas_call(
        _pre_kernel,
        grid=(B, S // tm),
        in_specs=[pl.BlockSpec((1, tm, D), row3),
                  pl.BlockSpec((tm, 4 * LANE), lambda b, i: (i, 0)),
                  pl.BlockSpec(lw["w_ext"].shape, const),
                  pl.BlockSpec(lw["w_uq"].shape, const),
                  pl.BlockSpec(lw["w_ukv"].shape, const),
                  pl.BlockSpec((1, MLA_Q_LORA), const),
                  pl.BlockSpec((1, MLA_KV_LORA), const),
                  pl.BlockSpec((1, LANE), const),
                  pl.BlockSpec((1, LANE), const),
                  pl.BlockSpec((1, LANE), const),
                  pl.BlockSpec((1, LANE), const)],
        out_specs=[pl.BlockSpec((1, tm, w), row3) for w, _ in outs],
        out_shape=[jax.ShapeDtypeStruct((B, S, w), dt) for w, dt in outs],
        compiler_params=_cparams("parallel", "parallel"),
    )(h, tab, lw["w_ext"], lw["w_uq"], lw["w_ukv"], lw["qn"], lw["kvn"],
      lw["gq"], lw["gqp"], lw["gk"], lw["gkp"])


def _attn_kernel(q_ref, kv_ref, o_ref, m_sc, l_sc, acc_sc, s_sc, *, qw, k_off, v_off, kw, tkc):
    j = pl.program_id(3)
    tk = kv_ref.shape[1]
    units = [(i, c) for c in range(tk // tkc) for i in range(2)]
    n_col = tkc // LANE

    @pl.when(j == 0)
    def _():
        m_sc[...] = jnp.full(m_sc.shape, -jnp.inf, F32)
        l_sc[...] = jnp.zeros(l_sc.shape, F32)
        acc_sc[...] = jnp.zeros(acc_sc.shape, F32)

    def scores(u):
        i, c = units[u]
        q = q_ref[0, :, i * qw:(i + 1) * qw]
        k = kv_ref[0, c * tkc:(c + 1) * tkc, k_off[i]:k_off[i] + kw]
        return lax.dot_general(q, k, (((1,), (1,)), ((), ())), preferred_element_type=F32)

    s_sc[0] = scores(0)
    for u, (i, c) in enumerate(units):
        if u + 1 < len(units):
            s_sc[(u + 1) % 2] = scores(u + 1)
        cols = [s_sc[u % 2, :, cb * LANE:(cb + 1) * LANE] for cb in range(n_col)]
        mx = cols[0]
        for x in cols[1:]:
            mx = jnp.maximum(mx, x)
        m_prev = m_sc[i]
        m_new = jnp.maximum(m_prev, jnp.max(mx, axis=1, keepdims=True))
        alpha = jnp.exp2(m_prev - m_new)
        ps = [jnp.exp2(x - m_new) for x in cols]
        lsum = ps[0]
        for x in ps[1:]:
            lsum = lsum + x
        l_sc[i] = alpha * l_sc[i] + lsum
        p = jnp.concatenate([x.astype(BF16) for x in ps], axis=1)
        v = kv_ref[0, c * tkc:(c + 1) * tkc, v_off[i]:v_off[i] + LANE]
        acc_sc[i] = alpha * acc_sc[i] + jnp.dot(p, v, preferred_element_type=F32)
        m_sc[i] = m_new

    @pl.when(j == pl.num_programs(3) - 1)
    def _():
        o0 = acc_sc[0] / jnp.sum(l_sc[0], axis=1, keepdims=True)
        o1 = acc_sc[1] / jnp.sum(l_sc[1], axis=1, keepdims=True)
        low = lax.broadcasted_iota(jnp.int32, o0.shape, 1) < HALF
        o_ref[0] = jnp.where(low, o0, o1).astype(o_ref.dtype)


ATTN_TQ = 1024
ATTN_TK = 2048
ATTN_TKC = 512


def _attention(q, kv, *, qw, kvw_block, kv_block_of_pair, k_off, v_off, kw):
    B, S, _ = q.shape
    n_pairs = 3
    tq = _row_tile(S, ATTN_TQ)
    tk = _row_tile(S, ATTN_TK)
    tkc = _row_tile(tk, ATTN_TKC)
    kern = functools.partial(_attn_kernel, qw=qw, k_off=k_off, v_off=v_off, kw=kw, tkc=tkc)
    return pl.pallas_call(
        kern,
        grid=(B, n_pairs, S // tq, S // tk),
        in_specs=[pl.BlockSpec((1, tq, 2 * qw), lambda b, p, i, j: (b, i, p)),
                  pl.BlockSpec((1, tk, kvw_block), lambda b, p, i, j: (b, j, kv_block_of_pair(p)))],
        out_specs=pl.BlockSpec((1, tq, LANE), lambda b, p, i, j: (b, i, p)),
        out_shape=jax.ShapeDtypeStruct((B, S, n_pairs * LANE), BF16),
        scratch_shapes=[pltpu.VMEM((2, tq, LANE), F32),
                        pltpu.VMEM((2, tq, LANE), F32),
                        pltpu.VMEM((2, tq, LANE), F32),
                        pltpu.VMEM((2, tq, tkc), F32)],
        compiler_params=_cparams("parallel", "parallel", "parallel", "arbitrary"),
    )(q, kv)


def _mla_attention(qa, kva):
    return _attention(qa, kva, qw=256, kvw_block=512, kv_block_of_pair=lambda p: p,
                      k_off=(0, 256), v_off=(0, 256), kw=256)


def _gqa_attention(qg, kvg):
    return _attention(qg, kvg, qw=LANE, kvw_block=2 * LANE, kv_block_of_pair=lambda p: 0,
                      k_off=(0, 0), v_off=(LANE, LANE), kw=LANE)


_HALO = 16


def _conv_kernel(prev_ref, cur_ref, next_ref, w_ref, b_ref, g_ref, beta_ref, o_ref, xe_sc):
    i = pl.program_id(1)
    tc = cur_ref.shape[1]
    first = i == 0
    last = i == pl.num_programs(1) - 1
    xe_sc[0:_HALO, :] = jnp.where(first, 0.0, prev_ref[0])
    xe_sc[_HALO:_HALO + tc, :] = cur_ref[0]
    xe_sc[_HALO + tc:_HALO + tc + _HALO, :] = jnp.where(last, 0.0, next_ref[0])
    acc = jnp.zeros((tc, CONV_CH), F32)
    for t in range(CONV_K):
        off = _HALO - CONV_K // 2 + t
        acc = acc + xe_sc[off:off + tc, :] * w_ref[t:t + 1, :]
    u = acc + b_ref[...]
    y = _ln_rows(u, g_ref[...], beta_ref[...])
    o_ref[0] = (y * jax.nn.sigmoid(y)).astype(o_ref.dtype)


def _conv_module(glu, lw):
    B, S, C = glu.shape
    tc = _row_tile(S, 512)
    nh = tc // _HALO
    n_halo_blocks = S // _HALO
    const = lambda b, i: (0, 0)
    return pl.pallas_call(
        _conv_kernel,
        grid=(B, S // tc),
        in_specs=[pl.BlockSpec((1, _HALO, C), lambda b, i: (b, jnp.maximum(i * nh - 1, 0), 0)),
                  pl.BlockSpec((1, tc, C), lambda b, i: (b, i, 0)),
                  pl.BlockSpec((1, _HALO, C), lambda b, i: (b, jnp.minimum((i + 1) * nh, n_halo_blocks - 1), 0)),
                  pl.BlockSpec((CONV_K, C), const),
                  pl.BlockSpec((1, C), const),
                  pl.BlockSpec((1, C), const),
                  pl.BlockSpec((1, C), const)],
        out_specs=pl.BlockSpec((1, tc, C), lambda b, i: (b, i, 0)),
        out_shape=jax.ShapeDtypeStruct((B, S, C), BF16),
        scratch_shapes=[pltpu.VMEM((tc + 2 * _HALO, C), F32)],
        compiler_params=_cparams("parallel", "parallel"),
    )(glu, glu, glu, lw["conv_w"], lw["conv_b"], lw["conv_ln_g"], lw["conv_ln_b"])


def _oproj_kernel(h_ref, oa_ref, ob_ref, oc_ref, wa_ref, wb_ref, wc_ref, g_ref, b_ref, o_ref):
    mix = jnp.dot(oa_ref[...], wa_ref[...], preferred_element_type=F32)
    mix = mix + jnp.dot(ob_ref[...], wb_ref[...], preferred_element_type=F32)
    mix = mix + jnp.dot(oc_ref[...], wc_ref[...], preferred_element_type=F32)
    o_ref[...] = _ln_rows(DN_ALPHA * h_ref[...] + mix, g_ref[...], b_ref[...])


def _out_proj(h, oa, ob, oc, lw):
    n, d = h.shape
    tm = _row_tile(n, 512)
    const = lambda i: (0, 0)
    row = lambda i: (i, 0)
    return pl.pallas_call(
        _oproj_kernel,
        grid=(n // tm,),
        in_specs=[pl.BlockSpec((tm, d), row),
                  pl.BlockSpec((tm, oa.shape[1]), row),
                  pl.BlockSpec((tm, ob.shape[1]), row),
                  pl.BlockSpec((tm, oc.shape[1]), row),
                  pl.BlockSpec(lw["wo_a"].shape, const),
                  pl.BlockSpec(lw["wo_b"].shape, const),
                  pl.BlockSpec(lw["wo_c"].shape, const),
                  pl.BlockSpec((1, d), const),
                  pl.BlockSpec((1, d), const)],
        out_specs=pl.BlockSpec((tm, d), row),
        out_shape=jax.ShapeDtypeStruct((n, d), F32),
        compiler_params=_cparams("parallel"),
    )(h, oa, ob, oc, lw["wo_a"], lw["wo_b"], lw["wo_c"], lw["ln1_g"], lw["ln1_b"])


def _memkv_kernel(x_ref, w_ref, o_ref):
    o_ref[...] = jnp.dot(x_ref[...].astype(BF16), w_ref[...], preferred_element_type=F32).astype(o_ref.dtype)


def _mem_kv(mem2d, w_ckv):
    n, d = mem2d.shape
    tm = _row_tile(n, 256)
    return pl.pallas_call(
        _memkv_kernel,
        grid=(n // tm,),
        in_specs=[pl.BlockSpec((tm, d), lambda i: (i, 0)),
                  pl.BlockSpec(w_ckv.shape, lambda i: (0, 0))],
        out_specs=pl.BlockSpec((tm, w_ckv.shape[1]), lambda i: (i, 0)),
        out_shape=jax.ShapeDtypeStruct((n, w_ckv.shape[1]), BF16),
        compiler_params=_cparams("parallel"),
    )(mem2d, w_ckv)


def _xattn_kernel(h_ref, k_ref, v_ref, wq_ref, wo_ref, g_ref, b_ref, o_ref):
    h = h_ref[0]
    q = (jnp.dot(h.astype(BF16), wq_ref[...], preferred_element_type=F32) * XA_SCALE).astype(BF16)
    heads = []
    for hd in range(XA_HEADS):
        sl = slice(hd * XA_HD, (hd + 1) * XA_HD)
        s = lax.dot_general(q[:, sl], k_ref[0, :, sl], (((1,), (1,)), ((), ())),
                            preferred_element_type=F32)
        p = jnp.exp(s - jnp.max(s, axis=1, keepdims=True))
        o = jnp.dot(p.astype(BF16), v_ref[0, :, sl], preferred_element_type=F32)
        heads.append((o / jnp.sum(p, axis=1, keepdims=True)).astype(BF16))
    o = jnp.concatenate(heads, axis=1)
    y = jnp.dot(o, wo_ref[...], preferred_element_type=F32)
    o_ref[0] = _ln_rows(DN_ALPHA * h + y, g_ref[...], b_ref[...])


def _cross_attention(h, kvmem, lw):
    B, S, D = h.shape
    M = kvmem.shape[1]
    tm = _row_tile(S, 512)
    const = lambda b, i: (0, 0)
    return pl.pallas_call(
        _xattn_kernel,
        grid=(B, S // tm),
        in_specs=[pl.BlockSpec((1, tm, D), lambda b, i: (b, i, 0)),
                  pl.BlockSpec((1, M, D), lambda b, i: (b, 0, 0)),
                  pl.BlockSpec((1, M, D), lambda b, i: (b, 0, 1)),
                  pl.BlockSpec((D, D), const),
                  pl.BlockSpec((D, D), const),
                  pl.BlockSpec((1, D), const),
                  pl.BlockSpec((1, D), const)],
        out_specs=pl.BlockSpec((1, tm, D), lambda b, i: (b, i, 0)),
        out_shape=jax.ShapeDtypeStruct((B, S, D), F32),
        compiler_params=_cparams("parallel", "parallel"),
    )(h, kvmem, kvmem, lw["w_cq"], lw["w_co"], lw["ln2_g"], lw["ln2_b"])


MOE_TILE = 1024
MOE_ROWS = 256


def _router_kernel(x_ref, wr_ref, bias_ref, gate_ref, pos_ref, cnt_ref, sel_sc):
    T = x_ref.shape[0]
    logits = lax.dot_general(wr_ref[...], x_ref[...], (((1,), (1,)), ((), ())),
                             precision=lax.Precision.HIGHEST, preferred_element_type=F32)
    scores = jax.nn.sigmoid(logits)
    sel = scores + bias_ref[...]
    sc = [scores[e:e + 1, :] for e in range(N_EXPERTS)]
    se = [sel[e:e + 1, :] for e in range(N_EXPERTS)]

    gs = []
    for g in range(N_GROUPS):
        m = [se[g * EXPERTS_PER_GROUP + k] for k in range(EXPERTS_PER_GROUP)]
        best = None
        for a in range(EXPERTS_PER_GROUP):
            for b in range(a + 1, EXPERTS_PER_GROUP):
                pair = m[a] + m[b]
                best = pair if best is None else jnp.maximum(best, pair)
        gs.append(best)
    in_group = []
    for g in range(N_GROUPS):
        ok = None
        for o in range(N_GROUPS):
            if o == g:
                continue
            c = (gs[g] > gs[o]) if o < g else (gs[g] >= gs[o])
            ok = c if ok is None else (ok & c)
        in_group.append(ok)
    chosen = []
    for e in range(N_EXPERTS):
        g = e // EXPERTS_PER_GROUP
        beaten = jnp.zeros((1, T), jnp.int32)
        for o in range(g * EXPERTS_PER_GROUP, (g + 1) * EXPERTS_PER_GROUP):
            if o == e:
                continue
            c = (se[o] > se[e]) if o > e else (se[o] >= se[e])
            beaten = beaten + c.astype(jnp.int32)
        chosen.append(in_group[g] & (beaten < 2))
    denom = jnp.zeros((1, T), F32)
    for e in range(N_EXPERTS):
        denom = denom + jnp.where(chosen[e], sc[e], 0.0)
    for e in range(N_EXPERTS):
        gate_ref[e:e + 1, :] = jnp.where(chosen[e], sc[e] / denom, 0.0)
        sel_sc[e:e + 1, :] = chosen[e].astype(F32)

    selm = sel_sc[...]
    earlier = (lax.broadcasted_iota(jnp.int32, (T, T), 0) < lax.broadcasted_iota(jnp.int32, (T, T), 1))
    rank = jnp.dot(selm.astype(BF16), earlier.astype(BF16), preferred_element_type=F32)
    pos_ref[...] = jnp.where(selm > 0.0, rank, -1.0)
    cnt = jnp.sum(selm, axis=1, keepdims=True)
    cnt_ref[0] = jnp.broadcast_to(cnt, (N_EXPERTS, LANE)).astype(jnp.int32)


def _router(x, w_rt, bias):
    n, d = x.shape
    T = _row_tile(n, MOE_TILE)
    nt = n // T
    gate, pos, cnt = pl.pallas_call(
        _router_kernel,
        grid=(nt,),
        in_specs=[pl.BlockSpec((T, d), lambda i: (i, 0)),
                  pl.BlockSpec((N_EXPERTS, d), lambda i: (0, 0)),
                  pl.BlockSpec((N_EXPERTS, 1), lambda i: (0, 0))],
        out_specs=[pl.BlockSpec((N_EXPERTS, T), lambda i: (0, i)),
                   pl.BlockSpec((N_EXPERTS, T), lambda i: (0, i)),
                   pl.BlockSpec((1, N_EXPERTS, LANE), lambda i: (i, 0, 0))],
        out_shape=[jax.ShapeDtypeStruct((N_EXPERTS, n), F32),
                   jax.ShapeDtypeStruct((N_EXPERTS, n), F32),
                   jax.ShapeDtypeStruct((nt, N_EXPERTS, LANE), jnp.int32)],
        scratch_shapes=[pltpu.VMEM((N_EXPERTS, T), F32)],
        compiler_params=_cparams("parallel"),
    )(x, w_rt, bias)
    return gate, pos, cnt[:, :, 0].reshape(-1)


def _moe_kernel(cnt_ref, h_ref, gate_ref, pos_ref, wg_ref, wu_ref, wd_ref, g_ref, b_ref, o_ref, xb_sc, acc_sc):
    i = pl.program_id(0)
    e = pl.program_id(1)
    T = h_ref.shape[0]

    @pl.when(e == 0)
    def _():
        xb_sc[...] = h_ref[...].astype(BF16)
        acc_sc[...] = jnp.zeros(acc_sc.shape, F32)

    n_rows = cnt_ref[i * N_EXPERTS + e]
    n_blocks = (n_rows + MOE_ROWS - 1) // MOE_ROWS
    pos_row = pos_ref[pl.ds(e, 1), :].astype(jnp.int32)
    gate_row = gate_ref[pl.ds(e, 1), :]
    slot = lax.broadcasted_iota(jnp.int32, (MOE_ROWS, T), 0)

    def block(jb, carry):
        hit = (pos_row - jb * MOE_ROWS) == slot
        onehot = hit.astype(BF16)
        xg = jnp.dot(onehot, xb_sc[...], preferred_element_type=F32).astype(BF16)
        hg = jnp.dot(xg, wg_ref[0], preferred_element_type=F32)
        hu = jnp.dot(xg, wu_ref[0], preferred_element_type=F32)
        act = (hg * jax.nn.sigmoid(hg) * hu).astype(BF16)
        y = jnp.dot(act, wd_ref[0], preferred_element_type=F32)
        g_rows = jnp.sum(jnp.where(hit, gate_row, 0.0), axis=1, keepdims=True)
        ys = (y * g_rows).astype(BF16)
        acc_sc[...] += lax.dot_general(onehot, ys, (((0,), (0,)), ((), ())), preferred_element_type=F32)
        return carry

    lax.fori_loop(0, n_blocks, block, 0)

    @pl.when(e == N_EXPERTS - 1)
    def _():
        o_ref[...] = _ln_rows(DN_ALPHA * h_ref[...] + acc_sc[...], g_ref[...], b_ref[...])


def _moe(h, gate, pos, cnt, lw):
    n, d = h.shape
    T = _row_tile(n, MOE_TILE)
    f = D_FF_EXPERT
    grid_spec = pltpu.PrefetchScalarGridSpec(
        num_scalar_prefetch=1,
        grid=(n // T, N_EXPERTS),
        in_specs=[pl.BlockSpec((T, d), lambda i, e, c: (i, 0)),
                  pl.BlockSpec((N_EXPERTS, T), lambda i, e, c: (0, i)),
                  pl.BlockSpec((N_EXPERTS, T), lambda i, e, c: (0, i)),
                  pl.BlockSpec((1, d, f), lambda i, e, c: (e, 0, 0)),
                  pl.BlockSpec((1, d, f), lambda i, e, c: (e, 0, 0)),
                  pl.BlockSpec((1, f, d), lambda i, e, c: (e, 0, 0)),
                  pl.BlockSpec((1, d), lambda i, e, c: (0, 0)),
                  pl.BlockSpec((1, d), lambda i, e, c: (0, 0))],
        out_specs=pl.BlockSpec((T, d), lambda i, e, c: (i, 0)),
        scratch_shapes=[pltpu.VMEM((T, d), BF16), pltpu.VMEM((T, d), F32)],
    )
    return pl.pallas_call(
        _moe_kernel,
        grid_spec=grid_spec,
        out_shape=jax.ShapeDtypeStruct((n, d), F32),
        compiler_params=_cparams("parallel", "arbitrary"),
    )(cnt, h, gate, pos, lw["w_gate"], lw["w_up"], lw["w_down"], lw["ln3_g"], lw["ln3_b"])


def _prep_layer(l, w):
    ext_idx, ext_sgn = _w_ext_index()
    uq_idx, uq_sgn = _w_uq_index()
    mla_w = MLA_HEADS * MLA_V
    gq = w["gqa_q_norm"][l]
    gk = w["gqa_k_norm"][l]
    row = lambda v: v.reshape(1, -1).astype(F32)
    return {
        "w_ext": _take_cols(w["w_in"][l], ext_idx, ext_sgn).astype(BF16),
        "w_uq": _take_cols(w["mla_w_uq"][l], uq_idx, uq_sgn).astype(BF16),
        "w_ukv": w["mla_w_ukv"][l][:, _w_ukv_index()].astype(BF16),
        "qn": row(w["mla_q_norm"][l]),
        "kvn": row(w["mla_kv_norm"][l]),
        "gq": row(jnp.tile(gq, 2)),
        "gqp": row(jnp.tile(_perm_gain(gq, GQA_HD // 4), 2)),
        "gk": row(jnp.tile(gk, 2)),
        "gkp": row(jnp.tile(_perm_gain(gk, GQA_HD // 4), 2)),
        "conv_w": w["conv_w"][l].astype(F32),
        "conv_b": row(w["conv_b"][l]),
        "conv_ln_g": row(w["conv_ln_g"][l]),
        "conv_ln_b": row(w["conv_ln_b"][l]),
        "wo_a": w["w_o"][l][:mla_w].astype(BF16),
        "wo_b": w["w_o"][l][mla_w:mla_w + CONV_CH].astype(BF16),
        "wo_c": w["w_o"][l][mla_w + CONV_CH:][_w_oc_index()].astype(BF16),
        "ln1_g": row(w["ln1_g"][l]), "ln1_b": row(w["ln1_b"][l]),
        "w_cq": w["w_cq"][l].astype(BF16),
        "w_ckv": w["w_ckv"][l].astype(BF16),
        "w_co": w["w_co"][l].astype(BF16),
        "ln2_g": row(w["ln2_g"][l]), "ln2_b": row(w["ln2_b"][l]),
        "w_gate": w["w_gate"][l].astype(BF16),
        "w_up": w["w_up"][l].astype(BF16),
        "w_down": w["w_down"][l].astype(BF16),
        "ln3_g": row(w["ln3_g"][l]), "ln3_b": row(w["ln3_b"][l]),
    }


def _trunk(x, mem, w, layers):
    B, S, D = x.shape
    n = B * S
    tab = _rope_tables(S)
    w_rt = w["w_router"].T.astype(F32)
    r_bias = w["router_bias"].reshape(N_EXPERTS, 1).astype(F32)
    h = _layer_norm_rows(x.reshape(n, D), w["ln_in_g"], w["ln_in_b"])
    mem2d = mem.reshape(B * mem.shape[1], D)
    for lw in layers:
        qa, kva, glu, qg, kvg = _mixer_pre(h.reshape(B, S, D), tab, lw)
        oa = _mla_attention(qa, kva)
        ob = _conv_module(glu, lw)
        oc = _gqa_attention(qg, kvg)
        h = _out_proj(h, oa.reshape(n, -1), ob.reshape(n, -1), oc.reshape(n, -1), lw)
        kvmem = _mem_kv(mem2d, lw["w_ckv"]).reshape(B, mem.shape[1], 2 * D)
        h = _cross_attention(h.reshape(B, S, D), kvmem, lw).reshape(n, D)
        gate, pos, cnt = _router(h, w_rt, r_bias)
        h = _moe(h, gate, pos, cnt, lw)
    return h.reshape(B, S, D)


def kernel(x_prompt, x_sample, mem_prompt, mem_sample, ln_in_g, ln_in_b, w_in, mla_q_norm, mla_w_uq,
           mla_kv_norm, mla_w_ukv, conv_w, conv_b, conv_ln_g, conv_ln_b, gqa_q_norm, gqa_k_norm, w_o,
           ln1_g, ln1_b, w_cq, w_ckv, w_co, ln2_g, ln2_b, w_router, router_bias, w_gate, w_up, w_down,
           ln3_g, ln3_b):
    w = dict(ln_in_g=ln_in_g, ln_in_b=ln_in_b, w_in=w_in, mla_q_norm=mla_q_norm, mla_w_uq=mla_w_uq,
             mla_kv_norm=mla_kv_norm, mla_w_ukv=mla_w_ukv, conv_w=conv_w, conv_b=conv_b,
             conv_ln_g=conv_ln_g, conv_ln_b=conv_ln_b, gqa_q_norm=gqa_q_norm, gqa_k_norm=gqa_k_norm,
             w_o=w_o, ln1_g=ln1_g, ln1_b=ln1_b, w_cq=w_cq, w_ckv=w_ckv, w_co=w_co, ln2_g=ln2_g,
             ln2_b=ln2_b, w_router=w_router, router_bias=router_bias, w_gate=w_gate, w_up=w_up,
             w_down=w_down, ln3_g=ln3_g, ln3_b=ln3_b)
    layers = [_prep_layer(l, w) for l in range(w_in.shape[0])]
    y_prompt = _trunk(x_prompt, mem_prompt, w, layers)
    y_sample = _trunk(x_sample, mem_sample, w, layers)
    return (y_prompt, y_sample)
```

```python
import functools

import numpy as np
import jax
import jax.numpy as jnp
from jax import lax
from jax.experimental import pallas as pl
from jax.experimental.pallas import tpu as pltpu

F32 = jnp.float32
BF16 = jnp.bfloat16

D_MODEL = 1024
DEPTH = 4
GRID_W = 64
ROPE_THETA = 10000.0
LN_EPS = 1e-5
RMS_EPS = 1e-6
MLA_HEADS = 6
MLA_Q_LORA = 256
MLA_KV_LORA = 128
MLA_NOPE = 64
MLA_ROPE = 32
MLA_V = 64
CONV_CH = 256
CONV_K = 31
GQA_HEADS = 6
GQA_KV_HEADS = 2
GQA_HD = 64
SPLITS = (MLA_Q_LORA, MLA_KV_LORA, MLA_ROPE, 2 * CONV_CH,
          GQA_HEADS * GQA_HD, GQA_KV_HEADS * GQA_HD, GQA_KV_HEADS * GQA_HD)
XA_HEADS = 4
XA_HD = D_MODEL // XA_HEADS
N_EXPERTS = 16
N_GROUPS = 4
EXPERTS_PER_GROUP = N_EXPERTS // N_GROUPS
D_FF_EXPERT = 512
DN_ALPHA = (2 * DEPTH) ** 0.25
LOG2E = 1.4426950408889634
MLA_SCALE = (MLA_NOPE + MLA_ROPE) ** -0.5 * LOG2E
GQA_SCALE = GQA_HD ** -0.5 * LOG2E
XA_SCALE = XA_HD ** -0.5

LANE = 128
HALF = LANE // 2
VMEM_LIMIT = 56 * 1024 * 1024

_OFF_CQ = 0
_OFF_CKV = 256
_OFF_KR = 384
_OFF_KRR = 512
_OFF_GA = 640
_OFF_GG = 896
_OFF_QC = 1152
_OFF_QCR = 1920
_OFF_KC = 2688
_OFF_KCR = 2816
_OFF_VC = 2944
_W_EXT = 3072
_GQA_BLOCK_HEADS = (0, 3, 1, 4, 2, 5)


def _cparams(*sem):
    return pltpu.CompilerParams(dimension_semantics=sem, vmem_limit_bytes=VMEM_LIMIT)


def _row_tile(n, want):
    t = min(n, want)
    while n % t:
        t //= 2
    return t


def _ln_rows(x, g, b):
    mu = jnp.mean(x, axis=-1, keepdims=True)
    xc = x - mu
    var = jnp.mean(xc * xc, axis=-1, keepdims=True)
    return xc * lax.rsqrt(var + LN_EPS) * g + b


def _take_cols(w, idx, sign=None):
    w_aug = jnp.concatenate([w, jnp.zeros((w.shape[0], 1), w.dtype)], axis=1)
    idx = np.where(idx < 0, w.shape[1], idx)
    out = w_aug[:, idx]
    if sign is not None:
        out = out * jnp.asarray(sign, w.dtype)[None, :]
    return out


def _rot_src(d, half):
    r = d % (2 * half)
    base = d - r
    if r < half:
        return base + r + half, -1.0
    return base + r - half, 1.0


def _w_ext_index():
    cut = np.concatenate([[0], np.cumsum(SPLITS)])
    c_cq, c_ckv, c_kr, c_glu, c_qc, c_kc, c_vc = [int(c) for c in cut[:-1]]
    idx = -np.ones((_W_EXT,), np.int64)
    sgn = np.ones((_W_EXT,), np.float32)
    idx[_OFF_CQ:_OFF_CQ + 256] = c_cq + np.arange(256)
    idx[_OFF_CKV:_OFF_CKV + 128] = c_ckv + np.arange(128)
    for d in range(MLA_ROPE):
        idx[_OFF_KR + d] = c_kr + d
        s, sg = _rot_src(d, MLA_ROPE // 2)
        idx[_OFF_KRR + d] = c_kr + s
        sgn[_OFF_KRR + d] = sg
    idx[_OFF_GA:_OFF_GA + 256] = c_glu + np.arange(256)
    idx[_OFF_GG:_OFF_GG + 256] = c_glu + 256 + np.arange(256)
    for blk, hq in enumerate(_GQA_BLOCK_HEADS):
        lane0 = 0 if hq < GQA_HEADS // GQA_KV_HEADS else HALF
        for d in range(GQA_HD):
            idx[_OFF_QC + blk * LANE + lane0 + d] = c_qc + hq * GQA_HD + d
            s, sg = _rot_src(d, GQA_HD // 4)
            idx[_OFF_QCR + blk * LANE + lane0 + d] = c_qc + hq * GQA_HD + s
            sgn[_OFF_QCR + blk * LANE + lane0 + d] = sg
    for c in range(GQA_KV_HEADS * GQA_HD):
        hk, d = divmod(c, GQA_HD)
        idx[_OFF_KC + c] = c_kc + c
        s, sg = _rot_src(d, GQA_HD // 4)
        idx[_OFF_KCR + c] = c_kc + hk * GQA_HD + s
        sgn[_OFF_KCR + c] = sg
    idx[_OFF_VC:_OFF_VC + 128] = c_vc + np.arange(128)
    return idx, sgn


def _w_uq_index():
    per = MLA_NOPE + MLA_ROPE
    idx = -np.ones((MLA_HEADS * 256 + MLA_HEADS * LANE,), np.int64)
    sgn = np.ones(idx.shape, np.float32)
    for h in range(MLA_HEADS):
        nope0 = h * 256 + (HALF if h % 2 == 0 else 0)
        idx[nope0:nope0 + MLA_NOPE] = h * per + np.arange(MLA_NOPE)
        for d in range(MLA_ROPE):
            idx[h * 256 + LANE + d] = h * per + MLA_NOPE + d
            s, sg = _rot_src(d, MLA_ROPE // 2)
            idx[MLA_HEADS * 256 + h * LANE + d] = h * per + MLA_NOPE + s
            sgn[MLA_HEADS * 256 + h * LANE + d] = sg
    return idx, sgn


def _w_ukv_index():
    per = MLA_NOPE + MLA_V
    idx = np.zeros((MLA_HEADS * per,), np.int64)
    for h in range(MLA_HEADS):
        nat = h * per + np.arange(per)
        idx[h * per:(h + 1) * per] = np.roll(nat, MLA_V) if h % 2 == 0 else nat
    return idx


def _w_oc_index():
    idx = np.zeros((GQA_HEADS * GQA_HD,), np.int64)
    for blk, hq in enumerate(_GQA_BLOCK_HEADS):
        c0 = (blk // 2) * LANE + (blk % 2) * HALF
        idx[c0:c0 + GQA_HD] = hq * GQA_HD + np.arange(GQA_HD)
    return idx


def _perm_gain(g, half):
    src = np.array([_rot_src(d, half)[0] for d in range(g.shape[-1])])
    return g[src]


def _rope_tables(S):
    def cs(pos, dim):
        inv = 1.0 / (ROPE_THETA ** (jnp.arange(0, dim, 2, dtype=F32) / dim))
        ang = pos[:, None] * inv[None, :]
        return jnp.cos(ang), jnp.sin(ang)

    c1, s1 = cs(jnp.arange(S, dtype=F32), MLA_ROPE)
    rows = S // GRID_W
    row = jnp.repeat(jnp.arange(rows, dtype=F32), GRID_W)
    col = jnp.tile(jnp.arange(GRID_W, dtype=F32), rows)
    cr, sr = cs(row, GQA_HD // 2)
    cc, sc = cs(col, GQA_HD // 2)
    zpad = jnp.zeros((S, LANE - MLA_ROPE), F32)
    cm = jnp.concatenate([c1, c1, zpad], axis=1)
    sm = jnp.concatenate([s1, s1, zpad], axis=1)
    cg = jnp.concatenate([cr, cr, cc, cc] * 2, axis=1)
    sg = jnp.concatenate([sr, sr, sc, sc] * 2, axis=1)
    return jnp.concatenate([cm, sm, cg, sg], axis=1)


def _ln_kernel(x_ref, g_ref, b_ref, o_ref):
    o_ref[...] = _ln_rows(x_ref[...], g_ref[...], b_ref[...])


def _layer_norm_rows(x, g, b):
    n, d = x.shape
    tm = _row_tile(n, 1024)
    return pl.pallas_call(
        _ln_kernel,
        grid=(n // tm,),
        in_specs=[pl.BlockSpec((tm, d), lambda i: (i, 0)),
                  pl.BlockSpec((1, d), lambda i: (0, 0)),
                  pl.BlockSpec((1, d), lambda i: (0, 0))],
        out_specs=pl.BlockSpec((tm, d), lambda i: (i, 0)),
        out_shape=jax.ShapeDtypeStruct((n, d), F32),
        compiler_params=_cparams("parallel"),
    )(x, g.reshape(1, d), b.reshape(1, d))


def _rms_scale(ss, width):
    return lax.rsqrt(ss * (1.0 / width) + RMS_EPS)


def _pre_kernel(h_ref, tab_ref, wext_ref, wuq_ref, wukv_ref, qn_ref, kvn_ref,
                gq_ref, gqp_ref, gk_ref, gkp_ref,
                qa_ref, kva_ref, glu_ref, qg_ref, kvg_ref):
    hb = h_ref[0].astype(BF16)
    z = jnp.dot(hb, wext_ref[...], preferred_element_type=F32)
    cm = tab_ref[:, 0:LANE]
    sm = tab_ref[:, LANE:2 * LANE]
    cg = tab_ref[:, 2 * LANE:3 * LANE]
    sg = tab_ref[:, 3 * LANE:4 * LANE]

    cq = z[:, _OFF_CQ:_OFF_CQ + MLA_Q_LORA]
    cqn = cq * _rms_scale(jnp.sum(cq * cq, axis=-1, keepdims=True), MLA_Q_LORA) * qn_ref[...]
    a = jnp.dot(cqn.astype(BF16), wuq_ref[...], preferred_element_type=F32)
    rot0 = MLA_HEADS * 256
    for h in range(MLA_HEADS):
        c0 = h * 256
        qa_ref[0, :, c0:c0 + LANE] = (a[:, c0:c0 + LANE] * MLA_SCALE).astype(BF16)
        roped = a[:, c0 + LANE:c0 + 256] * cm + a[:, rot0 + h * LANE:rot0 + (h + 1) * LANE] * sm
        qa_ref[0, :, c0 + LANE:c0 + 256] = (roped * MLA_SCALE).astype(BF16)

    ckv = z[:, _OFF_CKV:_OFF_CKV + MLA_KV_LORA]
    ckvn = ckv * _rms_scale(jnp.sum(ckv * ckv, axis=-1, keepdims=True), MLA_KV_LORA) * kvn_ref[...]
    kvm = jnp.dot(ckvn.astype(BF16), wukv_ref[...], preferred_element_type=F32)
    kr = (z[:, _OFF_KR:_OFF_KR + LANE] * cm + z[:, _OFF_KRR:_OFF_KRR + LANE] * sm).astype(BF16)
    for h in range(MLA_HEADS):
        kva_ref[0, :, h * 256:h * 256 + LANE] = kvm[:, h * LANE:(h + 1) * LANE].astype(BF16)
        kva_ref[0, :, h * 256 + LANE:(h + 1) * 256] = kr

    glu_ref[0] = z[:, _OFF_GA:_OFF_GA + CONV_CH] * jax.nn.sigmoid(z[:, _OFF_GG:_OFF_GG + CONV_CH])

    gq = gq_ref[...]
    gqp = gqp_ref[...]
    for blk in range(GQA_HEADS):
        zq = z[:, _OFF_QC + blk * LANE:_OFF_QC + (blk + 1) * LANE]
        zr = z[:, _OFF_QCR + blk * LANE:_OFF_QCR + (blk + 1) * LANE]
        r = _rms_scale(jnp.sum(zq * zq, axis=-1, keepdims=True), GQA_HD)
        qg_ref[0, :, blk * LANE:(blk + 1) * LANE] = (
            (zq * gq * cg + zr * gqp * sg) * (r * GQA_SCALE)).astype(BF16)

    zk = z[:, _OFF_KC:_OFF_KC + LANE]
    zkr = z[:, _OFF_KCR:_OFF_KCR + LANE]
    low = lax.broadcasted_iota(jnp.int32, zk.shape, 1) < HALF
    sq = zk * zk
    r0 = _rms_scale(jnp.sum(jnp.where(low, sq, 0.0), axis=-1, keepdims=True), GQA_HD)
    r1 = _rms_scale(jnp.sum(jnp.where(low, 0.0, sq), axis=-1, keepdims=True), GQA_HD)
    kvg_ref[0, :, 0:LANE] = ((zk * gk_ref[...] * cg + zkr * gkp_ref[...] * sg)
                             * jnp.where(low, r0, r1)).astype(BF16)
    kvg_ref[0, :, LANE:2 * LANE] = z[:, _OFF_VC:_OFF_VC + LANE].astype(BF16)


def _mixer_pre(h, tab, lw):
    B, S, D = h.shape
    tm = _row_tile(S, 512)
    const = lambda b, i: (0, 0)
    row3 = lambda b, i: (b, i, 0)
    outs = [(MLA_HEADS * 256, BF16), (MLA_HEADS * 256, BF16), (CONV_CH, F32),
            (GQA_HEADS * LANE, BF16), (2 * LANE, BF16)]
    return pl.pallas_call(
        _pre_kernel,
        grid=(B, S // tm),
        in_specs=[pl.BlockSpec((1, tm, D), row3),
                  pl.BlockSpec((tm, 4 * LANE), lambda b, i: (i, 0)),
                  pl.BlockSpec(lw["w_ext"].shape, const),
                  pl.BlockSpec(lw["w_uq"].shape, const),
                  pl.BlockSpec(lw["w_ukv"].shape, const),
                  pl.BlockSpec((1, MLA_Q_LORA), const),
                  pl.BlockSpec((1, MLA_KV_LORA), const),
                  pl.BlockSpec((1, LANE), const),
                  pl.BlockSpec((1, LANE), const),
                  pl.BlockSpec((1, LANE), const),
                  pl.BlockSpec((1, LANE), const)],
        out_specs=[pl.BlockSpec((1, tm, w), row3) for w, _ in outs],
        out_shape=[jax.ShapeDtypeStruct((B, S, w), dt) for w, dt in outs],
        compiler_params=_cparams("parallel", "parallel"),
    )(h, tab, lw["w_ext"], lw["w_uq"], lw["w_ukv"], lw["qn"], lw["kvn"],
      lw["gq"], lw["gqp"], lw["gk"], lw["gkp"])


def _attn_kernel(q_ref, kv_ref, o_ref, m_sc, l_sc, acc_sc, s_sc, *, qw, k_off, v_off, kw, tkc):
    j = pl.program_id(3)
    tk = kv_ref.shape[1]
    units = [(i, c) for c in range(tk // tkc) for i in range(2)]
    n_col = tkc // LANE

    @pl.when(j == 0)
    def _():
        m_sc[...] = jnp.full(m_sc.shape, -jnp.inf, F32)
        l_sc[...] = jnp.zeros(l_sc.shape, F32)
        acc_sc[...] = jnp.zeros(acc_sc.shape, F32)

    def scores(u):
        i, c = units[u]
        q = q_ref[0, :, i * qw:(i + 1) * qw]
        k = kv_ref[0, c * tkc:(c + 1) * tkc, k_off[i]:k_off[i] + kw]
        return lax.dot_general(q, k, (((1,), (1,)), ((), ())), preferred_element_type=F32)

    s_sc[0] = scores(0)
    for u, (i, c) in enumerate(units):
        if u + 1 < len(units):
            s_sc[(u + 1) % 2] = scores(u + 1)
        cols = [s_sc[u % 2, :, cb * LANE:(cb + 1) * LANE] for cb in range(n_col)]
        mx = cols[0]
        for x in cols[1:]:
            mx = jnp.maximum(mx, x)
        m_prev = m_sc[i]
        m_new = jnp.maximum(m_prev, jnp.max(mx, axis=1, keepdims=True))
        alpha = jnp.exp2(m_prev - m_new)
        ps = [jnp.exp2(x - m_new) for x in cols]
        lsum = ps[0]
        for x in ps[1:]:
            lsum = lsum + x
        l_sc[i] = alpha * l_sc[i] + lsum
        p = jnp.concatenate([x.astype(BF16) for x in ps], axis=1)
        v = kv_ref[0, c * tkc:(c + 1) * tkc, v_off[i]:v_off[i] + LANE]
        acc_sc[i] = alpha * acc_sc[i] + jnp.dot(p, v, preferred_element_type=F32)
        m_sc[i] = m_new

    @pl.when(j == pl.num_programs(3) - 1)
    def _():
        o0 = acc_sc[0] / jnp.sum(l_sc[0], axis=1, keepdims=True)
        o1 = acc_sc[1] / jnp.sum(l_sc[1], axis=1, keepdims=True)
        low = lax.broadcasted_iota(jnp.int32, o0.shape, 1) < HALF
        o_ref[0] = jnp.where(low, o0, o1).astype(o_ref.dtype)


ATTN_TQ = 1024
ATTN_TK = 2048
ATTN_TKC = 512


def _attention(q, kv, *, qw, kvw_block, kv_block_of_pair, k_off, v_off, kw):
    B, S, _ = q.shape
    n_pairs = 3
    tq = _row_tile(S, ATTN_TQ)
    tk = _row_tile(S, ATTN_TK)
    tkc = _row_tile(tk, ATTN_TKC)
    kern = functools.partial(_attn_kernel, qw=qw, k_off=k_off, v_off=v_off, kw=kw, tkc=tkc)
    return pl.pallas_call(
        kern,
        grid=(B, n_pairs, S // tq, S // tk),
        in_specs=[pl.BlockSpec((1, tq, 2 * qw), lambda b, p, i, j: (b, i, p)),
                  pl.BlockSpec((1, tk, kvw_block), lambda b, p, i, j: (b, j, kv_block_of_pair(p)))],
        out_specs=pl.BlockSpec((1, tq, LANE), lambda b, p, i, j: (b, i, p)),
        out_shape=jax.ShapeDtypeStruct((B, S, n_pairs * LANE), BF16),
        scratch_shapes=[pltpu.VMEM((2, tq, LANE), F32),
                        pltpu.VMEM((2, tq, LANE), F32),
                        pltpu.VMEM((2, tq, LANE), F32),
                        pltpu.VMEM((2, tq, tkc), F32)],
        compiler_params=_cparams("parallel", "parallel", "parallel", "arbitrary"),
    )(q, kv)


def _mla_attention(qa, kva):
    return _attention(qa, kva, qw=256, kvw_block=512, kv_block_of_pair=lambda p: p,
                      k_off=(0, 256), v_off=(0, 256), kw=256)


def _gqa_attention(qg, kvg):
    return _attention(qg, kvg, qw=LANE, kvw_block=2 * LANE, kv_block_of_pair=lambda p: 0,
                      k_off=(0, 0), v_off=(LANE, LANE), kw=LANE)


_HALO = 16


def _conv_kernel(prev_ref, cur_ref, next_ref, w_ref, b_ref, g_ref, beta_ref, o_ref, xe_sc):
    i = pl.program_id(1)
    tc = cur_ref.shape[1]
    first = i == 0
    last = i == pl.num_programs(1) - 1
    xe_sc[0:_HALO, :] = jnp.where(first, 0.0, prev_ref[0])
    xe_sc[_HALO:_HALO + tc, :] = cur_ref[0]
    xe_sc[_HALO + tc:_HALO + tc + _HALO, :] = jnp.where(last, 0.0, next_ref[0])
    acc = jnp.zeros((tc, CONV_CH), F32)
    for t in range(CONV_K):
        off = _HALO - CONV_K // 2 + t
        acc = acc + xe_sc[off:off + tc, :] * w_ref[t:t + 1, :]
    u = acc + b_ref[...]
    y = _ln_rows(u, g_ref[...], beta_ref[...])
    o_ref[0] = (y * jax.nn.sigmoid(y)).astype(o_ref.dtype)


def _conv_module(glu, lw):
    B, S, C = glu.shape
    tc = _row_tile(S, 512)
    nh = tc // _HALO
    n_halo_blocks = S // _HALO
    const = lambda b, i: (0, 0)
    return pl.pallas_call(
        _conv_kernel,
        grid=(B, S // tc),
        in_specs=[pl.BlockSpec((1, _HALO, C), lambda b, i: (b, jnp.maximum(i * nh - 1, 0), 0)),
                  pl.BlockSpec((1, tc, C), lambda b, i: (b, i, 0)),
                  pl.BlockSpec((1, _HALO, C), lambda b, i: (b, jnp.minimum((i + 1) * nh, n_halo_blocks - 1), 0)),
                  pl.BlockSpec((CONV_K, C), const),
                  pl.BlockSpec((1, C), const),
                  pl.BlockSpec((1, C), const),
                  pl.BlockSpec((1, C), const)],
        out_specs=pl.BlockSpec((1, tc, C), lambda b, i: (b, i, 0)),
        out_shape=jax.ShapeDtypeStruct((B, S, C), BF16),
        scratch_shapes=[pltpu.VMEM((tc + 2 * _HALO, C), F32)],
        compiler_params=_cparams("parallel", "parallel"),
    )(glu, glu, glu, lw["conv_w"], lw["conv_b"], lw["conv_ln_g"], lw["conv_ln_b"])


def _oproj_kernel(h_ref, oa_ref, ob_ref, oc_ref, wa_ref, wb_ref, wc_ref, g_ref, b_ref, o_ref):
    mix = jnp.dot(oa_ref[...], wa_ref[...], preferred_element_type=F32)
    mix = mix + jnp.dot(ob_ref[...], wb_ref[...], preferred_element_type=F32)
    mix = mix + jnp.dot(oc_ref[...], wc_ref[...], preferred_element_type=F32)
    o_ref[...] = _ln_rows(DN_ALPHA * h_ref[...] + mix, g_ref[...], b_ref[...])


def _out_proj(h, oa, ob, oc, lw):
    n, d = h.shape
    tm = _row_tile(n, 512)
    const = lambda i: (0, 0)
    row = lambda i: (i, 0)
    return pl.pallas_call(
        _oproj_kernel,
        grid=(n // tm,),
        in_specs=[pl.BlockSpec((tm, d), row),
                  pl.BlockSpec((tm, oa.shape[1]), row),
                  pl.BlockSpec((tm, ob.shape[1]), row),
                  pl.BlockSpec((tm, oc.shape[1]), row),
                  pl.BlockSpec(lw["wo_a"].shape, const),
                  pl.BlockSpec(lw["wo_b"].shape, const),
                  pl.BlockSpec(lw["wo_c"].shape, const),
                  pl.BlockSpec((1, d), const),
                  pl.BlockSpec((1, d), const)],
        out_specs=pl.BlockSpec((tm, d), row),
        out_shape=jax.ShapeDtypeStruct((n, d), F32),
        compiler_params=_cparams("parallel"),
    )(h, oa, ob, oc, lw["wo_a"], lw["wo_b"], lw["wo_c"], lw["ln1_g"], lw["ln1_b"])


def _memkv_kernel(x_ref, w_ref, o_ref):
    o_ref[...] = jnp.dot(x_ref[...].astype(BF16), w_ref[...], preferred_element_type=F32).astype(o_ref.dtype)


def _mem_kv(mem2d, w_ckv):
    n, d = mem2d.shape
    tm = _row_tile(n, 256)
    return pl.pallas_call(
        _memkv_kernel,
        grid=(n // tm,),
        in_specs=[pl.BlockSpec((tm, d), lambda i: (i, 0)),
                  pl.BlockSpec(w_ckv.shape, lambda i: (0, 0))],
        out_specs=pl.BlockSpec((tm, w_ckv.shape[1]), lambda i: (i, 0)),
        out_shape=jax.ShapeDtypeStruct((n, w_ckv.shape[1]), BF16),
        compiler_params=_cparams("parallel"),
    )(mem2d, w_ckv)


def _xattn_kernel(h_ref, k_ref, v_ref, wq_ref, wo_ref, g_ref, b_ref, o_ref):
    h = h_ref[0]
    q = (jnp.dot(h.astype(BF16), wq_ref[...], preferred_element_type=F32) * XA_SCALE).astype(BF16)
    heads = []
    for hd in range(XA_HEADS):
        sl = slice(hd * XA_HD, (hd + 1) * XA_HD)
        s = lax.dot_general(q[:, sl], k_ref[0, :, sl], (((1,), (1,)), ((), ())),
                            preferred_element_type=F32)
        p = jnp.exp(s - jnp.max(s, axis=1, keepdims=True))
        o = jnp.dot(p.astype(BF16), v_ref[0, :, sl], preferred_element_type=F32)
        heads.append((o / jnp.sum(p, axis=1, keepdims=True)).astype(BF16))
    o = jnp.concatenate(heads, axis=1)
    y = jnp.dot(o, wo_ref[...], preferred_element_type=F32)
    o_ref[0] = _ln_rows(DN_ALPHA * h + y, g_ref[...], b_ref[...])


def _cross_attention(h, kvmem, lw):
    B, S, D = h.shape
    M = kvmem.shape[1]
    tm = _row_tile(S, 512)
    const = lambda b, i: (0, 0)
    return pl.pallas_call(
        _xattn_kernel,
        grid=(B, S // tm),
        in_specs=[pl.BlockSpec((1, tm, D), lambda b, i: (b, i, 0)),
                  pl.BlockSpec((1, M, D), lambda b, i: (b, 0, 0)),
                  pl.BlockSpec((1, M, D), lambda b, i: (b, 0, 1)),
                  pl.BlockSpec((D, D), const),
                  pl.BlockSpec((D, D), const),
                  pl.BlockSpec((1, D), const),
                  pl.BlockSpec((1, D), const)],
        out_specs=pl.BlockSpec((1, tm, D), lambda b, i: (b, i, 0)),
        out_shape=jax.ShapeDtypeStruct((B, S, D), F32),
        compiler_params=_cparams("parallel", "parallel"),
    )(h, kvmem, kvmem, lw["w_cq"], lw["w_co"], lw["ln2_g"], lw["ln2_b"])


MOE_TILE = 1024
MOE_ROWS = 256
MOE_ALIGN = 16


def _router_kernel(x_ref, wr_ref, bias_ref, gate_ref, pos_ref, dd_ref, cnt_ref, start_ref, sel_sc):
    T = x_ref.shape[0]
    logits = lax.dot_general(wr_ref[...], x_ref[...], (((1,), (1,)), ((), ())),
                             precision=lax.Precision.HIGHEST, preferred_element_type=F32)
    scores = jax.nn.sigmoid(logits)
    sel = scores + bias_ref[...]
    sc = [scores[e:e + 1, :] for e in range(N_EXPERTS)]
    se = [sel[e:e + 1, :] for e in range(N_EXPERTS)]

    gs = []
    for g in range(N_GROUPS):
        m = [se[g * EXPERTS_PER_GROUP + k] for k in range(EXPERTS_PER_GROUP)]
        best = None
        for a in range(EXPERTS_PER_GROUP):
            for b in range(a + 1, EXPERTS_PER_GROUP):
                pair = m[a] + m[b]
                best = pair if best is None else jnp.maximum(best, pair)
        gs.append(best)
    in_group = []
    for g in range(N_GROUPS):
        ok = None
        for o in range(N_GROUPS):
            if o == g:
                continue
            c = (gs[g] > gs[o]) if o < g else (gs[g] >= gs[o])
            ok = c if ok is None else (ok & c)
        in_group.append(ok)
    chosen = []
    for e in range(N_EXPERTS):
        g = e // EXPERTS_PER_GROUP
        beaten = jnp.zeros((1, T), jnp.int32)
        for o in range(g * EXPERTS_PER_GROUP, (g + 1) * EXPERTS_PER_GROUP):
            if o == e:
                continue
            c = (se[o] > se[e]) if o > e else (se[o] >= se[e])
            beaten = beaten + c.astype(jnp.int32)
        chosen.append(in_group[g] & (beaten < 2))
    denom = jnp.zeros((1, T), F32)
    for e in range(N_EXPERTS):
        denom = denom + jnp.where(chosen[e], sc[e], 0.0)
    for e in range(N_EXPERTS):
        gate_ref[e:e + 1, :] = jnp.where(chosen[e], sc[e] / denom, 0.0)
        sel_sc[e:e + 1, :] = chosen[e].astype(F32)

    selm = sel_sc[...]
    earlier = (lax.broadcasted_iota(jnp.int32, (T, T), 0) < lax.broadcasted_iota(jnp.int32, (T, T), 1))
    rank = jnp.dot(selm.astype(BF16), earlier.astype(BF16), preferred_element_type=F32)
    cnt = jnp.sum(selm, axis=1, keepdims=True)
    cnt_ref[0] = jnp.broadcast_to(cnt, (N_EXPERTS, LANE)).astype(jnp.int32)

    seg = jnp.floor((cnt + (MOE_ALIGN - 1)) * (1.0 / MOE_ALIGN)) * MOE_ALIGN
    start = jnp.zeros((1, 1), F32)
    d_lo = jnp.full((1, T), float(4 * T), F32)
    d_hi = jnp.full((1, T), -1.0, F32)
    for e in range(N_EXPERTS):
        start_ref[0, e:e + 1, :] = jnp.broadcast_to(start, (1, LANE)).astype(jnp.int32)
        dest = jnp.where(chosen[e], rank[e:e + 1, :] + start, -1.0)
        pos_ref[e:e + 1, :] = dest
        d_lo = jnp.where(chosen[e], jnp.minimum(d_lo, dest), d_lo)
        d_hi = jnp.maximum(d_hi, dest)
        start = start + seg[e:e + 1, :]
    dd_ref[...] = jnp.zeros(dd_ref.shape, F32)
    dd_ref[0:1, :] = d_lo
    dd_ref[1:2, :] = d_hi


def _router(x, w_rt, bias):
    n, d = x.shape
    T = _row_tile(n, MOE_TILE)
    nt = n // T
    gate, pos, dd, cnt, start = pl.pallas_call(
        _router_kernel,
        grid=(nt,),
        in_specs=[pl.BlockSpec((T, d), lambda i: (i, 0)),
                  pl.BlockSpec((N_EXPERTS, d), lambda i: (0, 0)),
                  pl.BlockSpec((N_EXPERTS, 1), lambda i: (0, 0))],
        out_specs=[pl.BlockSpec((N_EXPERTS, T), lambda i: (0, i)),
                   pl.BlockSpec((N_EXPERTS, T), lambda i: (0, i)),
                   pl.BlockSpec((8, T), lambda i: (0, i)),
                   pl.BlockSpec((1, N_EXPERTS, LANE), lambda i: (i, 0, 0)),
                   pl.BlockSpec((1, N_EXPERTS, LANE), lambda i: (i, 0, 0))],
        out_shape=[jax.ShapeDtypeStruct((N_EXPERTS, n), F32),
                   jax.ShapeDtypeStruct((N_EXPERTS, n), F32),
                   jax.ShapeDtypeStruct((8, n), F32),
                   jax.ShapeDtypeStruct((nt, N_EXPERTS, LANE), jnp.int32),
                   jax.ShapeDtypeStruct((nt, N_EXPERTS, LANE), jnp.int32)],
        scratch_shapes=[pltpu.VMEM((N_EXPERTS, T), F32)],
        compiler_params=_cparams("parallel"),
    )(x, w_rt, bias)
    return gate, pos, dd, cnt[:, :, 0].reshape(-1), start[:, :, 0].reshape(-1)


def _sorted_rows(T):
    return 2 * T + N_EXPERTS * MOE_ALIGN + MOE_ROWS


def _moe_kernel(cnt_ref, start_ref, h_ref, gate_ref, pos_ref, dd_ref, wg_ref, wu_ref, wd_ref, g_ref, b_ref, o_ref,
                p_sc, xs_sc, ys_sc):
    i = pl.program_id(0)
    e = pl.program_id(1)
    T = h_ref.shape[0]
    R = p_sc.shape[0]

    @pl.when(e == 0)
    def _():
        d_lo = dd_ref[0:1, :].astype(jnp.int32)
        d_hi = dd_ref[1:2, :].astype(jnp.int32)
        xb = h_ref[...].astype(BF16)
        for r0 in range(0, R, MOE_ROWS):
            row = lax.broadcasted_iota(jnp.int32, (MOE_ROWS, T), 0) + r0
            onehot = ((row == d_lo) | (row == d_hi)).astype(BF16)
            p_sc[r0:r0 + MOE_ROWS, :] = onehot
            xs_sc[r0:r0 + MOE_ROWS, :] = jnp.dot(onehot, xb, preferred_element_type=F32).astype(BF16)
        ys_sc[...] = jnp.zeros(ys_sc.shape, BF16)

    n_rows = cnt_ref[i * N_EXPERTS + e]
    seg0 = start_ref[i * N_EXPERTS + e]
    n_blocks = (n_rows + MOE_ROWS - 1) // MOE_ROWS
    pos_row = pos_ref[pl.ds(e, 1), :].astype(jnp.int32)
    gate_row = gate_ref[pl.ds(e, 1), :]
    slot = lax.broadcasted_iota(jnp.int32, (MOE_ROWS, T), 0)

    def block(jb, carry):
        base = pl.multiple_of(seg0 + jb * MOE_ROWS, MOE_ALIGN)
        xg = xs_sc[pl.ds(base, MOE_ROWS), :]
        hg = jnp.dot(xg, wg_ref[0], preferred_element_type=F32)
        hu = jnp.dot(xg, wu_ref[0], preferred_element_type=F32)
        act = (hg * jax.nn.sigmoid(hg) * hu).astype(BF16)
        y = jnp.dot(act, wd_ref[0], preferred_element_type=F32)
        hit = (pos_row - base) == slot
        g_rows = jnp.sum(jnp.where(hit, gate_row, 0.0), axis=1, keepdims=True)
        ys_sc[pl.ds(base, MOE_ROWS), :] = (y * g_rows).astype(BF16)
        return carry

    lax.fori_loop(0, n_blocks, block, 0)

    @pl.when(e == N_EXPERTS - 1)
    def _():
        moe = lax.dot_general(p_sc[...], ys_sc[...], (((0,), (0,)), ((), ())), preferred_element_type=F32)
        o_ref[...] = _ln_rows(DN_ALPHA * h_ref[...] + moe, g_ref[...], b_ref[...])


def _moe(h, gate, pos, dd, cnt, start, lw):
    n, d = h.shape
    T = _row_tile(n, MOE_TILE)
    R = _sorted_rows(T)
    f = D_FF_EXPERT
    grid_spec = pltpu.PrefetchScalarGridSpec(
        num_scalar_prefetch=2,
        grid=(n // T, N_EXPERTS),
        in_specs=[pl.BlockSpec((T, d), lambda i, e, c, s: (i, 0)),
                  pl.BlockSpec((N_EXPERTS, T), lambda i, e, c, s: (0, i)),
                  pl.BlockSpec((N_EXPERTS, T), lambda i, e, c, s: (0, i)),
                  pl.BlockSpec((8, T), lambda i, e, c, s: (0, i)),
                  pl.BlockSpec((1, d, f), lambda i, e, c, s: (e, 0, 0)),
                  pl.BlockSpec((1, d, f), lambda i, e, c, s: (e, 0, 0)),
                  pl.BlockSpec((1, f, d), lambda i, e, c, s: (e, 0, 0)),
                  pl.BlockSpec((1, d), lambda i, e, c, s: (0, 0)),
                  pl.BlockSpec((1, d), lambda i, e, c, s: (0, 0))],
        out_specs=pl.BlockSpec((T, d), lambda i, e, c, s: (i, 0)),
        scratch_shapes=[pltpu.VMEM((R, T), BF16), pltpu.VMEM((R, d), BF16), pltpu.VMEM((R, d), BF16)],
    )
    return pl.pallas_call(
        _moe_kernel,
        grid_spec=grid_spec,
        out_shape=jax.ShapeDtypeStruct((n, d), F32),
        compiler_params=_cparams("parallel", "arbitrary"),
    )(cnt, start, h, gate, pos, dd, lw["w_gate"], lw["w_up"], lw["w_down"], lw["ln3_g"], lw["ln3_b"])


def _prep_layer(l, w):
    ext_idx, ext_sgn = _w_ext_index()
    uq_idx, uq_sgn = _w_uq_index()
    mla_w = MLA_HEADS * MLA_V
    gq = w["gqa_q_norm"][l]
    gk = w["gqa_k_norm"][l]
    row = lambda v: v.reshape(1, -1).astype(F32)
    return {
        "w_ext": _take_cols(w["w_in"][l], ext_idx, ext_sgn).astype(BF16),
        "w_uq": _take_cols(w["mla_w_uq"][l], uq_idx, uq_sgn).astype(BF16),
        "w_ukv": w["mla_w_ukv"][l][:, _w_ukv_index()].astype(BF16),
        "qn": row(w["mla_q_norm"][l]),
        "kvn": row(w["mla_kv_norm"][l]),
        "gq": row(jnp.tile(gq, 2)),
        "gqp": row(jnp.tile(_perm_gain(gq, GQA_HD // 4), 2)),
        "gk": row(jnp.tile(gk, 2)),
        "gkp": row(jnp.tile(_perm_gain(gk, GQA_HD // 4), 2)),
        "conv_w": w["conv_w"][l].astype(F32),
        "conv_b": row(w["conv_b"][l]),
        "conv_ln_g": row(w["conv_ln_g"][l]),
        "conv_ln_b": row(w["conv_ln_b"][l]),
        "wo_a": w["w_o"][l][:mla_w].astype(BF16),
        "wo_b": w["w_o"][l][mla_w:mla_w + CONV_CH].astype(BF16),
        "wo_c": w["w_o"][l][mla_w + CONV_CH:][_w_oc_index()].astype(BF16),
        "ln1_g": row(w["ln1_g"][l]), "ln1_b": row(w["ln1_b"][l]),
        "w_cq": w["w_cq"][l].astype(BF16),
        "w_ckv": w["w_ckv"][l].astype(BF16),
        "w_co": w["w_co"][l].astype(BF16),
        "ln2_g": row(w["ln2_g"][l]), "ln2_b": row(w["ln2_b"][l]),
        "w_gate": w["w_gate"][l].astype(BF16),
        "w_up": w["w_up"][l].astype(BF16),
        "w_down": w["w_down"][l].astype(BF16),
        "ln3_g": row(w["ln3_g"][l]), "ln3_b": row(w["ln3_b"][l]),
    }


def _trunk(x, mem, w, layers):
    B, S, D = x.shape
    n = B * S
    tab = _rope_tables(S)
    w_rt = w["w_router"].T.astype(F32)
    r_bias = w["router_bias"].reshape(N_EXPERTS, 1).astype(F32)
    h = _layer_norm_rows(x.reshape(n, D), w["ln_in_g"], w["ln_in_b"])
    mem2d = mem.reshape(B * mem.shape[1], D)
    for lw in layers:
        qa, kva, glu, qg, kvg = _mixer_pre(h.reshape(B, S, D), tab, lw)
        oa = _mla_attention(qa, kva)
        ob = _conv_module(glu, lw)
        oc = _gqa_attention(qg, kvg)
        h = _out_proj(h, oa.reshape(n, -1), ob.reshape(n, -1), oc.reshape(n, -1), lw)
        kvmem = _mem_kv(mem2d, lw["w_ckv"]).reshape(B, mem.shape[1], 2 * D)
        h = _cross_attention(h.reshape(B, S, D), kvmem, lw).reshape(n, D)
        gate, pos, dd, cnt, start = _router(h, w_rt, r_bias)
        h = _moe(h, gate, pos, dd, cnt, start, lw)
    return h.reshape(B, S, D)


def kernel(x_prompt, x_sample, mem_prompt, mem_sample, ln_in_g, ln_in_b, w_in, mla_q_norm, mla_w_uq,
           mla_kv_norm, mla_w_ukv, conv_w, conv_b, conv_ln_g, conv_ln_b, gqa_q_norm, gqa_k_norm, w_o,
           ln1_g, ln1_b, w_cq, w_ckv, w_co, ln2_g, ln2_b, w_router, router_bias, w_gate, w_up, w_down,
           ln3_g, ln3_b):
    w = dict(ln_in_g=ln_in_g, ln_in_b=ln_in_b, w_in=w_in, mla_q_norm=mla_q_norm, mla_w_uq=mla_w_uq,
             mla_kv_norm=mla_kv_norm, mla_w_ukv=mla_w_ukv, conv_w=conv_w, conv_b=conv_b,
             conv_ln_g=conv_ln_g, conv_ln_b=conv_ln_b, gqa_q_norm=gqa_q_norm, gqa_k_norm=gqa_k_norm,
             w_o=w_o, ln1_g=ln1_g, ln1_b=ln1_b, w_cq=w_cq, w_ckv=w_ckv, w_co=w_co, ln2_g=ln2_g,
             ln2_b=ln2_b, w_router=w_router, router_bias=router_bias, w_gate=w_gate, w_up=w_up,
             w_down=w_down, ln3_g=ln3_g, ln3_b=ln3_b)
    layers = [_prep_layer(l, w) for l in range(w_in.shape[0])]
    y_prompt = _trunk(x_prompt, mem_prompt, w, layers)
    y_sample = _trunk(x_sample, mem_sample, w, layers)
    return (y_prompt, y_sample)
```

```python
import functools

import numpy as np
import jax
import jax.numpy as jnp
from jax import lax
from jax.experimental import pallas as pl
from jax.experimental.pallas import tpu as pltpu

F32 = jnp.float32
BF16 = jnp.bfloat16

D_MODEL = 1024
DEPTH = 4
GRID_W = 64
ROPE_THETA = 10000.0
LN_EPS = 1e-5
RMS_EPS = 1e-6
MLA_HEADS = 6
MLA_Q_LORA = 256
MLA_KV_LORA = 128
MLA_NOPE = 64
MLA_ROPE = 32
MLA_V = 64
CONV_CH = 256
CONV_K = 31
GQA_HEADS = 6
GQA_KV_HEADS = 2
GQA_HD = 64
SPLITS = (MLA_Q_LORA, MLA_KV_LORA, MLA_ROPE, 2 * CONV_CH,
          GQA_HEADS * GQA_HD, GQA_KV_HEADS * GQA_HD, GQA_KV_HEADS * GQA_HD)
XA_HEADS = 4
XA_HD = D_MODEL // XA_HEADS
N_EXPERTS = 16
N_GROUPS = 4
EXPERTS_PER_GROUP = N_EXPERTS // N_GROUPS
D_FF_EXPERT = 512
DN_ALPHA = (2 * DEPTH) ** 0.25
LOG2E = 1.4426950408889634
MLA_SCALE = (MLA_NOPE + MLA_ROPE) ** -0.5 * LOG2E
GQA_SCALE = GQA_HD ** -0.5 * LOG2E
XA_SCALE = XA_HD ** -0.5

LANE = 128
HALF = LANE // 2
VMEM_LIMIT = 56 * 1024 * 1024

_OFF_CQ = 0
_OFF_CKV = 256
_OFF_KR = 384
_OFF_KRR = 512
_OFF_GA = 640
_OFF_GG = 896
_OFF_QC = 1152
_OFF_QCR = 1920
_OFF_KC = 2688
_OFF_KCR = 2816
_OFF_VC = 2944
_W_EXT = 3200
_GQA_BLOCK_HEADS = (0, 3, 1, 4, 2, 5)
_SUM_LANE = (HALF, 0)


def _cparams(*sem):
    return pltpu.CompilerParams(dimension_semantics=sem, vmem_limit_bytes=VMEM_LIMIT)


def _row_tile(n, want):
    t = min(n, want)
    while n % t:
        t //= 2
    return t


def _ln_rows(x, g, b):
    mu = jnp.mean(x, axis=-1, keepdims=True)
    xc = x - mu
    var = jnp.mean(xc * xc, axis=-1, keepdims=True)
    return xc * lax.rsqrt(var + LN_EPS) * g + b


def _take_cols(w, idx, sign=None):
    w_aug = jnp.concatenate([w, jnp.zeros((w.shape[0], 1), w.dtype)], axis=1)
    idx = np.where(idx < 0, w.shape[1], idx)
    out = w_aug[:, idx]
    if sign is not None:
        out = out * jnp.asarray(sign, w.dtype)[None, :]
    return out


def _rot_src(d, half):
    r = d % (2 * half)
    base = d - r
    if r < half:
        return base + r + half, -1.0
    return base + r - half, 1.0


def _w_ext_index():
    cut = np.concatenate([[0], np.cumsum(SPLITS)])
    c_cq, c_ckv, c_kr, c_glu, c_qc, c_kc, c_vc = [int(c) for c in cut[:-1]]
    idx = -np.ones((_W_EXT,), np.int64)
    sgn = np.ones((_W_EXT,), np.float32)
    idx[_OFF_CQ:_OFF_CQ + 256] = c_cq + np.arange(256)
    idx[_OFF_CKV:_OFF_CKV + 128] = c_ckv + np.arange(128)
    for d in range(MLA_ROPE):
        idx[_OFF_KR + MLA_NOPE + d] = c_kr + d
        s, sg = _rot_src(d, MLA_ROPE // 2)
        idx[_OFF_KRR + MLA_NOPE + d] = c_kr + s
        sgn[_OFF_KRR + MLA_NOPE + d] = sg
    idx[_OFF_GA:_OFF_GA + 256] = c_glu + np.arange(256)
    idx[_OFF_GG:_OFF_GG + 256] = c_glu + 256 + np.arange(256)
    for blk, hq in enumerate(_GQA_BLOCK_HEADS):
        lane0 = 0 if hq < GQA_HEADS // GQA_KV_HEADS else HALF
        for d in range(GQA_HD):
            idx[_OFF_QC + blk * LANE + lane0 + d] = c_qc + hq * GQA_HD + d
            s, sg = _rot_src(d, GQA_HD // 4)
            idx[_OFF_QCR + blk * LANE + lane0 + d] = c_qc + hq * GQA_HD + s
            sgn[_OFF_QCR + blk * LANE + lane0 + d] = sg
    for c in range(GQA_KV_HEADS * GQA_HD):
        hk, d = divmod(c, GQA_HD)
        idx[_OFF_KC + c] = c_kc + c
        s, sg = _rot_src(d, GQA_HD // 4)
        idx[_OFF_KCR + c] = c_kc + hk * GQA_HD + s
        sgn[_OFF_KCR + c] = sg
    idx[_OFF_VC:_OFF_VC + HALF] = c_vc + np.arange(HALF)
    idx[_OFF_VC + LANE + HALF:_OFF_VC + 2 * LANE] = c_vc + HALF + np.arange(HALF)
    return idx, sgn


def _w_uq_index():
    per = MLA_NOPE + MLA_ROPE
    idx = -np.ones((2 * MLA_HEADS * LANE,), np.int64)
    sgn = np.ones(idx.shape, np.float32)
    for h in range(MLA_HEADS):
        idx[h * LANE:h * LANE + per] = h * per + np.arange(per)
        for d in range(MLA_ROPE):
            s, sg = _rot_src(d, MLA_ROPE // 2)
            idx[(MLA_HEADS + h) * LANE + MLA_NOPE + d] = h * per + MLA_NOPE + s
            sgn[(MLA_HEADS + h) * LANE + MLA_NOPE + d] = sg
    return idx, sgn


def _w_ukv_index():
    per = MLA_NOPE + MLA_V
    idx = -np.ones((MLA_HEADS * 2 * LANE,), np.int64)
    for h in range(MLA_HEADS):
        v0 = h * 2 * LANE + (0 if h % 2 == 0 else HALF)
        idx[v0:v0 + MLA_V] = h * per + MLA_NOPE + np.arange(MLA_V)
        idx[h * 2 * LANE + LANE:h * 2 * LANE + LANE + MLA_NOPE] = h * per + np.arange(MLA_NOPE)
    return idx


def _w_oc_index():
    idx = np.zeros((GQA_HEADS * GQA_HD,), np.int64)
    for blk, hq in enumerate(_GQA_BLOCK_HEADS):
        c0 = (blk // 2) * LANE + (blk % 2) * HALF
        idx[c0:c0 + GQA_HD] = hq * GQA_HD + np.arange(GQA_HD)
    return idx


def _perm_gain(g, half):
    src = np.array([_rot_src(d, half)[0] for d in range(g.shape[-1])])
    return g[src]


def _rope_tables(S):
    def cs(pos, dim):
        inv = 1.0 / (ROPE_THETA ** (jnp.arange(0, dim, 2, dtype=F32) / dim))
        ang = pos[:, None] * inv[None, :]
        return jnp.cos(ang), jnp.sin(ang)

    c1, s1 = cs(jnp.arange(S, dtype=F32), MLA_ROPE)
    rows = S // GRID_W
    row = jnp.repeat(jnp.arange(rows, dtype=F32), GRID_W)
    col = jnp.tile(jnp.arange(GRID_W, dtype=F32), rows)
    cr, sr = cs(row, GQA_HD // 2)
    cc, sc = cs(col, GQA_HD // 2)
    zpad = jnp.zeros((S, LANE - MLA_NOPE - MLA_ROPE), F32)
    cm = jnp.concatenate([jnp.ones((S, MLA_NOPE), F32), c1, c1, zpad], axis=1)
    sm = jnp.concatenate([jnp.zeros((S, MLA_NOPE), F32), s1, s1, zpad], axis=1)
    cg = jnp.concatenate([cr, cr, cc, cc] * 2, axis=1)
    sg = jnp.concatenate([sr, sr, sc, sc] * 2, axis=1)
    return jnp.concatenate([cm, sm, cg, sg], axis=1)


def _ln_kernel(x_ref, g_ref, b_ref, o_ref):
    o_ref[...] = _ln_rows(x_ref[...], g_ref[...], b_ref[...])


def _layer_norm_rows(x, g, b):
    n, d = x.shape
    tm = _row_tile(n, 1024)
    return pl.pallas_call(
        _ln_kernel,
        grid=(n // tm,),
        in_specs=[pl.BlockSpec((tm, d), lambda i: (i, 0)),
                  pl.BlockSpec((1, d), lambda i: (0, 0)),
                  pl.BlockSpec((1, d), lambda i: (0, 0))],
        out_specs=pl.BlockSpec((tm, d), lambda i: (i, 0)),
        out_shape=jax.ShapeDtypeStruct((n, d), F32),
        compiler_params=_cparams("parallel"),
    )(x, g.reshape(1, d), b.reshape(1, d))


def _rms_scale(ss, width):
    return lax.rsqrt(ss * (1.0 / width) + RMS_EPS)


def _pre_kernel(h_ref, tab_ref, wext_ref, wuq_ref, wukv_ref, qn_ref, kvn_ref,
                gq_ref, gqp_ref, gk_ref, gkp_ref,
                qa_ref, kva_ref, glu_ref, qg_ref, kvg_ref):
    hb = h_ref[0].astype(BF16)
    z = jnp.dot(hb, wext_ref[...], preferred_element_type=F32)
    cm = tab_ref[:, 0:LANE]
    sm = tab_ref[:, LANE:2 * LANE]
    cg = tab_ref[:, 2 * LANE:3 * LANE]
    sg = tab_ref[:, 3 * LANE:4 * LANE]

    cq = z[:, _OFF_CQ:_OFF_CQ + MLA_Q_LORA]
    cqn = cq * _rms_scale(jnp.sum(cq * cq, axis=-1, keepdims=True), MLA_Q_LORA) * qn_ref[...]
    a = jnp.dot(cqn.astype(BF16), wuq_ref[...], preferred_element_type=F32)
    rot0 = MLA_HEADS * LANE
    for h in range(MLA_HEADS):
        roped = a[:, h * LANE:(h + 1) * LANE] * cm + a[:, rot0 + h * LANE:rot0 + (h + 1) * LANE] * sm
        qa_ref[0, :, h * LANE:(h + 1) * LANE] = (roped * MLA_SCALE).astype(BF16)

    ckv = z[:, _OFF_CKV:_OFF_CKV + MLA_KV_LORA]
    ckvn = ckv * _rms_scale(jnp.sum(ckv * ckv, axis=-1, keepdims=True), MLA_KV_LORA) * kvn_ref[...]
    kvm = jnp.dot(ckvn.astype(BF16), wukv_ref[...], preferred_element_type=F32)
    kr = z[:, _OFF_KR:_OFF_KR + LANE] * cm + z[:, _OFF_KRR:_OFF_KRR + LANE] * sm
    lane = lax.broadcasted_iota(jnp.int32, (1, LANE), 1)
    one_at = [jnp.where(lane == L, 1.0, 0.0) for L in _SUM_LANE]
    for h in range(MLA_HEADS):
        c0 = h * 2 * LANE
        kva_ref[0, :, c0:c0 + LANE] = (kvm[:, c0:c0 + LANE] + one_at[h % 2]).astype(BF16)
        kva_ref[0, :, c0 + LANE:c0 + 2 * LANE] = (kvm[:, c0 + LANE:c0 + 2 * LANE] + kr).astype(BF16)

    glu_ref[0] = z[:, _OFF_GA:_OFF_GA + CONV_CH] * jax.nn.sigmoid(z[:, _OFF_GG:_OFF_GG + CONV_CH])

    gq = gq_ref[...]
    gqp = gqp_ref[...]
    for blk in range(GQA_HEADS):
        zq = z[:, _OFF_QC + blk * LANE:_OFF_QC + (blk + 1) * LANE]
        zr = z[:, _OFF_QCR + blk * LANE:_OFF_QCR + (blk + 1) * LANE]
        r = _rms_scale(jnp.sum(zq * zq, axis=-1, keepdims=True), GQA_HD)
        qg_ref[0, :, blk * LANE:(blk + 1) * LANE] = (
            (zq * gq * cg + zr * gqp * sg) * (r * GQA_SCALE)).astype(BF16)

    zk = z[:, _OFF_KC:_OFF_KC + LANE]
    zkr = z[:, _OFF_KCR:_OFF_KCR + LANE]
    low = lax.broadcasted_iota(jnp.int32, zk.shape, 1) < HALF
    sq = zk * zk
    r0 = _rms_scale(jnp.sum(jnp.where(low, sq, 0.0), axis=-1, keepdims=True), GQA_HD)
    r1 = _rms_scale(jnp.sum(jnp.where(low, 0.0, sq), axis=-1, keepdims=True), GQA_HD)
    kvg_ref[0, :, 0:LANE] = ((zk * gk_ref[...] * cg + zkr * gkp_ref[...] * sg)
                             * jnp.where(low, r0, r1)).astype(BF16)
    kvg_ref[0, :, LANE:2 * LANE] = (z[:, _OFF_VC:_OFF_VC + LANE] + one_at[0]).astype(BF16)
    kvg_ref[0, :, 2 * LANE:3 * LANE] = (z[:, _OFF_VC + LANE:_OFF_VC + 2 * LANE] + one_at[1]).astype(BF16)


def _mixer_pre(h, tab, lw):
    B, S, D = h.shape
    tm = _row_tile(S, 512)
    const = lambda b, i: (0, 0)
    row3 = lambda b, i: (b, i, 0)
    outs = [(MLA_HEADS * LANE, BF16), (MLA_HEADS * 2 * LANE, BF16), (CONV_CH, F32),
            (GQA_HEADS * LANE, BF16), (3 * LANE, BF16)]
    return pl.pallas_call(
        _pre_kernel,
        grid=(B, S // tm),
        in_specs=[pl.BlockSpec((1, tm, D), row3),
                  pl.BlockSpec((tm, 4 * LANE), lambda b, i: (i, 0)),
                  pl.BlockSpec(lw["w_ext"].shape, const),
                  pl.BlockSpec(lw["w_uq"].shape, const),
                  pl.BlockSpec(lw["w_ukv"].shape, const),
                  pl.BlockSpec((1, MLA_Q_LORA), const),
                  pl.BlockSpec((1, MLA_KV_LORA), const),
                  pl.BlockSpec((1, LANE), const),
                  pl.BlockSpec((1, LANE), const),
                  pl.BlockSpec((1, LANE), const),
                  pl.BlockSpec((1, LANE), const)],
        out_specs=[pl.BlockSpec((1, tm, w), row3) for w, _ in outs],
        out_shape=[jax.ShapeDtypeStruct((B, S, w), dt) for w, dt in outs],
        compiler_params=_cparams("parallel", "parallel"),
    )(h, tab, lw["w_ext"], lw["w_uq"], lw["w_ukv"], lw["qn"], lw["kvn"],
      lw["gq"], lw["gqp"], lw["gk"], lw["gkp"])


def _attn_kernel(q_ref, kv_ref, o_ref, m_sc, acc_sc, s_sc, *, k_off, v_off, tkc):
    j = pl.program_id(3)
    tk = kv_ref.shape[1]
    units = [(i, c) for c in range(tk // tkc) for i in range(2)]
    n_col = tkc // LANE

    @pl.when(j == 0)
    def _():
        m_sc[...] = jnp.full(m_sc.shape, -jnp.inf, F32)
        acc_sc[...] = jnp.zeros(acc_sc.shape, F32)

    def scores(u):
        i, c = units[u]
        q = q_ref[0, :, i * LANE:(i + 1) * LANE]
        k = kv_ref[0, c * tkc:(c + 1) * tkc, k_off[i]:k_off[i] + LANE]
        return lax.dot_general(q, k, (((1,), (1,)), ((), ())), preferred_element_type=F32)

    s_sc[0] = scores(0)
    for u, (i, c) in enumerate(units):
        if u + 1 < len(units):
            s_sc[(u + 1) % 2] = scores(u + 1)
        cols = [s_sc[u % 2, :, cb * LANE:(cb + 1) * LANE] for cb in range(n_col)]
        mx = cols[0]
        for x in cols[1:]:
            mx = jnp.maximum(mx, x)
        m_prev = m_sc[i]
        m_new = jnp.maximum(m_prev, jnp.max(mx, axis=1, keepdims=True))
        alpha = jnp.exp2(m_prev - m_new)
        p = jnp.concatenate([jnp.exp2(x - m_new).astype(BF16) for x in cols], axis=1)
        v = kv_ref[0, c * tkc:(c + 1) * tkc, v_off[i]:v_off[i] + LANE]
        acc_sc[i] = alpha * acc_sc[i] + jnp.dot(p, v, preferred_element_type=F32)
        m_sc[i] = m_new

    @pl.when(j == pl.num_programs(3) - 1)
    def _():
        o0 = acc_sc[0] / acc_sc[0, :, _SUM_LANE[0]:_SUM_LANE[0] + 1]
        o1 = acc_sc[1] / acc_sc[1, :, _SUM_LANE[1]:_SUM_LANE[1] + 1]
        low = lax.broadcasted_iota(jnp.int32, o0.shape, 1) < HALF
        o_ref[0] = jnp.where(low, o0, o1).astype(o_ref.dtype)


ATTN_TQ = 1024
ATTN_TK = 4096
ATTN_TKC = 512


def _attention(q, kv, *, kvw_block, kv_block_of_pair, k_off, v_off):
    B, S, _ = q.shape
    n_pairs = 3
    tq = _row_tile(S, ATTN_TQ)
    tk = _row_tile(S, ATTN_TK)
    tkc = _row_tile(tk, ATTN_TKC)
    kern = functools.partial(_attn_kernel, k_off=k_off, v_off=v_off, tkc=tkc)
    return pl.pallas_call(
        kern,
        grid=(B, n_pairs, S // tq, S // tk),
        in_specs=[pl.BlockSpec((1, tq, 2 * LANE), lambda b, p, i, j: (b, i, p)),
                  pl.BlockSpec((1, tk, kvw_block), lambda b, p, i, j: (b, j, kv_block_of_pair(p)))],
        out_specs=pl.BlockSpec((1, tq, LANE), lambda b, p, i, j: (b, i, p)),
        out_shape=jax.ShapeDtypeStruct((B, S, n_pairs * LANE), BF16),
        scratch_shapes=[pltpu.VMEM((2, tq, LANE), F32),
                        pltpu.VMEM((2, tq, LANE), F32),
                        pltpu.VMEM((2, tq, tkc), F32)],
        compiler_params=_cparams("parallel", "parallel", "parallel", "arbitrary"),
    )(q, kv)


def _mla_attention(qa, kva):
    return _attention(qa, kva, kvw_block=4 * LANE, kv_block_of_pair=lambda p: p,
                      k_off=(LANE, 3 * LANE), v_off=(0, 2 * LANE))


def _gqa_attention(qg, kvg):
    return _attention(qg, kvg, kvw_block=3 * LANE, kv_block_of_pair=lambda p: 0,
                      k_off=(0, 0), v_off=(LANE, 2 * LANE))


_HALO = 16


def _conv_kernel(prev_ref, cur_ref, next_ref, w_ref, b_ref, g_ref, beta_ref, o_ref, xe_sc):
    i = pl.program_id(1)
    tc = cur_ref.shape[1]
    first = i == 0
    last = i == pl.num_programs(1) - 1
    xe_sc[0:_HALO, :] = jnp.where(first, 0.0, prev_ref[0])
    xe_sc[_HALO:_HALO + tc, :] = cur_ref[0]
    xe_sc[_HALO + tc:_HALO + tc + _HALO, :] = jnp.where(last, 0.0, next_ref[0])
    acc = jnp.zeros((tc, CONV_CH), F32)
    for t in range(CONV_K):
        off = _HALO - CONV_K // 2 + t
        acc = acc + xe_sc[off:off + tc, :] * w_ref[t:t + 1, :]
    u = acc + b_ref[...]
    y = _ln_rows(u, g_ref[...], beta_ref[...])
    o_ref[0] = (y * jax.nn.sigmoid(y)).astype(o_ref.dtype)


def _conv_module(glu, lw):
    B, S, C = glu.shape
    tc = _row_tile(S, 512)
    nh = tc // _HALO
    n_halo_blocks = S // _HALO
    const = lambda b, i: (0, 0)
    return pl.pallas_call(
        _conv_kernel,
        grid=(B, S // tc),
        in_specs=[pl.BlockSpec((1, _HALO, C), lambda b, i: (b, jnp.maximum(i * nh - 1, 0), 0)),
                  pl.BlockSpec((1, tc, C), lambda b, i: (b, i, 0)),
                  pl.BlockSpec((1, _HALO, C), lambda b, i: (b, jnp.minimum((i + 1) * nh, n_halo_blocks - 1), 0)),
                  pl.BlockSpec((CONV_K, C), const),
                  pl.BlockSpec((1, C), const),
                  pl.BlockSpec((1, C), const),
                  pl.BlockSpec((1, C), const)],
        out_specs=pl.BlockSpec((1, tc, C), lambda b, i: (b, i, 0)),
        out_shape=jax.ShapeDtypeStruct((B, S, C), BF16),
        scratch_shapes=[pltpu.VMEM((tc + 2 * _HALO, C), F32)],
        compiler_params=_cparams("parallel", "parallel"),
    )(glu, glu, glu, lw["conv_w"], lw["conv_b"], lw["conv_ln_g"], lw["conv_ln_b"])


def _oproj_kernel(h_ref, oa_ref, ob_ref, oc_ref, wa_ref, wb_ref, wc_ref, g_ref, b_ref, o_ref):
    mix = jnp.dot(oa_ref[...], wa_ref[...], preferred_element_type=F32)
    mix = mix + jnp.dot(ob_ref[...], wb_ref[...], preferred_element_type=F32)
    mix = mix + jnp.dot(oc_ref[...], wc_ref[...], preferred_element_type=F32)
    o_ref[...] = _ln_rows(DN_ALPHA * h_ref[...] + mix, g_ref[...], b_ref[...])


def _out_proj(h, oa, ob, oc, lw):
    n, d = h.shape
    tm = _row_tile(n, 512)
    const = lambda i: (0, 0)
    row = lambda i: (i, 0)
    return pl.pallas_call(
        _oproj_kernel,
        grid=(n // tm,),
        in_specs=[pl.BlockSpec((tm, d), row),
                  pl.BlockSpec((tm, oa.shape[1]), row),
                  pl.BlockSpec((tm, ob.shape[1]), row),
                  pl.BlockSpec((tm, oc.shape[1]), row),
                  pl.BlockSpec(lw["wo_a"].shape, const),
                  pl.BlockSpec(lw["wo_b"].shape, const),
                  pl.BlockSpec(lw["wo_c"].shape, const),
                  pl.BlockSpec((1, d), const),
                  pl.BlockSpec((1, d), const)],
        out_specs=pl.BlockSpec((tm, d), row),
        out_shape=jax.ShapeDtypeStruct((n, d), F32),
        compiler_params=_cparams("parallel"),
    )(h, oa, ob, oc, lw["wo_a"], lw["wo_b"], lw["wo_c"], lw["ln1_g"], lw["ln1_b"])


def _memkv_kernel(x_ref, w_ref, o_ref):
    o_ref[...] = jnp.dot(x_ref[...].astype(BF16), w_ref[...], preferred_element_type=F32).astype(o_ref.dtype)


def _mem_kv(mem2d, w_ckv):
    n, d = mem2d.shape
    tm = _row_tile(n, 256)
    return pl.pallas_call(
        _memkv_kernel,
        grid=(n // tm,),
        in_specs=[pl.BlockSpec((tm, d), lambda i: (i, 0)),
                  pl.BlockSpec(w_ckv.shape, lambda i: (0, 0))],
        out_specs=pl.BlockSpec((tm, w_ckv.shape[1]), lambda i: (i, 0)),
        out_shape=jax.ShapeDtypeStruct((n, w_ckv.shape[1]), BF16),
        compiler_params=_cparams("parallel"),
    )(mem2d, w_ckv)


def _xattn_kernel(h_ref, k_ref, v_ref, wq_ref, wo_ref, g_ref, b_ref, o_ref):
    h = h_ref[0]
    q = (jnp.dot(h.astype(BF16), wq_ref[...], preferred_element_type=F32) * XA_SCALE).astype(BF16)
    heads = []
    for hd in range(XA_HEADS):
        sl = slice(hd * XA_HD, (hd + 1) * XA_HD)
        s = lax.dot_general(q[:, sl], k_ref[0, :, sl], (((1,), (1,)), ((), ())),
                            preferred_element_type=F32)
        p = jnp.exp(s - jnp.max(s, axis=1, keepdims=True))
        o = jnp.dot(p.astype(BF16), v_ref[0, :, sl], preferred_element_type=F32)
        heads.append((o / jnp.sum(p, axis=1, keepdims=True)).astype(BF16))
    o = jnp.concatenate(heads, axis=1)
    y = jnp.dot(o, wo_ref[...], preferred_element_type=F32)
    o_ref[0] = _ln_rows(DN_ALPHA * h + y, g_ref[...], b_ref[...])


def _cross_attention(h, kvmem, lw):
    B, S, D = h.shape
    M = kvmem.shape[1]
    tm = _row_tile(S, 512)
    const = lambda b, i: (0, 0)
    return pl.pallas_call(
        _xattn_kernel,
        grid=(B, S // tm),
        in_specs=[pl.BlockSpec((1, tm, D), lambda b, i: (b, i, 0)),
                  pl.BlockSpec((1, M, D), lambda b, i: (b, 0, 0)),
                  pl.BlockSpec((1, M, D), lambda b, i: (b, 0, 1)),
                  pl.BlockSpec((D, D), const),
                  pl.BlockSpec((D, D), const),
                  pl.BlockSpec((1, D), const),
                  pl.BlockSpec((1, D), const)],
        out_specs=pl.BlockSpec((1, tm, D), lambda b, i: (b, i, 0)),
        out_shape=jax.ShapeDtypeStruct((B, S, D), F32),
        compiler_params=_cparams("parallel", "parallel"),
    )(h, kvmem, kvmem, lw["w_cq"], lw["w_co"], lw["ln2_g"], lw["ln2_b"])


MOE_TILE = 1024
MOE_ROWS = 256
MOE_ALIGN = 16


def _router_kernel(x_ref, wr_ref, bias_ref, gate_ref, pos_ref, dd_ref, cnt_ref, start_ref, sel_sc):
    T = x_ref.shape[0]
    logits = lax.dot_general(wr_ref[...], x_ref[...], (((1,), (1,)), ((), ())),
                             precision=lax.Precision.HIGHEST, preferred_element_type=F32)
    scores = jax.nn.sigmoid(logits)
    sel = scores + bias_ref[...]
    sc = [scores[e:e + 1, :] for e in range(N_EXPERTS)]
    se = [sel[e:e + 1, :] for e in range(N_EXPERTS)]

    gs = []
    for g in range(N_GROUPS):
        m = [se[g * EXPERTS_PER_GROUP + k] for k in range(EXPERTS_PER_GROUP)]
        best = None
        for a in range(EXPERTS_PER_GROUP):
            for b in range(a + 1, EXPERTS_PER_GROUP):
                pair = m[a] + m[b]
                best = pair if best is None else jnp.maximum(best, pair)
        gs.append(best)
    in_group = []
    for g in range(N_GROUPS):
        ok = None
        for o in range(N_GROUPS):
            if o == g:
                continue
            c = (gs[g] > gs[o]) if o < g else (gs[g] >= gs[o])
            ok = c if ok is None else (ok & c)
        in_group.append(ok)
    chosen = []
    for e in range(N_EXPERTS):
        g = e // EXPERTS_PER_GROUP
        beaten = jnp.zeros((1, T), jnp.int32)
        for o in range(g * EXPERTS_PER_GROUP, (g + 1) * EXPERTS_PER_GROUP):
            if o == e:
                continue
            c = (se[o] > se[e]) if o > e else (se[o] >= se[e])
            beaten = beaten + c.astype(jnp.int32)
        chosen.append(in_group[g] & (beaten < 2))
    denom = jnp.zeros((1, T), F32)
    for e in range(N_EXPERTS):
        denom = denom + jnp.where(chosen[e], sc[e], 0.0)
    for e in range(N_EXPERTS):
        gate_ref[e:e + 1, :] = jnp.where(chosen[e], sc[e] / denom, 0.0)
        sel_sc[e:e + 1, :] = chosen[e].astype(F32)

    selm = sel_sc[...]
    earlier = (lax.broadcasted_iota(jnp.int32, (T, T), 0) < lax.broadcasted_iota(jnp.int32, (T, T), 1))
    rank = jnp.dot(selm.astype(BF16), earlier.astype(BF16), preferred_element_type=F32)
    cnt = jnp.sum(selm, axis=1, keepdims=True)
    cnt_ref[0] = jnp.broadcast_to(cnt, (N_EXPERTS, LANE)).astype(jnp.int32)

    seg = jnp.floor((cnt + (MOE_ALIGN - 1)) * (1.0 / MOE_ALIGN)) * MOE_ALIGN
    start = jnp.zeros((1, 1), F32)
    d_lo = jnp.full((1, T), float(4 * T), F32)
    d_hi = jnp.full((1, T), -1.0, F32)
    for e in range(N_EXPERTS):
        start_ref[0, e:e + 1, :] = jnp.broadcast_to(start, (1, LANE)).astype(jnp.int32)
        dest = jnp.where(chosen[e], rank[e:e + 1, :] + start, -1.0)
        pos_ref[e:e + 1, :] = dest
        d_lo = jnp.where(chosen[e], jnp.minimum(d_lo, dest), d_lo)
        d_hi = jnp.maximum(d_hi, dest)
        start = start + seg[e:e + 1, :]
    dd_ref[...] = jnp.zeros(dd_ref.shape, F32)
    dd_ref[0:1, :] = d_lo
    dd_ref[1:2, :] = d_hi


def _router(x, w_rt, bias):
    n, d = x.shape
    T = _row_tile(n, MOE_TILE)
    nt = n // T
    gate, pos, dd, cnt, start = pl.pallas_call(
        _router_kernel,
        grid=(nt,),
        in_specs=[pl.BlockSpec((T, d), lambda i: (i, 0)),
                  pl.BlockSpec((N_EXPERTS, d), lambda i: (0, 0)),
                  pl.BlockSpec((N_EXPERTS, 1), lambda i: (0, 0))],
        out_specs=[pl.BlockSpec((N_EXPERTS, T), lambda i: (0, i)),
                   pl.BlockSpec((N_EXPERTS, T), lambda i: (0, i)),
                   pl.BlockSpec((8, T), lambda i: (0, i)),
                   pl.BlockSpec((1, N_EXPERTS, LANE), lambda i: (i, 0, 0)),
                   pl.BlockSpec((1, N_EXPERTS, LANE), lambda i: (i, 0, 0))],
        out_shape=[jax.ShapeDtypeStruct((N_EXPERTS, n), F32),
                   jax.ShapeDtypeStruct((N_EXPERTS, n), F32),
                   jax.ShapeDtypeStruct((8, n), F32),
                   jax.ShapeDtypeStruct((nt, N_EXPERTS, LANE), jnp.int32),
                   jax.ShapeDtypeStruct((nt, N_EXPERTS, LANE), jnp.int32)],
        scratch_shapes=[pltpu.VMEM((N_EXPERTS, T), F32)],
        compiler_params=_cparams("parallel"),
    )(x, w_rt, bias)
    return gate, pos, dd, cnt[:, :, 0].reshape(-1), start[:, :, 0].reshape(-1)


def _sorted_rows(T):
    return 2 * T + N_EXPERTS * MOE_ALIGN + MOE_ROWS


def _moe_kernel(cnt_ref, start_ref, h_ref, gate_ref, pos_ref, dd_ref, wg_ref, wu_ref, wd_ref, g_ref, b_ref, o_ref,
                p_sc, xs_sc, ys_sc):
    i = pl.program_id(0)
    e = pl.program_id(1)
    T = h_ref.shape[0]
    R = p_sc.shape[0]

    @pl.when(e == 0)
    def _():
        d_lo = dd_ref[0:1, :].astype(jnp.int32)
        d_hi = dd_ref[1:2, :].astype(jnp.int32)
        xb = h_ref[...].astype(BF16)
        for r0 in range(0, R, MOE_ROWS):
            row = lax.broadcasted_iota(jnp.int32, (MOE_ROWS, T), 0) + r0
            onehot = ((row == d_lo) | (row == d_hi)).astype(BF16)
            p_sc[r0:r0 + MOE_ROWS, :] = onehot
            xs_sc[r0:r0 + MOE_ROWS, :] = jnp.dot(onehot, xb, preferred_element_type=F32).astype(BF16)
        ys_sc[...] = jnp.zeros(ys_sc.shape, BF16)

    n_rows = cnt_ref[i * N_EXPERTS + e]
    seg0 = start_ref[i * N_EXPERTS + e]
    n_blocks = (n_rows + MOE_ROWS - 1) // MOE_ROWS
    pos_row = pos_ref[pl.ds(e, 1), :].astype(jnp.int32)
    gate_row = gate_ref[pl.ds(e, 1), :]
    slot = lax.broadcasted_iota(jnp.int32, (MOE_ROWS, T), 0)

    def block(jb, carry):
        base = pl.multiple_of(seg0 + jb * MOE_ROWS, MOE_ALIGN)
        xg = xs_sc[pl.ds(base, MOE_ROWS), :]
        hg = jnp.dot(xg, wg_ref[0], preferred_element_type=F32)
        hu = jnp.dot(xg, wu_ref[0], preferred_element_type=F32)
        act = (hg * jax.nn.sigmoid(hg) * hu).astype(BF16)
        y = jnp.dot(act, wd_ref[0], preferred_element_type=F32)
        hit = (pos_row - base) == slot
        g_rows = jnp.sum(jnp.where(hit, gate_row, 0.0), axis=1, keepdims=True)
        ys_sc[pl.ds(base, MOE_ROWS), :] = (y * g_rows).astype(BF16)
        return carry

    lax.fori_loop(0, n_blocks, block, 0)

    @pl.when(e == N_EXPERTS - 1)
    def _():
        moe = lax.dot_general(p_sc[...], ys_sc[...], (((0,), (0,)), ((), ())), preferred_element_type=F32)
        o_ref[...] = _ln_rows(DN_ALPHA * h_ref[...] + moe, g_ref[...], b_ref[...])


def _moe(h, gate, pos, dd, cnt, start, lw):
    n, d = h.shape
    T = _row_tile(n, MOE_TILE)
    R = _sorted_rows(T)
    f = D_FF_EXPERT
    grid_spec = pltpu.PrefetchScalarGridSpec(
        num_scalar_prefetch=2,
        grid=(n // T, N_EXPERTS),
        in_specs=[pl.BlockSpec((T, d), lambda i, e, c, s: (i, 0)),
                  pl.BlockSpec((N_EXPERTS, T), lambda i, e, c, s: (0, i)),
                  pl.BlockSpec((N_EXPERTS, T), lambda i, e, c, s: (0, i)),
                  pl.BlockSpec((8, T), lambda i, e, c, s: (0, i)),
                  pl.BlockSpec((1, d, f), lambda i, e, c, s: (e, 0, 0)),
                  pl.BlockSpec((1, d, f), lambda i, e, c, s: (e, 0, 0)),
                  pl.BlockSpec((1, f, d), lambda i, e, c, s: (e, 0, 0)),
                  pl.BlockSpec((1, d), lambda i, e, c, s: (0, 0)),
                  pl.BlockSpec((1, d), lambda i, e, c, s: (0, 0))],
        out_specs=pl.BlockSpec((T, d), lambda i, e, c, s: (i, 0)),
        scratch_shapes=[pltpu.VMEM((R, T), BF16), pltpu.VMEM((R, d), BF16), pltpu.VMEM((R, d), BF16)],
    )
    return pl.pallas_call(
        _moe_kernel,
        grid_spec=grid_spec,
        out_shape=jax.ShapeDtypeStruct((n, d), F32),
        compiler_params=_cparams("parallel", "arbitrary"),
    )(cnt, start, h, gate, pos, dd, lw["w_gate"], lw["w_up"], lw["w_down"], lw["ln3_g"], lw["ln3_b"])


def _prep_layer(l, w):
    ext_idx, ext_sgn = _w_ext_index()
    uq_idx, uq_sgn = _w_uq_index()
    mla_w = MLA_HEADS * MLA_V
    gq = w["gqa_q_norm"][l]
    gk = w["gqa_k_norm"][l]
    row = lambda v: v.reshape(1, -1).astype(F32)
    return {
        "w_ext": _take_cols(w["w_in"][l], ext_idx, ext_sgn).astype(BF16),
        "w_uq": _take_cols(w["mla_w_uq"][l], uq_idx, uq_sgn).astype(BF16),
        "w_ukv": _take_cols(w["mla_w_ukv"][l], _w_ukv_index()).astype(BF16),
        "qn": row(w["mla_q_norm"][l]),
        "kvn": row(w["mla_kv_norm"][l]),
        "gq": row(jnp.tile(gq, 2)),
        "gqp": row(jnp.tile(_perm_gain(gq, GQA_HD // 4), 2)),
        "gk": row(jnp.tile(gk, 2)),
        "gkp": row(jnp.tile(_perm_gain(gk, GQA_HD // 4), 2)),
        "conv_w": w["conv_w"][l].astype(F32),
        "conv_b": row(w["conv_b"][l]),
        "conv_ln_g": row(w["conv_ln_g"][l]),
        "conv_ln_b": row(w["conv_ln_b"][l]),
        "wo_a": w["w_o"][l][:mla_w].astype(BF16),
        "wo_b": w["w_o"][l][mla_w:mla_w + CONV_CH].astype(BF16),
        "wo_c": w["w_o"][l][mla_w + CONV_CH:][_w_oc_index()].astype(BF16),
        "ln1_g": row(w["ln1_g"][l]), "ln1_b": row(w["ln1_b"][l]),
        "w_cq": w["w_cq"][l].astype(BF16),
        "w_ckv": w["w_ckv"][l].astype(BF16),
        "w_co": w["w_co"][l].astype(BF16),
        "ln2_g": row(w["ln2_g"][l]), "ln2_b": row(w["ln2_b"][l]),
        "w_gate": w["w_gate"][l].astype(BF16),
        "w_up": w["w_up"][l].astype(BF16),
        "w_down": w["w_down"][l].astype(BF16),
        "ln3_g": row(w["ln3_g"][l]), "ln3_b": row(w["ln3_b"][l]),
    }


def _trunk(x, mem, w, layers):
    B, S, D = x.shape
    n = B * S
    tab = _rope_tables(S)
    w_rt = w["w_router"].T.astype(F32)
    r_bias = w["router_bias"].reshape(N_EXPERTS, 1).astype(F32)
    h = _layer_norm_rows(x.reshape(n, D), w["ln_in_g"], w["ln_in_b"])
    mem2d = mem.reshape(B * mem.shape[1], D)
    for lw in layers:
        qa, kva, glu, qg, kvg = _mixer_pre(h.reshape(B, S, D), tab, lw)
        oa = _mla_attention(qa, kva)
        ob = _conv_module(glu, lw)
        oc = _gqa_attention(qg, kvg)
        h = _out_proj(h, oa.reshape(n, -1), ob.reshape(n, -1), oc.reshape(n, -1), lw)
        kvmem = _mem_kv(mem2d, lw["w_ckv"]).reshape(B, mem.shape[1], 2 * D)
        h = _cross_attention(h.reshape(B, S, D), kvmem, lw).reshape(n, D)
        gate, pos, dd, cnt, start = _router(h, w_rt, r_bias)
        h = _moe(h, gate, pos, dd, cnt, start, lw)
    return h.reshape(B, S, D)


def kernel(x_prompt, x_sample, mem_prompt, mem_sample, ln_in_g, ln_in_b, w_in, mla_q_norm, mla_w_uq,
           mla_kv_norm, mla_w_ukv, conv_w, conv_b, conv_ln_g, conv_ln_b, gqa_q_norm, gqa_k_norm, w_o,
           ln1_g, ln1_b, w_cq, w_ckv, w_co, ln2_g, ln2_b, w_router, router_bias, w_gate, w_up, w_down,
           ln3_g, ln3_b):
    w = dict(ln_in_g=ln_in_g, ln_in_b=ln_in_b, w_in=w_in, mla_q_norm=mla_q_norm, mla_w_uq=mla_w_uq,
             mla_kv_norm=mla_kv_norm, mla_w_ukv=mla_w_ukv, conv_w=conv_w, conv_b=conv_b,
             conv_ln_g=conv_ln_g, conv_ln_b=conv_ln_b, gqa_q_norm=gqa_q_norm, gqa_k_norm=gqa_k_norm,
             w_o=w_o, ln1_g=ln1_g, ln1_b=ln1_b, w_cq=w_cq, w_ckv=w_ckv, w_co=w_co, ln2_g=ln2_g,
             ln2_b=ln2_b, w_router=w_router, router_bias=router_bias, w_gate=w_gate, w_up=w_up,
             w_down=w_down, ln3_g=ln3_g, ln3_b=ln3_b)
    layers = [_prep_layer(l, w) for l in range(w_in.shape[0])]
    y_prompt = _trunk(x_prompt, mem_prompt, w, layers)
    y_sample = _trunk(x_sample, mem_sample, w, layers)
    return (y_prompt, y_sample)
```

```python
import functools

import numpy as np
import jax
import jax.numpy as jnp
from jax import lax
from jax.experimental import pallas as pl
from jax.experimental.pallas import tpu as pltpu

F32 = jnp.float32
BF16 = jnp.bfloat16

D_MODEL = 1024
DEPTH = 4
GRID_W = 64
ROPE_THETA = 10000.0
LN_EPS = 1e-5
RMS_EPS = 1e-6
MLA_HEADS = 6
MLA_Q_LORA = 256
MLA_KV_LORA = 128
MLA_NOPE = 64
MLA_ROPE = 32
MLA_V = 64
CONV_CH = 256
CONV_K = 31
GQA_HEADS = 6
GQA_KV_HEADS = 2
GQA_HD = 64
SPLITS = (MLA_Q_LORA, MLA_KV_LORA, MLA_ROPE, 2 * CONV_CH,
          GQA_HEADS * GQA_HD, GQA_KV_HEADS * GQA_HD, GQA_KV_HEADS * GQA_HD)
XA_HEADS = 4
XA_HD = D_MODEL // XA_HEADS
N_EXPERTS = 16
N_GROUPS = 4
EXPERTS_PER_GROUP = N_EXPERTS // N_GROUPS
D_FF_EXPERT = 512
DN_ALPHA = (2 * DEPTH) ** 0.25
LOG2E = 1.4426950408889634
MLA_SCALE = (MLA_NOPE + MLA_ROPE) ** -0.5 * LOG2E
GQA_SCALE = GQA_HD ** -0.5 * LOG2E
XA_SCALE = XA_HD ** -0.5

LANE = 128
HALF = LANE // 2
SUBLANE = 8
VMEM_LIMIT = 56 * 1024 * 1024

_OFF_CQ = 0
_OFF_CKV = 256
_OFF_KR = 384
_OFF_KRR = 512
_OFF_GA = 640
_OFF_GG = 896
_OFF_QC = 1152
_OFF_QCR = 1536
_OFF_KC = 1920
_OFF_KCR = 2048
_OFF_VC = 2176
_W_EXT = 2432
_GQA_BLOCK_HEADS = (0, 3, 1, 4, 2, 5)
_SUM_LANE = (HALF, 0)


def _cparams(*sem):
    return pltpu.CompilerParams(dimension_semantics=sem, vmem_limit_bytes=VMEM_LIMIT)


def _row_tile(n, want):
    t = min(n, want)
    while n % t:
        t //= 2
    return t


def _ln_rows(x, g, b):
    mu = jnp.mean(x, axis=-1, keepdims=True)
    xc = x - mu
    var = jnp.mean(xc * xc, axis=-1, keepdims=True)
    return xc * lax.rsqrt(var + LN_EPS) * g + b


def _take_cols(w, idx, sign=None):
    w_aug = jnp.concatenate([w, jnp.zeros((w.shape[0], 1), w.dtype)], axis=1)
    idx = np.where(idx < 0, w.shape[1], idx)
    out = w_aug[:, idx]
    if sign is not None:
        out = out * jnp.asarray(sign, w.dtype)[None, :]
    return out


def _rot_src(d, half):
    r = d % (2 * half)
    base = d - r
    if r < half:
        return base + r + half, -1.0
    return base + r - half, 1.0


def _w_ext_index():
    cut = np.concatenate([[0], np.cumsum(SPLITS)])
    c_cq, c_ckv, c_kr, c_glu, c_qc, c_kc, c_vc = [int(c) for c in cut[:-1]]
    idx = -np.ones((_W_EXT,), np.int64)
    sgn = np.ones((_W_EXT,), np.float32)
    idx[_OFF_CQ:_OFF_CQ + 256] = c_cq + np.arange(256)
    idx[_OFF_CKV:_OFF_CKV + 128] = c_ckv + np.arange(128)
    for d in range(MLA_ROPE):
        idx[_OFF_KR + MLA_NOPE + d] = c_kr + d
        s, sg = _rot_src(d, MLA_ROPE // 2)
        idx[_OFF_KRR + MLA_NOPE + d] = c_kr + s
        sgn[_OFF_KRR + MLA_NOPE + d] = sg
    idx[_OFF_GA:_OFF_GA + 256] = c_glu + np.arange(256)
    idx[_OFF_GG:_OFF_GG + 256] = c_glu + 256 + np.arange(256)
    for blk, hq in enumerate(_GQA_BLOCK_HEADS):
        col0 = (blk // 2) * LANE + (blk % 2) * HALF
        for d in range(GQA_HD):
            idx[_OFF_QC + col0 + d] = c_qc + hq * GQA_HD + d
            s, sg = _rot_src(d, GQA_HD // 4)
            idx[_OFF_QCR + col0 + d] = c_qc + hq * GQA_HD + s
            sgn[_OFF_QCR + col0 + d] = sg
    for c in range(GQA_KV_HEADS * GQA_HD):
        hk, d = divmod(c, GQA_HD)
        idx[_OFF_KC + c] = c_kc + c
        s, sg = _rot_src(d, GQA_HD // 4)
        idx[_OFF_KCR + c] = c_kc + hk * GQA_HD + s
        sgn[_OFF_KCR + c] = sg
    idx[_OFF_VC:_OFF_VC + HALF] = c_vc + np.arange(HALF)
    idx[_OFF_VC + LANE + HALF:_OFF_VC + 2 * LANE] = c_vc + HALF + np.arange(HALF)
    return idx, sgn


def _w_uq_index():
    per = MLA_NOPE + MLA_ROPE
    idx = -np.ones((2 * MLA_HEADS * LANE,), np.int64)
    sgn = np.ones(idx.shape, np.float32)
    for h in range(MLA_HEADS):
        idx[h * LANE:h * LANE + per] = h * per + np.arange(per)
        for d in range(MLA_ROPE):
            s, sg = _rot_src(d, MLA_ROPE // 2)
            idx[(MLA_HEADS + h) * LANE + MLA_NOPE + d] = h * per + MLA_NOPE + s
            sgn[(MLA_HEADS + h) * LANE + MLA_NOPE + d] = sg
    return idx, sgn


def _w_ukv_index():
    per = MLA_NOPE + MLA_V
    idx = -np.ones((MLA_HEADS * 2 * LANE,), np.int64)
    for h in range(MLA_HEADS):
        v0 = h * 2 * LANE + (0 if h % 2 == 0 else HALF)
        idx[v0:v0 + MLA_V] = h * per + MLA_NOPE + np.arange(MLA_V)
        idx[h * 2 * LANE + LANE:h * 2 * LANE + LANE + MLA_NOPE] = h * per + np.arange(MLA_NOPE)
    return idx


def _w_oc_index():
    idx = np.zeros((GQA_HEADS * GQA_HD,), np.int64)
    for blk, hq in enumerate(_GQA_BLOCK_HEADS):
        c0 = (blk // 2) * LANE + (blk % 2) * HALF
        idx[c0:c0 + GQA_HD] = hq * GQA_HD + np.arange(GQA_HD)
    return idx


def _perm_gain(g, half):
    src = np.array([_rot_src(d, half)[0] for d in range(g.shape[-1])])
    return g[src]


def _rope_tables(S):
    def cs(pos, dim):
        inv = 1.0 / (ROPE_THETA ** (jnp.arange(0, dim, 2, dtype=F32) / dim))
        ang = pos[:, None] * inv[None, :]
        return jnp.cos(ang), jnp.sin(ang)

    c1, s1 = cs(jnp.arange(S, dtype=F32), MLA_ROPE)
    rows = S // GRID_W
    row = jnp.repeat(jnp.arange(rows, dtype=F32), GRID_W)
    col = jnp.tile(jnp.arange(GRID_W, dtype=F32), rows)
    cr, sr = cs(row, GQA_HD // 2)
    cc, sc = cs(col, GQA_HD // 2)
    zpad = jnp.zeros((S, LANE - MLA_NOPE - MLA_ROPE), F32)
    cm = jnp.concatenate([jnp.ones((S, MLA_NOPE), F32), c1, c1, zpad], axis=1)
    sm = jnp.concatenate([jnp.zeros((S, MLA_NOPE), F32), s1, s1, zpad], axis=1)
    cg = jnp.concatenate([cr, cr, cc, cc] * 2, axis=1)
    sg = jnp.concatenate([sr, sr, sc, sc] * 2, axis=1)
    return jnp.concatenate([cm, sm, cg, sg], axis=1)


def _ln_kernel(x_ref, g_ref, b_ref, o_ref):
    o_ref[...] = _ln_rows(x_ref[...], g_ref[...], b_ref[...])


def _layer_norm_rows(x, g, b):
    n, d = x.shape
    tm = _row_tile(n, 1024)
    return pl.pallas_call(
        _ln_kernel,
        grid=(n // tm,),
        in_specs=[pl.BlockSpec((tm, d), lambda i: (i, 0)),
                  pl.BlockSpec((1, d), lambda i: (0, 0)),
                  pl.BlockSpec((1, d), lambda i: (0, 0))],
        out_specs=pl.BlockSpec((tm, d), lambda i: (i, 0)),
        out_shape=jax.ShapeDtypeStruct((n, d), F32),
        compiler_params=_cparams("parallel"),
    )(x, g.reshape(1, d), b.reshape(1, d))


def _rms_scale(ss, width):
    return lax.rsqrt(ss * (1.0 / width) + RMS_EPS)


def _pre_kernel(h_ref, tab_ref, wext_ref, wuq_ref, wukv_ref, qn_ref, kvn_ref,
                gq_ref, gqp_ref, gk_ref, gkp_ref,
                qa_ref, kva_ref, glu_ref, qg_ref, kvg_ref):
    hb = h_ref[0].astype(BF16)
    z = jnp.dot(hb, wext_ref[...], preferred_element_type=F32)
    cm = tab_ref[:, 0:LANE]
    sm = tab_ref[:, LANE:2 * LANE]
    cg = tab_ref[:, 2 * LANE:3 * LANE]
    sg = tab_ref[:, 3 * LANE:4 * LANE]

    cq = z[:, _OFF_CQ:_OFF_CQ + MLA_Q_LORA]
    cqn = cq * _rms_scale(jnp.sum(cq * cq, axis=-1, keepdims=True), MLA_Q_LORA) * qn_ref[...]
    a = jnp.dot(cqn.astype(BF16), wuq_ref[...], preferred_element_type=F32)
    rot0 = MLA_HEADS * LANE
    for h in range(MLA_HEADS):
        roped = a[:, h * LANE:(h + 1) * LANE] * cm + a[:, rot0 + h * LANE:rot0 + (h + 1) * LANE] * sm
        qa_ref[0, :, h * LANE:(h + 1) * LANE] = (roped * MLA_SCALE).astype(BF16)

    ckv = z[:, _OFF_CKV:_OFF_CKV + MLA_KV_LORA]
    ckvn = ckv * _rms_scale(jnp.sum(ckv * ckv, axis=-1, keepdims=True), MLA_KV_LORA) * kvn_ref[...]
    kvm = jnp.dot(ckvn.astype(BF16), wukv_ref[...], preferred_element_type=F32)
    kr = z[:, _OFF_KR:_OFF_KR + LANE] * cm + z[:, _OFF_KRR:_OFF_KRR + LANE] * sm
    lane = lax.broadcasted_iota(jnp.int32, (1, LANE), 1)
    one_at = [jnp.where(lane == L, 1.0, 0.0) for L in _SUM_LANE]
    for h in range(MLA_HEADS):
        c0 = h * 2 * LANE
        kva_ref[0, :, c0:c0 + LANE] = (kvm[:, c0:c0 + LANE] + one_at[h % 2]).astype(BF16)
        kva_ref[0, :, c0 + LANE:c0 + 2 * LANE] = (kvm[:, c0 + LANE:c0 + 2 * LANE] + kr).astype(BF16)

    glu_ref[0] = z[:, _OFF_GA:_OFF_GA + CONV_CH] * jax.nn.sigmoid(z[:, _OFF_GG:_OFF_GG + CONV_CH])

    low = lax.broadcasted_iota(jnp.int32, (h_ref.shape[1], LANE), 1) < HALF

    def norm_rope(x, x_rot, g, g_rot):
        sq = x * x
        r0 = _rms_scale(jnp.sum(jnp.where(low, sq, 0.0), axis=-1, keepdims=True), GQA_HD)
        r1 = _rms_scale(jnp.sum(jnp.where(low, 0.0, sq), axis=-1, keepdims=True), GQA_HD)
        return (x * g * cg + x_rot * g_rot * sg) * jnp.where(low, r0, r1)

    gq = gq_ref[...]
    gqp = gqp_ref[...]
    for p in range(GQA_HEADS // 2):
        qn = norm_rope(z[:, _OFF_QC + p * LANE:_OFF_QC + (p + 1) * LANE],
                       z[:, _OFF_QCR + p * LANE:_OFF_QCR + (p + 1) * LANE], gq, gqp) * GQA_SCALE
        qg_ref[0, :, 2 * p * LANE:(2 * p + 1) * LANE] = jnp.where(low, qn, 0.0).astype(BF16)
        qg_ref[0, :, (2 * p + 1) * LANE:(2 * p + 2) * LANE] = jnp.where(low, 0.0, qn).astype(BF16)

    kvg_ref[0, :, 0:LANE] = norm_rope(z[:, _OFF_KC:_OFF_KC + LANE], z[:, _OFF_KCR:_OFF_KCR + LANE],
                                      gk_ref[...], gkp_ref[...]).astype(BF16)
    kvg_ref[0, :, LANE:2 * LANE] = (z[:, _OFF_VC:_OFF_VC + LANE] + one_at[0]).astype(BF16)
    kvg_ref[0, :, 2 * LANE:3 * LANE] = (z[:, _OFF_VC + LANE:_OFF_VC + 2 * LANE] + one_at[1]).astype(BF16)


def _mixer_pre(h, tab, lw):
    B, S, D = h.shape
    tm = _row_tile(S, 512)
    const = lambda b, i: (0, 0)
    row3 = lambda b, i: (b, i, 0)
    outs = [(MLA_HEADS * LANE, BF16), (MLA_HEADS * 2 * LANE, BF16), (CONV_CH, F32),
            (GQA_HEADS * LANE, BF16), (3 * LANE, BF16)]
    return pl.pallas_call(
        _pre_kernel,
        grid=(B, S // tm),
        in_specs=[pl.BlockSpec((1, tm, D), row3),
                  pl.BlockSpec((tm, 4 * LANE), lambda b, i: (i, 0)),
                  pl.BlockSpec(lw["w_ext"].shape, const),
                  pl.BlockSpec(lw["w_uq"].shape, const),
                  pl.BlockSpec(lw["w_ukv"].shape, const),
                  pl.BlockSpec((1, MLA_Q_LORA), const),
                  pl.BlockSpec((1, MLA_KV_LORA), const),
                  pl.BlockSpec((1, LANE), const),
                  pl.BlockSpec((1, LANE), const),
                  pl.BlockSpec((1, LANE), const),
                  pl.BlockSpec((1, LANE), const)],
        out_specs=[pl.BlockSpec((1, tm, w), row3) for w, _ in outs],
        out_shape=[jax.ShapeDtypeStruct((B, S, w), dt) for w, dt in outs],
        compiler_params=_cparams("parallel", "parallel"),
    )(h, tab, lw["w_ext"], lw["w_uq"], lw["w_ukv"], lw["qn"], lw["kvn"],
      lw["gq"], lw["gqp"], lw["gk"], lw["gkp"])


def _attn_kernel(q_ref, kv_ref, o_ref, m_sc, acc_sc, s_sc, *, k_off, v_off, tkc):
    j = pl.program_id(3)
    tk = kv_ref.shape[1]
    units = [(i, c) for c in range(tk // tkc) for i in range(2)]
    n_col = tkc // LANE

    @pl.when(j == 0)
    def _():
        m_sc[...] = jnp.full(m_sc.shape, -jnp.inf, F32)
        acc_sc[...] = jnp.zeros(acc_sc.shape, F32)

    def scores(u):
        i, c = units[u]
        q = q_ref[0, :, i * LANE:(i + 1) * LANE]
        k = kv_ref[0, c * tkc:(c + 1) * tkc, k_off[i]:k_off[i] + LANE]
        return lax.dot_general(q, k, (((1,), (1,)), ((), ())), preferred_element_type=F32)

    s_sc[0] = scores(0)
    for u, (i, c) in enumerate(units):
        if u + 1 < len(units):
            s_sc[(u + 1) % 2] = scores(u + 1)
        cols = [s_sc[u % 2, :, cb * LANE:(cb + 1) * LANE] for cb in range(n_col)]
        mx = cols[0]
        for x in cols[1:]:
            mx = jnp.maximum(mx, x)
        m_prev = m_sc[i]
        m_new = jnp.maximum(m_prev, jnp.max(mx, axis=1, keepdims=True))
        alpha = jnp.exp2(m_prev - m_new)
        p = jnp.concatenate([jnp.exp2(x - m_new).astype(BF16) for x in cols], axis=1)
        v = kv_ref[0, c * tkc:(c + 1) * tkc, v_off[i]:v_off[i] + LANE]
        acc_sc[i] = alpha * acc_sc[i] + jnp.dot(p, v, preferred_element_type=F32)
        m_sc[i] = m_new

    @pl.when(j == pl.num_programs(3) - 1)
    def _():
        o0 = acc_sc[0] / acc_sc[0, :, _SUM_LANE[0]:_SUM_LANE[0] + 1]
        o1 = acc_sc[1] / acc_sc[1, :, _SUM_LANE[1]:_SUM_LANE[1] + 1]
        low = lax.broadcasted_iota(jnp.int32, o0.shape, 1) < HALF
        o_ref[0] = jnp.where(low, o0, o1).astype(o_ref.dtype)


ATTN_TQ = 1024
ATTN_TK = 4096
ATTN_TKC = 512


def _attention(q, kv, *, kvw_block, kv_block_of_pair, k_off, v_off):
    B, S, _ = q.shape
    n_pairs = 3
    tq = _row_tile(S, ATTN_TQ)
    tk = _row_tile(S, ATTN_TK)
    tkc = _row_tile(tk, ATTN_TKC)
    kern = functools.partial(_attn_kernel, k_off=k_off, v_off=v_off, tkc=tkc)
    return pl.pallas_call(
        kern,
        grid=(B, n_pairs, S // tq, S // tk),
        in_specs=[pl.BlockSpec((1, tq, 2 * LANE), lambda b, p, i, j: (b, i, p)),
                  pl.BlockSpec((1, tk, kvw_block), lambda b, p, i, j: (b, j, kv_block_of_pair(p)))],
        out_specs=pl.BlockSpec((1, tq, LANE), lambda b, p, i, j: (b, i, p)),
        out_shape=jax.ShapeDtypeStruct((B, S, n_pairs * LANE), BF16),
        scratch_shapes=[pltpu.VMEM((2, tq, LANE), F32),
                        pltpu.VMEM((2, tq, LANE), F32),
                        pltpu.VMEM((2, tq, tkc), F32)],
        compiler_params=_cparams("parallel", "parallel", "parallel", "arbitrary"),
    )(q, kv)


def _mla_attention(qa, kva):
    return _attention(qa, kva, kvw_block=4 * LANE, kv_block_of_pair=lambda p: p,
                      k_off=(LANE, 3 * LANE), v_off=(0, 2 * LANE))


def _gqa_attention(qg, kvg):
    return _attention(qg, kvg, kvw_block=3 * LANE, kv_block_of_pair=lambda p: 0,
                      k_off=(0, 0), v_off=(LANE, 2 * LANE))


_HALO = 16


def _conv_kernel(prev_ref, cur_ref, next_ref, w_ref, b_ref, g_ref, beta_ref, o_ref, xe_sc, sh_sc):
    i = pl.program_id(1)
    tc = cur_ref.shape[1]
    first = i == 0
    last = i == pl.num_programs(1) - 1
    xe_sc[0:_HALO, :] = jnp.where(first, 0.0, prev_ref[0])
    xe_sc[_HALO:_HALO + tc, :] = cur_ref[0]
    xe_sc[_HALO + tc:_HALO + tc + _HALO, :] = jnp.where(last, 0.0, next_ref[0])
    acc = jnp.zeros((tc, CONV_CH), F32)
    span = sh_sc.shape[0]
    for shift in range(SUBLANE):
        taps = [t for t in range(CONV_K) if (_HALO - CONV_K // 2 + t) % SUBLANE == shift]
        if not taps:
            continue
        sh_sc[...] = xe_sc[shift:shift + span, :]
        for t in taps:
            base = _HALO - CONV_K // 2 + t - shift
            acc = acc + sh_sc[base:base + tc, :] * w_ref[t:t + 1, :]
    u = acc + b_ref[...]
    y = _ln_rows(u, g_ref[...], beta_ref[...])
    o_ref[0] = (y * jax.nn.sigmoid(y)).astype(o_ref.dtype)


def _conv_module(glu, lw):
    B, S, C = glu.shape
    tc = _row_tile(S, 512)
    nh = tc // _HALO
    n_halo_blocks = S // _HALO
    const = lambda b, i: (0, 0)
    return pl.pallas_call(
        _conv_kernel,
        grid=(B, S // tc),
        in_specs=[pl.BlockSpec((1, _HALO, C), lambda b, i: (b, jnp.maximum(i * nh - 1, 0), 0)),
                  pl.BlockSpec((1, tc, C), lambda b, i: (b, i, 0)),
                  pl.BlockSpec((1, _HALO, C), lambda b, i: (b, jnp.minimum((i + 1) * nh, n_halo_blocks - 1), 0)),
                  pl.BlockSpec((CONV_K, C), const),
                  pl.BlockSpec((1, C), const),
                  pl.BlockSpec((1, C), const),
                  pl.BlockSpec((1, C), const)],
        out_specs=pl.BlockSpec((1, tc, C), lambda b, i: (b, i, 0)),
        out_shape=jax.ShapeDtypeStruct((B, S, C), BF16),
        scratch_shapes=[pltpu.VMEM((tc + 2 * _HALO, C), F32),
                        pltpu.VMEM((tc + 2 * _HALO - SUBLANE, C), F32)],
        compiler_params=_cparams("parallel", "parallel"),
    )(glu, glu, glu, lw["conv_w"], lw["conv_b"], lw["conv_ln_g"], lw["conv_ln_b"])


def _oproj_kernel(h_ref, oa_ref, ob_ref, oc_ref, wa_ref, wb_ref, wc_ref, g_ref, b_ref, o_ref):
    mix = jnp.dot(oa_ref[...], wa_ref[...], preferred_element_type=F32)
    mix = mix + jnp.dot(ob_ref[...], wb_ref[...], preferred_element_type=F32)
    mix = mix + jnp.dot(oc_ref[...], wc_ref[...], preferred_element_type=F32)
    o_ref[...] = _ln_rows(DN_ALPHA * h_ref[...] + mix, g_ref[...], b_ref[...])


def _out_proj(h, oa, ob, oc, lw):
    n, d = h.shape
    tm = _row_tile(n, 512)
    const = lambda i: (0, 0)
    row = lambda i: (i, 0)
    return pl.pallas_call(
        _oproj_kernel,
        grid=(n // tm,),
        in_specs=[pl.BlockSpec((tm, d), row),
                  pl.BlockSpec((tm, oa.shape[1]), row),
                  pl.BlockSpec((tm, ob.shape[1]), row),
                  pl.BlockSpec((tm, oc.shape[1]), row),
                  pl.BlockSpec(lw["wo_a"].shape, const),
                  pl.BlockSpec(lw["wo_b"].shape, const),
                  pl.BlockSpec(lw["wo_c"].shape, const),
                  pl.BlockSpec((1, d), const),
                  pl.BlockSpec((1, d), const)],
        out_specs=pl.BlockSpec((tm, d), row),
        out_shape=jax.ShapeDtypeStruct((n, d), F32),
        compiler_params=_cparams("parallel"),
    )(h, oa, ob, oc, lw["wo_a"], lw["wo_b"], lw["wo_c"], lw["ln1_g"], lw["ln1_b"])


def _memkv_kernel(x_ref, w_ref, o_ref):
    o_ref[...] = jnp.dot(x_ref[...].astype(BF16), w_ref[...], preferred_element_type=F32).astype(o_ref.dtype)


def _mem_kv(mem2d, w_ckv):
    n, d = mem2d.shape
    tm = _row_tile(n, 256)
    return pl.pallas_call(
        _memkv_kernel,
        grid=(n // tm,),
        in_specs=[pl.BlockSpec((tm, d), lambda i: (i, 0)),
                  pl.BlockSpec(w_ckv.shape, lambda i: (0, 0))],
        out_specs=pl.BlockSpec((tm, w_ckv.shape[1]), lambda i: (i, 0)),
        out_shape=jax.ShapeDtypeStruct((n, w_ckv.shape[1]), BF16),
        compiler_params=_cparams("parallel"),
    )(mem2d, w_ckv)


def _xattn_kernel(h_ref, k_ref, v_ref, wq_ref, wo_ref, g_ref, b_ref, o_ref):
    h = h_ref[0]
    q = (jnp.dot(h.astype(BF16), wq_ref[...], preferred_element_type=F32) * XA_SCALE).astype(BF16)
    heads = []
    for hd in range(XA_HEADS):
        sl = slice(hd * XA_HD, (hd + 1) * XA_HD)
        s = lax.dot_general(q[:, sl], k_ref[0, :, sl], (((1,), (1,)), ((), ())),
                            preferred_element_type=F32)
        p = jnp.exp(s - jnp.max(s, axis=1, keepdims=True))
        o = jnp.dot(p.astype(BF16), v_ref[0, :, sl], preferred_element_type=F32)
        heads.append((o / jnp.sum(p, axis=1, keepdims=True)).astype(BF16))
    o = jnp.concatenate(heads, axis=1)
    y = jnp.dot(o, wo_ref[...], preferred_element_type=F32)
    o_ref[0] = _ln_rows(DN_ALPHA * h + y, g_ref[...], b_ref[...])


def _cross_attention(h, kvmem, lw):
    B, S, D = h.shape
    M = kvmem.shape[1]
    tm = _row_tile(S, 512)
    const = lambda b, i: (0, 0)
    return pl.pallas_call(
        _xattn_kernel,
        grid=(B, S // tm),
        in_specs=[pl.BlockSpec((1, tm, D), lambda b, i: (b, i, 0)),
                  pl.BlockSpec((1, M, D), lambda b, i: (b, 0, 0)),
                  pl.BlockSpec((1, M, D), lambda b, i: (b, 0, 1)),
                  pl.BlockSpec((D, D), const),
                  pl.BlockSpec((D, D), const),
                  pl.BlockSpec((1, D), const),
                  pl.BlockSpec((1, D), const)],
        out_specs=pl.BlockSpec((1, tm, D), lambda b, i: (b, i, 0)),
        out_shape=jax.ShapeDtypeStruct((B, S, D), F32),
        compiler_params=_cparams("parallel", "parallel"),
    )(h, kvmem, kvmem, lw["w_cq"], lw["w_co"], lw["ln2_g"], lw["ln2_b"])


MOE_TILE = 1024
MOE_ROWS = 256
MOE_ALIGN = 16
MOE_EXPERTS_PER_STEP = 2


def _router_kernel(x_ref, wr_ref, bias_ref, gate_ref, pos_ref, dd_ref, cnt_ref, start_ref, sel_sc):
    T = x_ref.shape[0]
    logits = lax.dot_general(wr_ref[...], x_ref[...], (((1,), (1,)), ((), ())),
                             precision=lax.Precision.HIGHEST, preferred_element_type=F32)
    scores = jax.nn.sigmoid(logits)
    sel = scores + bias_ref[...]
    sc = [scores[e:e + 1, :] for e in range(N_EXPERTS)]
    se = [sel[e:e + 1, :] for e in range(N_EXPERTS)]

    gs = []
    for g in range(N_GROUPS):
        m = [se[g * EXPERTS_PER_GROUP + k] for k in range(EXPERTS_PER_GROUP)]
        best = None
        for a in range(EXPERTS_PER_GROUP):
            for b in range(a + 1, EXPERTS_PER_GROUP):
                pair = m[a] + m[b]
                best = pair if best is None else jnp.maximum(best, pair)
        gs.append(best)
    in_group = []
    for g in range(N_GROUPS):
        ok = None
        for o in range(N_GROUPS):
            if o == g:
                continue
            c = (gs[g] > gs[o]) if o < g else (gs[g] >= gs[o])
            ok = c if ok is None else (ok & c)
        in_group.append(ok)
    chosen = []
    for e in range(N_EXPERTS):
        g = e // EXPERTS_PER_GROUP
        beaten = jnp.zeros((1, T), jnp.int32)
        for o in range(g * EXPERTS_PER_GROUP, (g + 1) * EXPERTS_PER_GROUP):
            if o == e:
                continue
            c = (se[o] > se[e]) if o > e else (se[o] >= se[e])
            beaten = beaten + c.astype(jnp.int32)
        chosen.append(in_group[g] & (beaten < 2))
    denom = jnp.zeros((1, T), F32)
    for e in range(N_EXPERTS):
        denom = denom + jnp.where(chosen[e], sc[e], 0.0)
    for e in range(N_EXPERTS):
        gate_ref[e:e + 1, :] = jnp.where(chosen[e], sc[e] / denom, 0.0)
        sel_sc[e:e + 1, :] = chosen[e].astype(F32)

    selm = sel_sc[...]
    earlier = (lax.broadcasted_iota(jnp.int32, (T, T), 0) < lax.broadcasted_iota(jnp.int32, (T, T), 1))
    rank = jnp.dot(selm.astype(BF16), earlier.astype(BF16), preferred_element_type=F32)
    cnt = jnp.sum(selm, axis=1, keepdims=True)
    cnt_ref[0] = jnp.broadcast_to(cnt, (N_EXPERTS, LANE)).astype(jnp.int32)

    seg = jnp.floor((cnt + (MOE_ALIGN - 1)) * (1.0 / MOE_ALIGN)) * MOE_ALIGN
    start = jnp.zeros((1, 1), F32)
    d_lo = jnp.full((1, T), float(4 * T), F32)
    d_hi = jnp.full((1, T), -1.0, F32)
    for e in range(N_EXPERTS):
        start_ref[0, e:e + 1, :] = jnp.broadcast_to(start, (1, LANE)).astype(jnp.int32)
        dest = jnp.where(chosen[e], rank[e:e + 1, :] + start, -1.0)
        pos_ref[e:e + 1, :] = dest
        d_lo = jnp.where(chosen[e], jnp.minimum(d_lo, dest), d_lo)
        d_hi = jnp.maximum(d_hi, dest)
        start = start + seg[e:e + 1, :]
    dd_ref[...] = jnp.zeros(dd_ref.shape, F32)
    dd_ref[0:1, :] = d_lo
    dd_ref[1:2, :] = d_hi


def _router(x, w_rt, bias):
    n, d = x.shape
    T = _row_tile(n, MOE_TILE)
    nt = n // T
    gate, pos, dd, cnt, start = pl.pallas_call(
        _router_kernel,
        grid=(nt,),
        in_specs=[pl.BlockSpec((T, d), lambda i: (i, 0)),
                  pl.BlockSpec((N_EXPERTS, d), lambda i: (0, 0)),
                  pl.BlockSpec((N_EXPERTS, 1), lambda i: (0, 0))],
        out_specs=[pl.BlockSpec((N_EXPERTS, T), lambda i: (0, i)),
                   pl.BlockSpec((N_EXPERTS, T), lambda i: (0, i)),
                   pl.BlockSpec((8, T), lambda i: (0, i)),
                   pl.BlockSpec((1, N_EXPERTS, LANE), lambda i: (i, 0, 0)),
                   pl.BlockSpec((1, N_EXPERTS, LANE), lambda i: (i, 0, 0))],
        out_shape=[jax.ShapeDtypeStruct((N_EXPERTS, n), F32),
                   jax.ShapeDtypeStruct((N_EXPERTS, n), F32),
                   jax.ShapeDtypeStruct((8, n), F32),
                   jax.ShapeDtypeStruct((nt, N_EXPERTS, LANE), jnp.int32),
                   jax.ShapeDtypeStruct((nt, N_EXPERTS, LANE), jnp.int32)],
        scratch_shapes=[pltpu.VMEM((N_EXPERTS, T), F32)],
        compiler_params=_cparams("parallel"),
    )(x, w_rt, bias)
    return gate, pos, dd, cnt[:, :, 0].reshape(-1), start[:, :, 0].reshape(-1)


def _sorted_rows(T):
    return 2 * T + N_EXPERTS * MOE_ALIGN + MOE_ROWS


def _moe_kernel(cnt_ref, start_ref, h_ref, gate_ref, pos_ref, dd_ref, wg_ref, wu_ref, wd_ref, g_ref, b_ref, o_ref,
                p_sc, xs_sc, ys_sc):
    i = pl.program_id(0)
    step = pl.program_id(1)
    T = h_ref.shape[0]
    R = p_sc.shape[0]

    @pl.when(step == 0)
    def _():
        d_lo = dd_ref[0:1, :].astype(jnp.int32)
        d_hi = dd_ref[1:2, :].astype(jnp.int32)
        xb = h_ref[...].astype(BF16)
        for r0 in range(0, R, MOE_ROWS):
            row = lax.broadcasted_iota(jnp.int32, (MOE_ROWS, T), 0) + r0
            onehot = ((row == d_lo) | (row == d_hi)).astype(BF16)
            p_sc[r0:r0 + MOE_ROWS, :] = onehot
            xs_sc[r0:r0 + MOE_ROWS, :] = jnp.dot(onehot, xb, preferred_element_type=F32).astype(BF16)
        ys_sc[...] = jnp.zeros(ys_sc.shape, BF16)

    slot = lax.broadcasted_iota(jnp.int32, (MOE_ROWS, T), 0)
    for k in range(MOE_EXPERTS_PER_STEP):
        e = step * MOE_EXPERTS_PER_STEP + k
        n_rows = cnt_ref[i * N_EXPERTS + e]
        seg0 = start_ref[i * N_EXPERTS + e]
        n_blocks = (n_rows + MOE_ROWS - 1) // MOE_ROWS
        pos_row = pos_ref[pl.ds(e, 1), :].astype(jnp.int32)
        gate_row = gate_ref[pl.ds(e, 1), :]

        def block(jb, carry, k=k, seg0=seg0, pos_row=pos_row, gate_row=gate_row):
            base = pl.multiple_of(seg0 + jb * MOE_ROWS, MOE_ALIGN)
            xg = xs_sc[pl.ds(base, MOE_ROWS), :]
            hg = jnp.dot(xg, wg_ref[k], preferred_element_type=F32)
            hu = jnp.dot(xg, wu_ref[k], preferred_element_type=F32)
            act = (hg * jax.nn.sigmoid(hg) * hu).astype(BF16)
            y = jnp.dot(act, wd_ref[k], preferred_element_type=F32)
            hit = (pos_row - base) == slot
            g_rows = jnp.sum(jnp.where(hit, gate_row, 0.0), axis=1, keepdims=True)
            ys_sc[pl.ds(base, MOE_ROWS), :] = (y * g_rows).astype(BF16)
            return carry

        lax.fori_loop(0, n_blocks, block, 0)

    @pl.when(step == N_EXPERTS // MOE_EXPERTS_PER_STEP - 1)
    def _():
        moe = lax.dot_general(p_sc[...], ys_sc[...], (((0,), (0,)), ((), ())), preferred_element_type=F32)
        o_ref[...] = _ln_rows(DN_ALPHA * h_ref[...] + moe, g_ref[...], b_ref[...])


def _moe(h, gate, pos, dd, cnt, start, lw):
    n, d = h.shape
    T = _row_tile(n, MOE_TILE)
    R = _sorted_rows(T)
    f = D_FF_EXPERT
    ne = MOE_EXPERTS_PER_STEP
    grid_spec = pltpu.PrefetchScalarGridSpec(
        num_scalar_prefetch=2,
        grid=(n // T, N_EXPERTS // ne),
        in_specs=[pl.BlockSpec((T, d), lambda i, e, c, s: (i, 0)),
                  pl.BlockSpec((N_EXPERTS, T), lambda i, e, c, s: (0, i)),
                  pl.BlockSpec((N_EXPERTS, T), lambda i, e, c, s: (0, i)),
                  pl.BlockSpec((8, T), lambda i, e, c, s: (0, i)),
                  pl.BlockSpec((ne, d, f), lambda i, e, c, s: (e, 0, 0)),
                  pl.BlockSpec((ne, d, f), lambda i, e, c, s: (e, 0, 0)),
                  pl.BlockSpec((ne, f, d), lambda i, e, c, s: (e, 0, 0)),
                  pl.BlockSpec((1, d), lambda i, e, c, s: (0, 0)),
                  pl.BlockSpec((1, d), lambda i, e, c, s: (0, 0))],
        out_specs=pl.BlockSpec((T, d), lambda i, e, c, s: (i, 0)),
        scratch_shapes=[pltpu.VMEM((R, T), BF16), pltpu.VMEM((R, d), BF16), pltpu.VMEM((R, d), BF16)],
    )
    return pl.pallas_call(
        _moe_kernel,
        grid_spec=grid_spec,
        out_shape=jax.ShapeDtypeStruct((n, d), F32),
        compiler_params=_cparams("parallel", "arbitrary"),
    )(cnt, start, h, gate, pos, dd, lw["w_gate"], lw["w_up"], lw["w_down"], lw["ln3_g"], lw["ln3_b"])


def _prep_layer(l, w):
    ext_idx, ext_sgn = _w_ext_index()
    uq_idx, uq_sgn = _w_uq_index()
    mla_w = MLA_HEADS * MLA_V
    gq = w["gqa_q_norm"][l]
    gk = w["gqa_k_norm"][l]
    row = lambda v: v.reshape(1, -1).astype(F32)
    return {
        "w_ext": _take_cols(w["w_in"][l], ext_idx, ext_sgn).astype(BF16),
        "w_uq": _take_cols(w["mla_w_uq"][l], uq_idx, uq_sgn).astype(BF16),
        "w_ukv": _take_cols(w["mla_w_ukv"][l], _w_ukv_index()).astype(BF16),
        "qn": row(w["mla_q_norm"][l]),
        "kvn": row(w["mla_kv_norm"][l]),
        "gq": row(jnp.tile(gq, 2)),
        "gqp": row(jnp.tile(_perm_gain(gq, GQA_HD // 4), 2)),
        "gk": row(jnp.tile(gk, 2)),
        "gkp": row(jnp.tile(_perm_gain(gk, GQA_HD // 4), 2)),
        "conv_w": w["conv_w"][l].astype(F32),
        "conv_b": row(w["conv_b"][l]),
        "conv_ln_g": row(w["conv_ln_g"][l]),
        "conv_ln_b": row(w["conv_ln_b"][l]),
        "wo_a": w["w_o"][l][:mla_w].astype(BF16),
        "wo_b": w["w_o"][l][mla_w:mla_w + CONV_CH].astype(BF16),
        "wo_c": w["w_o"][l][mla_w + CONV_CH:][_w_oc_index()].astype(BF16),
        "ln1_g": row(w["ln1_g"][l]), "ln1_b": row(w["ln1_b"][l]),
        "w_cq": w["w_cq"][l].astype(BF16),
        "w_ckv": w["w_ckv"][l].astype(BF16),
        "w_co": w["w_co"][l].astype(BF16),
        "ln2_g": row(w["ln2_g"][l]), "ln2_b": row(w["ln2_b"][l]),
        "w_gate": w["w_gate"][l].astype(BF16),
        "w_up": w["w_up"][l].astype(BF16),
        "w_down": w["w_down"][l].astype(BF16),
        "ln3_g": row(w["ln3_g"][l]), "ln3_b": row(w["ln3_b"][l]),
    }


def _trunk(x, mem, w, layers):
    B, S, D = x.shape
    n = B * S
    tab = _rope_tables(S)
    w_rt = w["w_router"].T.astype(F32)
    r_bias = w["router_bias"].reshape(N_EXPERTS, 1).astype(F32)
    h = _layer_norm_rows(x.reshape(n, D), w["ln_in_g"], w["ln_in_b"])
    mem2d = mem.reshape(B * mem.shape[1], D)
    for lw in layers:
        qa, kva, glu, qg, kvg = _mixer_pre(h.reshape(B, S, D), tab, lw)
        oa = _mla_attention(qa, kva)
        ob = _conv_module(glu, lw)
        oc = _gqa_attention(qg, kvg)
        h = _out_proj(h, oa.reshape(n, -1), ob.reshape(n, -1), oc.reshape(n, -1), lw)
        kvmem = _mem_kv(mem2d, lw["w_ckv"]).reshape(B, mem.shape[1], 2 * D)
        h = _cross_attention(h.reshape(B, S, D), kvmem, lw).reshape(n, D)
        gate, pos, dd, cnt, start = _router(h, w_rt, r_bias)
        h = _moe(h, gate, pos, dd, cnt, start, lw)
    return h.reshape(B, S, D)


def kernel(x_prompt, x_sample, mem_prompt, mem_sample, ln_in_g, ln_in_b, w_in, mla_q_norm, mla_w_uq,
           mla_kv_norm, mla_w_ukv, conv_w, conv_b, conv_ln_g, conv_ln_b, gqa_q_norm, gqa_k_norm, w_o,
           ln1_g, ln1_b, w_cq, w_ckv, w_co, ln2_g, ln2_b, w_router, router_bias, w_gate, w_up, w_down,
           ln3_g, ln3_b):
    w = dict(ln_in_g=ln_in_g, ln_in_b=ln_in_b, w_in=w_in, mla_q_norm=mla_q_norm, mla_w_uq=mla_w_uq,
             mla_kv_norm=mla_kv_norm, mla_w_ukv=mla_w_ukv, conv_w=conv_w, conv_b=conv_b,
             conv_ln_g=conv_ln_g, conv_ln_b=conv_ln_b, gqa_q_norm=gqa_q_norm, gqa_k_norm=gqa_k_norm,
             w_o=w_o, ln1_g=ln1_g, ln1_b=ln1_b, w_cq=w_cq, w_ckv=w_ckv, w_co=w_co, ln2_g=ln2_g,
             ln2_b=ln2_b, w_router=w_router, router_bias=router_bias, w_gate=w_gate, w_up=w_up,
             w_down=w_down, ln3_g=ln3_g, ln3_b=ln3_b)
    layers = [_prep_layer(l, w) for l in range(w_in.shape[0])]
    y_prompt = _trunk(x_prompt, mem_prompt, w, layers)
    y_sample = _trunk(x_sample, mem_sample, w, layers)
    return (y_prompt, y_sample)
```

```python
import functools

import numpy as np
import jax
import jax.numpy as jnp
from jax import lax
from jax.experimental import pallas as pl
from jax.experimental.pallas import tpu as pltpu

F32 = jnp.float32
BF16 = jnp.bfloat16

D_MODEL = 1024
DEPTH = 4
GRID_W = 64
ROPE_THETA = 10000.0
LN_EPS = 1e-5
RMS_EPS = 1e-6
MLA_HEADS = 6
MLA_Q_LORA = 256
MLA_KV_LORA = 128
MLA_NOPE = 64
MLA_ROPE = 32
MLA_V = 64
CONV_CH = 256
CONV_K = 31
GQA_HEADS = 6
GQA_KV_HEADS = 2
GQA_HD = 64
SPLITS = (MLA_Q_LORA, MLA_KV_LORA, MLA_ROPE, 2 * CONV_CH,
          GQA_HEADS * GQA_HD, GQA_KV_HEADS * GQA_HD, GQA_KV_HEADS * GQA_HD)
XA_HEADS = 4
XA_HD = D_MODEL // XA_HEADS
N_EXPERTS = 16
N_GROUPS = 4
EXPERTS_PER_GROUP = N_EXPERTS // N_GROUPS
D_FF_EXPERT = 512
DN_ALPHA = (2 * DEPTH) ** 0.25
LOG2E = 1.4426950408889634
MLA_SCALE = (MLA_NOPE + MLA_ROPE) ** -0.5 * LOG2E
GQA_SCALE = GQA_HD ** -0.5 * LOG2E
XA_SCALE = XA_HD ** -0.5

LANE = 128
HALF = LANE // 2
SUBLANE = 8
VMEM_LIMIT = 56 * 1024 * 1024

_OFF_CQ = 0
_OFF_CKV = 256
_OFF_KR = 384
_OFF_KRR = 512
_OFF_GA = 640
_OFF_GG = 896
_OFF_QC = 1152
_OFF_QCR = 1536
_OFF_KC = 1920
_OFF_KCR = 2048
_OFF_VC = 2176
_W_EXT = 2432
_GQA_BLOCK_HEADS = (0, 3, 1, 4, 2, 5)
_SUM_LANE = (HALF, 0)


def _cparams(*sem):
    return pltpu.CompilerParams(dimension_semantics=sem, vmem_limit_bytes=VMEM_LIMIT)


def _row_tile(n, want):
    t = min(n, want)
    while n % t:
        t //= 2
    return t


def _ln_rows(x, g, b):
    mu = jnp.mean(x, axis=-1, keepdims=True)
    xc = x - mu
    var = jnp.mean(xc * xc, axis=-1, keepdims=True)
    return xc * lax.rsqrt(var + LN_EPS) * g + b


def _take_cols(w, idx, sign=None):
    w_aug = jnp.concatenate([w, jnp.zeros((w.shape[0], 1), w.dtype)], axis=1)
    idx = np.where(idx < 0, w.shape[1], idx)
    out = w_aug[:, idx]
    if sign is not None:
        out = out * jnp.asarray(sign, w.dtype)[None, :]
    return out


def _rot_src(d, half):
    r = d % (2 * half)
    base = d - r
    if r < half:
        return base + r + half, -1.0
    return base + r - half, 1.0


def _w_ext_index():
    cut = np.concatenate([[0], np.cumsum(SPLITS)])
    c_cq, c_ckv, c_kr, c_glu, c_qc, c_kc, c_vc = [int(c) for c in cut[:-1]]
    idx = -np.ones((_W_EXT,), np.int64)
    sgn = np.ones((_W_EXT,), np.float32)
    idx[_OFF_CQ:_OFF_CQ + 256] = c_cq + np.arange(256)
    idx[_OFF_CKV:_OFF_CKV + 128] = c_ckv + np.arange(128)
    for d in range(MLA_ROPE):
        idx[_OFF_KR + MLA_NOPE + d] = c_kr + d
        s, sg = _rot_src(d, MLA_ROPE // 2)
        idx[_OFF_KRR + MLA_NOPE + d] = c_kr + s
        sgn[_OFF_KRR + MLA_NOPE + d] = sg
    idx[_OFF_GA:_OFF_GA + 256] = c_glu + np.arange(256)
    idx[_OFF_GG:_OFF_GG + 256] = c_glu + 256 + np.arange(256)
    for blk, hq in enumerate(_GQA_BLOCK_HEADS):
        col0 = (blk // 2) * LANE + (blk % 2) * HALF
        for d in range(GQA_HD):
            idx[_OFF_QC + col0 + d] = c_qc + hq * GQA_HD + d
            s, sg = _rot_src(d, GQA_HD // 4)
            idx[_OFF_QCR + col0 + d] = c_qc + hq * GQA_HD + s
            sgn[_OFF_QCR + col0 + d] = sg
    for c in range(GQA_KV_HEADS * GQA_HD):
        hk, d = divmod(c, GQA_HD)
        idx[_OFF_KC + c] = c_kc + c
        s, sg = _rot_src(d, GQA_HD // 4)
        idx[_OFF_KCR + c] = c_kc + hk * GQA_HD + s
        sgn[_OFF_KCR + c] = sg
    idx[_OFF_VC:_OFF_VC + HALF] = c_vc + np.arange(HALF)
    idx[_OFF_VC + LANE + HALF:_OFF_VC + 2 * LANE] = c_vc + HALF + np.arange(HALF)
    return idx, sgn


def _w_uq_index():
    per = MLA_NOPE + MLA_ROPE
    idx = -np.ones((2 * MLA_HEADS * LANE,), np.int64)
    sgn = np.ones(idx.shape, np.float32)
    for h in range(MLA_HEADS):
        idx[h * LANE:h * LANE + per] = h * per + np.arange(per)
        for d in range(MLA_ROPE):
            s, sg = _rot_src(d, MLA_ROPE // 2)
            idx[(MLA_HEADS + h) * LANE + MLA_NOPE + d] = h * per + MLA_NOPE + s
            sgn[(MLA_HEADS + h) * LANE + MLA_NOPE + d] = sg
    return idx, sgn


def _w_ukv_index():
    per = MLA_NOPE + MLA_V
    idx = -np.ones((MLA_HEADS * 2 * LANE,), np.int64)
    for h in range(MLA_HEADS):
        v0 = h * 2 * LANE + (0 if h % 2 == 0 else HALF)
        idx[v0:v0 + MLA_V] = h * per + MLA_NOPE + np.arange(MLA_V)
        idx[h * 2 * LANE + LANE:h * 2 * LANE + LANE + MLA_NOPE] = h * per + np.arange(MLA_NOPE)
    return idx


def _w_oc_index():
    idx = np.zeros((GQA_HEADS * GQA_HD,), np.int64)
    for blk, hq in enumerate(_GQA_BLOCK_HEADS):
        c0 = (blk // 2) * LANE + (blk % 2) * HALF
        idx[c0:c0 + GQA_HD] = hq * GQA_HD + np.arange(GQA_HD)
    return idx


def _perm_gain(g, half):
    src = np.array([_rot_src(d, half)[0] for d in range(g.shape[-1])])
    return g[src]


def _rope_tables(S):
    def cs(pos, dim):
        inv = 1.0 / (ROPE_THETA ** (jnp.arange(0, dim, 2, dtype=F32) / dim))
        ang = pos[:, None] * inv[None, :]
        return jnp.cos(ang), jnp.sin(ang)

    c1, s1 = cs(jnp.arange(S, dtype=F32), MLA_ROPE)
    rows = S // GRID_W
    row = jnp.repeat(jnp.arange(rows, dtype=F32), GRID_W)
    col = jnp.tile(jnp.arange(GRID_W, dtype=F32), rows)
    cr, sr = cs(row, GQA_HD // 2)
    cc, sc = cs(col, GQA_HD // 2)
    zpad = jnp.zeros((S, LANE - MLA_NOPE - MLA_ROPE), F32)
    cm = jnp.concatenate([jnp.ones((S, MLA_NOPE), F32), c1, c1, zpad], axis=1)
    sm = jnp.concatenate([jnp.zeros((S, MLA_NOPE), F32), s1, s1, zpad], axis=1)
    cg = jnp.concatenate([cr, cr, cc, cc] * 2, axis=1)
    sg = jnp.concatenate([sr, sr, sc, sc] * 2, axis=1)
    return jnp.concatenate([cm, sm, cg, sg], axis=1)


def _ln_kernel(x_ref, g_ref, b_ref, o_ref):
    o_ref[...] = _ln_rows(x_ref[...], g_ref[...], b_ref[...])


def _layer_norm_rows(x, g, b):
    n, d = x.shape
    tm = _row_tile(n, 1024)
    return pl.pallas_call(
        _ln_kernel,
        grid=(n // tm,),
        in_specs=[pl.BlockSpec((tm, d), lambda i: (i, 0)),
                  pl.BlockSpec((1, d), lambda i: (0, 0)),
                  pl.BlockSpec((1, d), lambda i: (0, 0))],
        out_specs=pl.BlockSpec((tm, d), lambda i: (i, 0)),
        out_shape=jax.ShapeDtypeStruct((n, d), F32),
        compiler_params=_cparams("parallel"),
    )(x, g.reshape(1, d), b.reshape(1, d))


def _rms_scale(ss, width):
    return lax.rsqrt(ss * (1.0 / width) + RMS_EPS)


def _pre_kernel(h_ref, tab_ref, wext_ref, wuq_ref, wukv_ref, qn_ref, kvn_ref,
                gq_ref, gqp_ref, gk_ref, gkp_ref,
                qa_ref, kva_ref, glu_ref, qg_ref, kvg_ref):
    hb = h_ref[0].astype(BF16)
    z = jnp.dot(hb, wext_ref[...], preferred_element_type=F32)
    cm = tab_ref[:, 0:LANE]
    sm = tab_ref[:, LANE:2 * LANE]
    cg = tab_ref[:, 2 * LANE:3 * LANE]
    sg = tab_ref[:, 3 * LANE:4 * LANE]

    cq = z[:, _OFF_CQ:_OFF_CQ + MLA_Q_LORA]
    cqn = cq * _rms_scale(jnp.sum(cq * cq, axis=-1, keepdims=True), MLA_Q_LORA) * qn_ref[...]
    a = jnp.dot(cqn.astype(BF16), wuq_ref[...], preferred_element_type=F32)
    rot0 = MLA_HEADS * LANE
    for h in range(MLA_HEADS):
        roped = a[:, h * LANE:(h + 1) * LANE] * cm + a[:, rot0 + h * LANE:rot0 + (h + 1) * LANE] * sm
        qa_ref[0, :, h * LANE:(h + 1) * LANE] = (roped * MLA_SCALE).astype(BF16)

    ckv = z[:, _OFF_CKV:_OFF_CKV + MLA_KV_LORA]
    ckvn = ckv * _rms_scale(jnp.sum(ckv * ckv, axis=-1, keepdims=True), MLA_KV_LORA) * kvn_ref[...]
    kvm = jnp.dot(ckvn.astype(BF16), wukv_ref[...], preferred_element_type=F32)
    kr = z[:, _OFF_KR:_OFF_KR + LANE] * cm + z[:, _OFF_KRR:_OFF_KRR + LANE] * sm
    lane = lax.broadcasted_iota(jnp.int32, (1, LANE), 1)
    one_at = [jnp.where(lane == L, 1.0, 0.0) for L in _SUM_LANE]
    for h in range(MLA_HEADS):
        c0 = h * 2 * LANE
        kva_ref[0, :, c0:c0 + LANE] = (kvm[:, c0:c0 + LANE] + one_at[h % 2]).astype(BF16)
        kva_ref[0, :, c0 + LANE:c0 + 2 * LANE] = (kvm[:, c0 + LANE:c0 + 2 * LANE] + kr).astype(BF16)

    glu_ref[0] = z[:, _OFF_GA:_OFF_GA + CONV_CH] * jax.nn.sigmoid(z[:, _OFF_GG:_OFF_GG + CONV_CH])

    low = lax.broadcasted_iota(jnp.int32, (h_ref.shape[1], LANE), 1) < HALF

    def norm_rope(x, x_rot, g, g_rot):
        sq = x * x
        r0 = _rms_scale(jnp.sum(jnp.where(low, sq, 0.0), axis=-1, keepdims=True), GQA_HD)
        r1 = _rms_scale(jnp.sum(jnp.where(low, 0.0, sq), axis=-1, keepdims=True), GQA_HD)
        return (x * g * cg + x_rot * g_rot * sg) * jnp.where(low, r0, r1)

    gq = gq_ref[...]
    gqp = gqp_ref[...]
    for p in range(GQA_HEADS // 2):
        qn = norm_rope(z[:, _OFF_QC + p * LANE:_OFF_QC + (p + 1) * LANE],
                       z[:, _OFF_QCR + p * LANE:_OFF_QCR + (p + 1) * LANE], gq, gqp) * GQA_SCALE
        qg_ref[0, :, 2 * p * LANE:(2 * p + 1) * LANE] = jnp.where(low, qn, 0.0).astype(BF16)
        qg_ref[0, :, (2 * p + 1) * LANE:(2 * p + 2) * LANE] = jnp.where(low, 0.0, qn).astype(BF16)

    kvg_ref[0, :, 0:LANE] = norm_rope(z[:, _OFF_KC:_OFF_KC + LANE], z[:, _OFF_KCR:_OFF_KCR + LANE],
                                      gk_ref[...], gkp_ref[...]).astype(BF16)
    kvg_ref[0, :, LANE:2 * LANE] = (z[:, _OFF_VC:_OFF_VC + LANE] + one_at[0]).astype(BF16)
    kvg_ref[0, :, 2 * LANE:3 * LANE] = (z[:, _OFF_VC + LANE:_OFF_VC + 2 * LANE] + one_at[1]).astype(BF16)


def _mixer_pre(h, tab, lw):
    B, S, D = h.shape
    tm = _row_tile(S, 512)
    const = lambda b, i: (0, 0)
    row3 = lambda b, i: (b, i, 0)
    outs = [(MLA_HEADS * LANE, BF16), (MLA_HEADS * 2 * LANE, BF16), (CONV_CH, F32),
            (GQA_HEADS * LANE, BF16), (3 * LANE, BF16)]
    return pl.pallas_call(
        _pre_kernel,
        grid=(B, S // tm),
        in_specs=[pl.BlockSpec((1, tm, D), row3),
                  pl.BlockSpec((tm, 4 * LANE), lambda b, i: (i, 0)),
                  pl.BlockSpec(lw["w_ext"].shape, const),
                  pl.BlockSpec(lw["w_uq"].shape, const),
                  pl.BlockSpec(lw["w_ukv"].shape, const),
                  pl.BlockSpec((1, MLA_Q_LORA), const),
                  pl.BlockSpec((1, MLA_KV_LORA), const),
                  pl.BlockSpec((1, LANE), const),
                  pl.BlockSpec((1, LANE), const),
                  pl.BlockSpec((1, LANE), const),
                  pl.BlockSpec((1, LANE), const)],
        out_specs=[pl.BlockSpec((1, tm, w), row3) for w, _ in outs],
        out_shape=[jax.ShapeDtypeStruct((B, S, w), dt) for w, dt in outs],
        compiler_params=_cparams("parallel", "parallel"),
    )(h, tab, lw["w_ext"], lw["w_uq"], lw["w_ukv"], lw["qn"], lw["kvn"],
      lw["gq"], lw["gqp"], lw["gk"], lw["gkp"])


def _attn_kernel(q_ref, kv_ref, o_ref, m_sc, acc_sc, s_sc, *, k_off, v_off, tkc):
    j = pl.program_id(3)
    tk = kv_ref.shape[1]
    units = [(i, c) for c in range(tk // tkc) for i in range(2)]
    n_col = tkc // LANE

    @pl.when(j == 0)
    def _():
        m_sc[...] = jnp.full(m_sc.shape, -jnp.inf, F32)
        acc_sc[...] = jnp.zeros(acc_sc.shape, F32)

    def scores(u):
        i, c = units[u]
        q = q_ref[0, :, i * LANE:(i + 1) * LANE]
        k = kv_ref[0, c * tkc:(c + 1) * tkc, k_off[i]:k_off[i] + LANE]
        return lax.dot_general(q, k, (((1,), (1,)), ((), ())), preferred_element_type=F32)

    s_sc[0] = scores(0)
    for u, (i, c) in enumerate(units):
        if u + 1 < len(units):
            s_sc[(u + 1) % 2] = scores(u + 1)
        cols = [s_sc[u % 2, :, cb * LANE:(cb + 1) * LANE] for cb in range(n_col)]
        mx = cols[0]
        for x in cols[1:]:
            mx = jnp.maximum(mx, x)
        m_prev = m_sc[i]
        m_new = jnp.maximum(m_prev, jnp.max(mx, axis=1, keepdims=True))
        alpha = jnp.exp2(m_prev - m_new)
        p = jnp.concatenate([jnp.exp2(x - m_new).astype(BF16) for x in cols], axis=1)
        v = kv_ref[0, c * tkc:(c + 1) * tkc, v_off[i]:v_off[i] + LANE]
        acc_sc[i] = alpha * acc_sc[i] + jnp.dot(p, v, preferred_element_type=F32)
        m_sc[i] = m_new

    @pl.when(j == pl.num_programs(3) - 1)
    def _():
        o0 = acc_sc[0] / acc_sc[0, :, _SUM_LANE[0]:_SUM_LANE[0] + 1]
        o1 = acc_sc[1] / acc_sc[1, :, _SUM_LANE[1]:_SUM_LANE[1] + 1]
        low = lax.broadcasted_iota(jnp.int32, o0.shape, 1) < HALF
        o_ref[0] = jnp.where(low, o0, o1).astype(o_ref.dtype)


ATTN_TQ = 1024
ATTN_TK = 4096
ATTN_TKC = 512


def _attention(q, kv, *, kvw_block, kv_block_of_pair, k_off, v_off):
    B, S, _ = q.shape
    n_pairs = 3
    tq = _row_tile(S, ATTN_TQ)
    tk = _row_tile(S, ATTN_TK)
    tkc = _row_tile(tk, ATTN_TKC)
    kern = functools.partial(_attn_kernel, k_off=k_off, v_off=v_off, tkc=tkc)
    return pl.pallas_call(
        kern,
        grid=(B, n_pairs, S // tq, S // tk),
        in_specs=[pl.BlockSpec((1, tq, 2 * LANE), lambda b, p, i, j: (b, i, p)),
                  pl.BlockSpec((1, tk, kvw_block), lambda b, p, i, j: (b, j, kv_block_of_pair(p)))],
        out_specs=pl.BlockSpec((1, tq, LANE), lambda b, p, i, j: (b, i, p)),
        out_shape=jax.ShapeDtypeStruct((B, S, n_pairs * LANE), BF16),
        scratch_shapes=[pltpu.VMEM((2, tq, LANE), F32),
                        pltpu.VMEM((2, tq, LANE), F32),
                        pltpu.VMEM((2, tq, tkc), F32)],
        compiler_params=_cparams("parallel", "parallel", "parallel", "arbitrary"),
    )(q, kv)


def _mla_attention(qa, kva):
    return _attention(qa, kva, kvw_block=4 * LANE, kv_block_of_pair=lambda p: p,
                      k_off=(LANE, 3 * LANE), v_off=(0, 2 * LANE))


def _gqa_attention(qg, kvg):
    return _attention(qg, kvg, kvw_block=3 * LANE, kv_block_of_pair=lambda p: 0,
                      k_off=(0, 0), v_off=(LANE, 2 * LANE))


_HALO = 16


def _conv_kernel(prev_ref, cur_ref, next_ref, w_ref, b_ref, g_ref, beta_ref, o_ref, xe_sc, sh_sc):
    i = pl.program_id(1)
    tc = cur_ref.shape[1]
    first = i == 0
    last = i == pl.num_programs(1) - 1
    xe_sc[0:_HALO, :] = jnp.where(first, 0.0, prev_ref[0])
    xe_sc[_HALO:_HALO + tc, :] = cur_ref[0]
    xe_sc[_HALO + tc:_HALO + tc + _HALO, :] = jnp.where(last, 0.0, next_ref[0])
    acc = jnp.zeros((tc, CONV_CH), F32)
    span = sh_sc.shape[0]
    for shift in range(SUBLANE):
        taps = [t for t in range(CONV_K) if (_HALO - CONV_K // 2 + t) % SUBLANE == shift]
        if not taps:
            continue
        sh_sc[...] = xe_sc[shift:shift + span, :]
        for t in taps:
            base = _HALO - CONV_K // 2 + t - shift
            acc = acc + sh_sc[base:base + tc, :] * w_ref[t:t + 1, :]
    u = acc + b_ref[...]
    y = _ln_rows(u, g_ref[...], beta_ref[...])
    o_ref[0] = (y * jax.nn.sigmoid(y)).astype(o_ref.dtype)


def _conv_module(glu, lw):
    B, S, C = glu.shape
    tc = _row_tile(S, 512)
    nh = tc // _HALO
    n_halo_blocks = S // _HALO
    const = lambda b, i: (0, 0)
    return pl.pallas_call(
        _conv_kernel,
        grid=(B, S // tc),
        in_specs=[pl.BlockSpec((1, _HALO, C), lambda b, i: (b, jnp.maximum(i * nh - 1, 0), 0)),
                  pl.BlockSpec((1, tc, C), lambda b, i: (b, i, 0)),
                  pl.BlockSpec((1, _HALO, C), lambda b, i: (b, jnp.minimum((i + 1) * nh, n_halo_blocks - 1), 0)),
                  pl.BlockSpec((CONV_K, C), const),
                  pl.BlockSpec((1, C), const),
                  pl.BlockSpec((1, C), const),
                  pl.BlockSpec((1, C), const)],
        out_specs=pl.BlockSpec((1, tc, C), lambda b, i: (b, i, 0)),
        out_shape=jax.ShapeDtypeStruct((B, S, C), BF16),
        scratch_shapes=[pltpu.VMEM((tc + 2 * _HALO, C), F32),
                        pltpu.VMEM((tc + 2 * _HALO - SUBLANE, C), F32)],
        compiler_params=_cparams("parallel", "parallel"),
    )(glu, glu, glu, lw["conv_w"], lw["conv_b"], lw["conv_ln_g"], lw["conv_ln_b"])


def _oproj_kernel(h_ref, oa_ref, ob_ref, oc_ref, wa_ref, wb_ref, wc_ref, g_ref, b_ref, o_ref):
    mix = jnp.dot(oa_ref[...], wa_ref[...], preferred_element_type=F32)
    mix = mix + jnp.dot(ob_ref[...], wb_ref[...], preferred_element_type=F32)
    mix = mix + jnp.dot(oc_ref[...], wc_ref[...], preferred_element_type=F32)
    o_ref[...] = _ln_rows(DN_ALPHA * h_ref[...] + mix, g_ref[...], b_ref[...])


def _out_proj(h, oa, ob, oc, lw):
    n, d = h.shape
    tm = _row_tile(n, 512)
    const = lambda i: (0, 0)
    row = lambda i: (i, 0)
    return pl.pallas_call(
        _oproj_kernel,
        grid=(n // tm,),
        in_specs=[pl.BlockSpec((tm, d), row),
                  pl.BlockSpec((tm, oa.shape[1]), row),
                  pl.BlockSpec((tm, ob.shape[1]), row),
                  pl.BlockSpec((tm, oc.shape[1]), row),
                  pl.BlockSpec(lw["wo_a"].shape, const),
                  pl.BlockSpec(lw["wo_b"].shape, const),
                  pl.BlockSpec(lw["wo_c"].shape, const),
                  pl.BlockSpec((1, d), const),
                  pl.BlockSpec((1, d), const)],
        out_specs=pl.BlockSpec((tm, d), row),
        out_shape=jax.ShapeDtypeStruct((n, d), F32),
        compiler_params=_cparams("parallel"),
    )(h, oa, ob, oc, lw["wo_a"], lw["wo_b"], lw["wo_c"], lw["ln1_g"], lw["ln1_b"])


def _memkv_kernel(x_ref, w_ref, o_ref):
    o_ref[...] = jnp.dot(x_ref[...].astype(BF16), w_ref[...], preferred_element_type=F32).astype(o_ref.dtype)


def _mem_kv(mem2d, w_ckv):
    n, d = mem2d.shape
    tm = _row_tile(n, 256)
    return pl.pallas_call(
        _memkv_kernel,
        grid=(n // tm,),
        in_specs=[pl.BlockSpec((tm, d), lambda i: (i, 0)),
                  pl.BlockSpec(w_ckv.shape, lambda i: (0, 0))],
        out_specs=pl.BlockSpec((tm, w_ckv.shape[1]), lambda i: (i, 0)),
        out_shape=jax.ShapeDtypeStruct((n, w_ckv.shape[1]), BF16),
        compiler_params=_cparams("parallel"),
    )(mem2d, w_ckv)


def _xattn_kernel(h_ref, k_ref, v_ref, wq_ref, wo_ref, g_ref, b_ref, o_ref):
    h = h_ref[0]
    q = (jnp.dot(h.astype(BF16), wq_ref[...], preferred_element_type=F32) * XA_SCALE).astype(BF16)
    heads = []
    for hd in range(XA_HEADS):
        sl = slice(hd * XA_HD, (hd + 1) * XA_HD)
        s = lax.dot_general(q[:, sl], k_ref[0, :, sl], (((1,), (1,)), ((), ())),
                            preferred_element_type=F32)
        p = jnp.exp(s - jnp.max(s, axis=1, keepdims=True))
        o = jnp.dot(p.astype(BF16), v_ref[0, :, sl], preferred_element_type=F32)
        heads.append((o / jnp.sum(p, axis=1, keepdims=True)).astype(BF16))
    o = jnp.concatenate(heads, axis=1)
    y = jnp.dot(o, wo_ref[...], preferred_element_type=F32)
    o_ref[0] = _ln_rows(DN_ALPHA * h + y, g_ref[...], b_ref[...])


def _cross_attention(h, kvmem, lw):
    B, S, D = h.shape
    M = kvmem.shape[1]
    tm = _row_tile(S, 512)
    const = lambda b, i: (0, 0)
    return pl.pallas_call(
        _xattn_kernel,
        grid=(B, S // tm),
        in_specs=[pl.BlockSpec((1, tm, D), lambda b, i: (b, i, 0)),
                  pl.BlockSpec((1, M, D), lambda b, i: (b, 0, 0)),
                  pl.BlockSpec((1, M, D), lambda b, i: (b, 0, 1)),
                  pl.BlockSpec((D, D), const),
                  pl.BlockSpec((D, D), const),
                  pl.BlockSpec((1, D), const),
                  pl.BlockSpec((1, D), const)],
        out_specs=pl.BlockSpec((1, tm, D), lambda b, i: (b, i, 0)),
        out_shape=jax.ShapeDtypeStruct((B, S, D), F32),
        compiler_params=_cparams("parallel", "parallel"),
    )(h, kvmem, kvmem, lw["w_cq"], lw["w_co"], lw["ln2_g"], lw["ln2_b"])


MOE_TILE = 1024
MOE_SUB = 256
MOE_SUB_SHIFT = 8
MOE_SEG = 64
MOE_ALIGN = 16
MOE_EXPERTS_PER_STEP = 2


def _router_kernel(x_ref, wr_ref, bias_ref, gate_ref, pos_ref, dd_ref, cnt_ref, start_ref, sel_sc):
    T = x_ref.shape[0]
    logits = lax.dot_general(wr_ref[...], x_ref[...], (((1,), (1,)), ((), ())),
                             precision=lax.Precision.HIGHEST, preferred_element_type=F32)
    scores = jax.nn.sigmoid(logits)
    sel = scores + bias_ref[...]
    sc = [scores[e:e + 1, :] for e in range(N_EXPERTS)]
    se = [sel[e:e + 1, :] for e in range(N_EXPERTS)]

    gs = []
    for g in range(N_GROUPS):
        m = [se[g * EXPERTS_PER_GROUP + k] for k in range(EXPERTS_PER_GROUP)]
        best = None
        for a in range(EXPERTS_PER_GROUP):
            for b in range(a + 1, EXPERTS_PER_GROUP):
                pair = m[a] + m[b]
                best = pair if best is None else jnp.maximum(best, pair)
        gs.append(best)
    in_group = []
    for g in range(N_GROUPS):
        ok = None
        for o in range(N_GROUPS):
            if o == g:
                continue
            c = (gs[g] > gs[o]) if o < g else (gs[g] >= gs[o])
            ok = c if ok is None else (ok & c)
        in_group.append(ok)
    chosen = []
    for e in range(N_EXPERTS):
        g = e // EXPERTS_PER_GROUP
        beaten = jnp.zeros((1, T), jnp.int32)
        for o in range(g * EXPERTS_PER_GROUP, (g + 1) * EXPERTS_PER_GROUP):
            if o == e:
                continue
            c = (se[o] > se[e]) if o > e else (se[o] >= se[e])
            beaten = beaten + c.astype(jnp.int32)
        chosen.append(in_group[g] & (beaten < 2))
    denom = jnp.zeros((1, T), F32)
    for e in range(N_EXPERTS):
        denom = denom + jnp.where(chosen[e], sc[e], 0.0)
    for e in range(N_EXPERTS):
        gate_ref[e:e + 1, :] = jnp.where(chosen[e], sc[e] / denom, 0.0)
        sel_sc[e:e + 1, :] = chosen[e].astype(F32)

    selm = sel_sc[...].astype(BF16)
    sub_r = jnp.right_shift(lax.broadcasted_iota(jnp.int32, (T, T), 0), MOE_SUB_SHIFT)
    sub_c = jnp.right_shift(lax.broadcasted_iota(jnp.int32, (T, T), 1), MOE_SUB_SHIFT)
    same = sub_r == sub_c
    earlier = same & (lax.broadcasted_iota(jnp.int32, (T, T), 0) < lax.broadcasted_iota(jnp.int32, (T, T), 1))
    rank = jnp.dot(selm, earlier.astype(BF16), preferred_element_type=F32)
    cnt = jnp.dot(selm, same.astype(BF16), preferred_element_type=F32)
    cnt_ref[0] = cnt.astype(jnp.int32)

    seg = jnp.floor((cnt + (MOE_ALIGN - 1)) * (1.0 / MOE_ALIGN)) * MOE_ALIGN
    start = jnp.zeros((1, T), F32)
    d_lo = jnp.full((1, T), float(4 * T), F32)
    d_hi = jnp.full((1, T), -1.0, F32)
    for e in range(N_EXPERTS):
        start_ref[0, e:e + 1, :] = start.astype(jnp.int32)
        dest = jnp.where(chosen[e], rank[e:e + 1, :] + start, -1.0)
        pos_ref[e:e + 1, :] = dest
        d_lo = jnp.where(chosen[e], jnp.minimum(d_lo, dest), d_lo)
        d_hi = jnp.maximum(d_hi, dest)
        start = start + seg[e:e + 1, :]
    dd_ref[...] = jnp.zeros(dd_ref.shape, F32)
    dd_ref[0:1, :] = d_lo
    dd_ref[1:2, :] = d_hi


def _router(x, w_rt, bias):
    n, d = x.shape
    T = _row_tile(n, MOE_TILE)
    nt = n // T
    gate, pos, dd, cnt, start = pl.pallas_call(
        _router_kernel,
        grid=(nt,),
        in_specs=[pl.BlockSpec((T, d), lambda i: (i, 0)),
                  pl.BlockSpec((N_EXPERTS, d), lambda i: (0, 0)),
                  pl.BlockSpec((N_EXPERTS, 1), lambda i: (0, 0))],
        out_specs=[pl.BlockSpec((N_EXPERTS, T), lambda i: (0, i)),
                   pl.BlockSpec((N_EXPERTS, T), lambda i: (0, i)),
                   pl.BlockSpec((8, T), lambda i: (0, i)),
                   pl.BlockSpec((1, N_EXPERTS, T), lambda i: (i, 0, 0)),
                   pl.BlockSpec((1, N_EXPERTS, T), lambda i: (i, 0, 0))],
        out_shape=[jax.ShapeDtypeStruct((N_EXPERTS, n), F32),
                   jax.ShapeDtypeStruct((N_EXPERTS, n), F32),
                   jax.ShapeDtypeStruct((8, n), F32),
                   jax.ShapeDtypeStruct((nt, N_EXPERTS, T), jnp.int32),
                   jax.ShapeDtypeStruct((nt, N_EXPERTS, T), jnp.int32)],
        scratch_shapes=[pltpu.VMEM((N_EXPERTS, T), F32)],
        compiler_params=_cparams("parallel"),
    )(x, w_rt, bias)
    return gate, pos, dd, cnt[:, :, ::MOE_SUB].reshape(-1), start[:, :, ::MOE_SUB].reshape(-1)


MOE_SORTED_ROWS = 2 * MOE_SUB + N_EXPERTS * MOE_ALIGN + MOE_SEG


def _moe_kernel(cnt_ref, start_ref, h_ref, gate_ref, pos_ref, dd_ref, wg_ref, wu_ref, wd_ref, g_ref, b_ref, o_ref,
                p_sc, xs_sc, ys_sc):
    i = pl.program_id(0)
    step = pl.program_id(1)
    T = h_ref.shape[0]
    n_sub = p_sc.shape[0]
    R = p_sc.shape[1]
    half = R // 2

    @pl.when(step == 0)
    def _():
        for s in range(n_sub):
            tok = slice(s * MOE_SUB, (s + 1) * MOE_SUB)
            d_lo = dd_ref[0:1, tok].astype(jnp.int32)
            d_hi = dd_ref[1:2, tok].astype(jnp.int32)
            xb = h_ref[tok, :].astype(BF16)
            for r0 in (0, half):
                row = lax.broadcasted_iota(jnp.int32, (half, MOE_SUB), 0) + r0
                onehot = ((row == d_lo) | (row == d_hi)).astype(BF16)
                p_sc[s, r0:r0 + half, :] = onehot
                xs_sc[s, r0:r0 + half, :] = jnp.dot(onehot, xb, preferred_element_type=F32).astype(BF16)
        ys_sc[...] = jnp.zeros(ys_sc.shape, BF16)

    slot = lax.broadcasted_iota(jnp.int32, (MOE_SEG, MOE_SUB), 0)
    for k in range(MOE_EXPERTS_PER_STEP):
        e = step * MOE_EXPERTS_PER_STEP + k
        flat = (i * N_EXPERTS + e) * n_sub
        n_rows = [cnt_ref[flat + s] for s in range(n_sub)]
        seg0 = [start_ref[flat + s] for s in range(n_sub)]
        most = n_rows[0]
        for c in n_rows[1:]:
            most = jnp.maximum(most, c)
        n_blocks = (most + MOE_SEG - 1) // MOE_SEG
        pos_row = pos_ref[pl.ds(e, 1), :].astype(jnp.int32)
        gate_row = gate_ref[pl.ds(e, 1), :]

        def block(jb, carry, k=k, n_rows=n_rows, seg0=seg0, pos_row=pos_row, gate_row=gate_row):
            base = [pl.multiple_of(jnp.minimum(seg0[s] + jb * MOE_SEG, R - MOE_SEG), MOE_ALIGN)
                    for s in range(n_sub)]
            dest = [pl.multiple_of(jnp.where(jb * MOE_SEG < n_rows[s], base[s], R), MOE_ALIGN)
                    for s in range(n_sub)]
            xg = jnp.concatenate([xs_sc[s, pl.ds(base[s], MOE_SEG), :] for s in range(n_sub)], axis=0)
            hg = jnp.dot(xg, wg_ref[k], preferred_element_type=F32)
            hu = jnp.dot(xg, wu_ref[k], preferred_element_type=F32)
            act = (hg * jax.nn.sigmoid(hg) * hu).astype(BF16)
            y = jnp.dot(act, wd_ref[k], preferred_element_type=F32)
            for s in range(n_sub):
                tok = slice(s * MOE_SUB, (s + 1) * MOE_SUB)
                hit = (pos_row[:, tok] - base[s]) == slot
                g_rows = jnp.sum(jnp.where(hit, gate_row[:, tok], 0.0), axis=1, keepdims=True)
                ys_sc[s, pl.ds(dest[s], MOE_SEG), :] = (y[s * MOE_SEG:(s + 1) * MOE_SEG, :] * g_rows).astype(BF16)
            return carry

        lax.fori_loop(0, n_blocks, block, 0)

    @pl.when(step == N_EXPERTS // MOE_EXPERTS_PER_STEP - 1)
    def _():
        for s in range(n_sub):
            tok = slice(s * MOE_SUB, (s + 1) * MOE_SUB)
            moe = lax.dot_general(p_sc[s], ys_sc[s, 0:R, :], (((0,), (0,)), ((), ())), preferred_element_type=F32)
            o_ref[tok, :] = _ln_rows(DN_ALPHA * h_ref[tok, :] + moe, g_ref[...], b_ref[...])


def _moe(h, gate, pos, dd, cnt, start, lw):
    n, d = h.shape
    T = _row_tile(n, MOE_TILE)
    n_sub = T // MOE_SUB
    R = MOE_SORTED_ROWS
    f = D_FF_EXPERT
    ne = MOE_EXPERTS_PER_STEP
    grid_spec = pltpu.PrefetchScalarGridSpec(
        num_scalar_prefetch=2,
        grid=(n // T, N_EXPERTS // ne),
        in_specs=[pl.BlockSpec((T, d), lambda i, e, c, s: (i, 0)),
                  pl.BlockSpec((N_EXPERTS, T), lambda i, e, c, s: (0, i)),
                  pl.BlockSpec((N_EXPERTS, T), lambda i, e, c, s: (0, i)),
                  pl.BlockSpec((8, T), lambda i, e, c, s: (0, i)),
                  pl.BlockSpec((ne, d, f), lambda i, e, c, s: (e, 0, 0)),
                  pl.BlockSpec((ne, d, f), lambda i, e, c, s: (e, 0, 0)),
                  pl.BlockSpec((ne, f, d), lambda i, e, c, s: (e, 0, 0)),
                  pl.BlockSpec((1, d), lambda i, e, c, s: (0, 0)),
                  pl.BlockSpec((1, d), lambda i, e, c, s: (0, 0))],
        out_specs=pl.BlockSpec((T, d), lambda i, e, c, s: (i, 0)),
        scratch_shapes=[pltpu.VMEM((n_sub, R, MOE_SUB), BF16), pltpu.VMEM((n_sub, R, d), BF16),
                        pltpu.VMEM((n_sub, R + MOE_SEG, d), BF16)],
    )
    return pl.pallas_call(
        _moe_kernel,
        grid_spec=grid_spec,
        out_shape=jax.ShapeDtypeStruct((n, d), F32),
        compiler_params=_cparams("parallel", "arbitrary"),
    )(cnt, start, h, gate, pos, dd, lw["w_gate"], lw["w_up"], lw["w_down"], lw["ln3_g"], lw["ln3_b"])


def _prep_layer(l, w):
    ext_idx, ext_sgn = _w_ext_index()
    uq_idx, uq_sgn = _w_uq_index()
    mla_w = MLA_HEADS * MLA_V
    gq = w["gqa_q_norm"][l]
    gk = w["gqa_k_norm"][l]
    row = lambda v: v.reshape(1, -1).astype(F32)
    return {
        "w_ext": _take_cols(w["w_in"][l], ext_idx, ext_sgn).astype(BF16),
        "w_uq": _take_cols(w["mla_w_uq"][l], uq_idx, uq_sgn).astype(BF16),
        "w_ukv": _take_cols(w["mla_w_ukv"][l], _w_ukv_index()).astype(BF16),
        "qn": row(w["mla_q_norm"][l]),
        "kvn": row(w["mla_kv_norm"][l]),
        "gq": row(jnp.tile(gq, 2)),
        "gqp": row(jnp.tile(_perm_gain(gq, GQA_HD // 4), 2)),
        "gk": row(jnp.tile(gk, 2)),
        "gkp": row(jnp.tile(_perm_gain(gk, GQA_HD // 4), 2)),
        "conv_w": w["conv_w"][l].astype(F32),
        "conv_b": row(w["conv_b"][l]),
        "conv_ln_g": row(w["conv_ln_g"][l]),
        "conv_ln_b": row(w["conv_ln_b"][l]),
        "wo_a": w["w_o"][l][:mla_w].astype(BF16),
        "wo_b": w["w_o"][l][mla_w:mla_w + CONV_CH].astype(BF16),
        "wo_c": w["w_o"][l][mla_w + CONV_CH:][_w_oc_index()].astype(BF16),
        "ln1_g": row(w["ln1_g"][l]), "ln1_b": row(w["ln1_b"][l]),
        "w_cq": w["w_cq"][l].astype(BF16),
        "w_ckv": w["w_ckv"][l].astype(BF16),
        "w_co": w["w_co"][l].astype(BF16),
        "ln2_g": row(w["ln2_g"][l]), "ln2_b": row(w["ln2_b"][l]),
        "w_gate": w["w_gate"][l].astype(BF16),
        "w_up": w["w_up"][l].astype(BF16),
        "w_down": w["w_down"][l].astype(BF16),
        "ln3_g": row(w["ln3_g"][l]), "ln3_b": row(w["ln3_b"][l]),
    }


def _trunk(x, mem, w, layers):
    B, S, D = x.shape
    n = B * S
    tab = _rope_tables(S)
    w_rt = w["w_router"].T.astype(F32)
    r_bias = w["router_bias"].reshape(N_EXPERTS, 1).astype(F32)
    h = _layer_norm_rows(x.reshape(n, D), w["ln_in_g"], w["ln_in_b"])
    mem2d = mem.reshape(B * mem.shape[1], D)
    for lw in layers:
        qa, kva, glu, qg, kvg = _mixer_pre(h.reshape(B, S, D), tab, lw)
        oa = _mla_attention(qa, kva)
        ob = _conv_module(glu, lw)
        oc = _gqa_attention(qg, kvg)
        h = _out_proj(h, oa.reshape(n, -1), ob.reshape(n, -1), oc.reshape(n, -1), lw)
        kvmem = _mem_kv(mem2d, lw["w_ckv"]).reshape(B, mem.shape[1], 2 * D)
        h = _cross_attention(h.reshape(B, S, D), kvmem, lw).reshape(n, D)
        gate, pos, dd, cnt, start = _router(h, w_rt, r_bias)
        h = _moe(h, gate, pos, dd, cnt, start, lw)
    return h.reshape(B, S, D)


def kernel(x_prompt, x_sample, mem_prompt, mem_sample, ln_in_g, ln_in_b, w_in, mla_q_norm, mla_w_uq,
           mla_kv_norm, mla_w_ukv, conv_w, conv_b, conv_ln_g, conv_ln_b, gqa_q_norm, gqa_k_norm, w_o,
           ln1_g, ln1_b, w_cq, w_ckv, w_co, ln2_g, ln2_b, w_router, router_bias, w_gate, w_up, w_down,
           ln3_g, ln3_b):
    w = dict(ln_in_g=ln_in_g, ln_in_b=ln_in_b, w_in=w_in, mla_q_norm=mla_q_norm, mla_w_uq=mla_w_uq,
             mla_kv_norm=mla_kv_norm, mla_w_ukv=mla_w_ukv, conv_w=conv_w, conv_b=conv_b,
             conv_ln_g=conv_ln_g, conv_ln_b=conv_ln_b, gqa_q_norm=gqa_q_norm, gqa_k_norm=gqa_k_norm,
             w_o=w_o, ln1_g=ln1_g, ln1_b=ln1_b, w_cq=w_cq, w_ckv=w_ckv, w_co=w_co, ln2_g=ln2_g,
             ln2_b=ln2_b, w_router=w_router, router_bias=router_bias, w_gate=w_gate, w_up=w_up,
             w_down=w_down, ln3_g=ln3_g, ln3_b=ln3_b)
    layers = [_prep_layer(l, w) for l in range(w_in.shape[0])]
    y_prompt = _trunk(x_prompt, mem_prompt, w, layers)
    y_sample = _trunk(x_sample, mem_sample, w, layers)
    return (y_prompt, y_sample)
```

```python
import functools

import numpy as np
import jax
import jax.numpy as jnp
from jax import lax
from jax.experimental import pallas as pl
from jax.experimental.pallas import tpu as pltpu

F32 = jnp.float32
BF16 = jnp.bfloat16

D_MODEL = 1024
DEPTH = 4
GRID_W = 64
ROPE_THETA = 10000.0
LN_EPS = 1e-5
RMS_EPS = 1e-6
MLA_HEADS = 6
MLA_Q_LORA = 256
MLA_KV_LORA = 128
MLA_NOPE = 64
MLA_ROPE = 32
MLA_V = 64
CONV_CH = 256
CONV_K = 31
GQA_HEADS = 6
GQA_KV_HEADS = 2
GQA_HD = 64
SPLITS = (MLA_Q_LORA, MLA_KV_LORA, MLA_ROPE, 2 * CONV_CH,
          GQA_HEADS * GQA_HD, GQA_KV_HEADS * GQA_HD, GQA_KV_HEADS * GQA_HD)
XA_HEADS = 4
XA_HD = D_MODEL // XA_HEADS
N_EXPERTS = 16
N_GROUPS = 4
EXPERTS_PER_GROUP = N_EXPERTS // N_GROUPS
D_FF_EXPERT = 512
DN_ALPHA = (2 * DEPTH) ** 0.25
LOG2E = 1.4426950408889634
MLA_SCALE = (MLA_NOPE + MLA_ROPE) ** -0.5 * LOG2E
GQA_SCALE = GQA_HD ** -0.5 * LOG2E
XA_SCALE = XA_HD ** -0.5

LANE = 128
HALF = LANE // 2
SUBLANE = 8
VMEM_LIMIT = 56 * 1024 * 1024

_OFF_CQ = 0
_OFF_CKV = 256
_OFF_KR = 384
_OFF_KRR = 512
_OFF_GA = 640
_OFF_GG = 896
_OFF_QC = 1152
_OFF_QCR = 1536
_OFF_KC = 1920
_OFF_KCR = 2048
_OFF_VC = 2176
_W_EXT = 2432
_GQA_BLOCK_HEADS = (0, 3, 1, 4, 2, 5)
_SUM_LANE = (HALF, 0)


def _cparams(*sem):
    return pltpu.CompilerParams(dimension_semantics=sem, vmem_limit_bytes=VMEM_LIMIT)


def _row_tile(n, want):
    t = min(n, want)
    while n % t:
        t //= 2
    return t


def _ln_rows(x, g, b):
    mu = jnp.mean(x, axis=-1, keepdims=True)
    xc = x - mu
    var = jnp.mean(xc * xc, axis=-1, keepdims=True)
    return xc * lax.rsqrt(var + LN_EPS) * g + b


def _take_cols(w, idx, sign=None):
    w_aug = jnp.concatenate([w, jnp.zeros((w.shape[0], 1), w.dtype)], axis=1)
    idx = np.where(idx < 0, w.shape[1], idx)
    out = w_aug[:, idx]
    if sign is not None:
        out = out * jnp.asarray(sign, w.dtype)[None, :]
    return out


def _rot_src(d, half):
    r = d % (2 * half)
    base = d - r
    if r < half:
        return base + r + half, -1.0
    return base + r - half, 1.0


def _w_ext_index():
    cut = np.concatenate([[0], np.cumsum(SPLITS)])
    c_cq, c_ckv, c_kr, c_glu, c_qc, c_kc, c_vc = [int(c) for c in cut[:-1]]
    idx = -np.ones((_W_EXT,), np.int64)
    sgn = np.ones((_W_EXT,), np.float32)
    idx[_OFF_CQ:_OFF_CQ + 256] = c_cq + np.arange(256)
    idx[_OFF_CKV:_OFF_CKV + 128] = c_ckv + np.arange(128)
    for d in range(MLA_ROPE):
        idx[_OFF_KR + MLA_NOPE + d] = c_kr + d
        s, sg = _rot_src(d, MLA_ROPE // 2)
        idx[_OFF_KRR + MLA_NOPE + d] = c_kr + s
        sgn[_OFF_KRR + MLA_NOPE + d] = sg
    idx[_OFF_GA:_OFF_GA + 256] = c_glu + np.arange(256)
    idx[_OFF_GG:_OFF_GG + 256] = c_glu + 256 + np.arange(256)
    for blk, hq in enumerate(_GQA_BLOCK_HEADS):
        col0 = (blk // 2) * LANE + (blk % 2) * HALF
        for d in range(GQA_HD):
            idx[_OFF_QC + col0 + d] = c_qc + hq * GQA_HD + d
            s, sg = _rot_src(d, GQA_HD // 4)
            idx[_OFF_QCR + col0 + d] = c_qc + hq * GQA_HD + s
            sgn[_OFF_QCR + col0 + d] = sg
    for c in range(GQA_KV_HEADS * GQA_HD):
        hk, d = divmod(c, GQA_HD)
        idx[_OFF_KC + c] = c_kc + c
        s, sg = _rot_src(d, GQA_HD // 4)
        idx[_OFF_KCR + c] = c_kc + hk * GQA_HD + s
        sgn[_OFF_KCR + c] = sg
    idx[_OFF_VC:_OFF_VC + HALF] = c_vc + np.arange(HALF)
    idx[_OFF_VC + LANE + HALF:_OFF_VC + 2 * LANE] = c_vc + HALF + np.arange(HALF)
    return idx, sgn


def _w_uq_index():
    per = MLA_NOPE + MLA_ROPE
    idx = -np.ones((2 * MLA_HEADS * LANE,), np.int64)
    sgn = np.ones(idx.shape, np.float32)
    for h in range(MLA_HEADS):
        idx[h * LANE:h * LANE + per] = h * per + np.arange(per)
        for d in range(MLA_ROPE):
            s, sg = _rot_src(d, MLA_ROPE // 2)
            idx[(MLA_HEADS + h) * LANE + MLA_NOPE + d] = h * per + MLA_NOPE + s
            sgn[(MLA_HEADS + h) * LANE + MLA_NOPE + d] = sg
    return idx, sgn


def _w_ukv_index():
    per = MLA_NOPE + MLA_V
    idx = -np.ones((MLA_HEADS * 2 * LANE,), np.int64)
    for h in range(MLA_HEADS):
        v0 = h * 2 * LANE + (0 if h % 2 == 0 else HALF)
        idx[v0:v0 + MLA_V] = h * per + MLA_NOPE + np.arange(MLA_V)
        idx[h * 2 * LANE + LANE:h * 2 * LANE + LANE + MLA_NOPE] = h * per + np.arange(MLA_NOPE)
    return idx


def _w_oc_index():
    idx = np.zeros((GQA_HEADS * GQA_HD,), np.int64)
    for blk, hq in enumerate(_GQA_BLOCK_HEADS):
        c0 = (blk // 2) * LANE + (blk % 2) * HALF
        idx[c0:c0 + GQA_HD] = hq * GQA_HD + np.arange(GQA_HD)
    return idx


def _perm_gain(g, half):
    src = np.array([_rot_src(d, half)[0] for d in range(g.shape[-1])])
    return g[src]


def _rope_tables(S):
    assert MLA_ROPE == GQA_HD // 2
    inv = 1.0 / (ROPE_THETA ** (jnp.arange(0, MLA_ROPE, 2, dtype=F32) / MLA_ROPE))
    z = lambda k: jnp.zeros((k,), F32)
    per_head = [jnp.concatenate([inv, inv, z(GQA_HD // 2)]), jnp.concatenate([z(GQA_HD // 2), inv, inv])]
    f_pos = jnp.concatenate([z(MLA_NOPE), inv, inv, z(LANE - MLA_NOPE - MLA_ROPE), z(LANE)])
    f_row = jnp.concatenate([z(LANE), per_head[0], per_head[0]])
    f_col = jnp.concatenate([z(LANE), per_head[1], per_head[1]])
    t = jnp.arange(S, dtype=jnp.int32)
    pos = t.astype(F32)[:, None]
    row = (t // GRID_W).astype(F32)[:, None]
    col = (t % GRID_W).astype(F32)[:, None]
    ang = pos * f_pos[None, :] + row * f_row[None, :] + col * f_col[None, :]
    return jnp.cos(ang), jnp.sin(ang)


def _ln_kernel(x_ref, g_ref, b_ref, o_ref):
    o_ref[...] = _ln_rows(x_ref[...], g_ref[...], b_ref[...])


def _layer_norm_rows(x, g, b):
    n, d = x.shape
    tm = _row_tile(n, 1024)
    return pl.pallas_call(
        _ln_kernel,
        grid=(n // tm,),
        in_specs=[pl.BlockSpec((tm, d), lambda i: (i, 0)),
                  pl.BlockSpec((1, d), lambda i: (0, 0)),
                  pl.BlockSpec((1, d), lambda i: (0, 0))],
        out_specs=pl.BlockSpec((tm, d), lambda i: (i, 0)),
        out_shape=jax.ShapeDtypeStruct((n, d), F32),
        compiler_params=_cparams("parallel"),
    )(x, g.reshape(1, d), b.reshape(1, d))


def _rms_scale(ss, width):
    return lax.rsqrt(ss * (1.0 / width) + RMS_EPS)


def _pre_kernel(h_ref, cos_ref, sin_ref, wext_ref, wuq_ref, wukv_ref, qn_ref, kvn_ref,
                gq_ref, gqp_ref, gk_ref, gkp_ref,
                qa_ref, kva_ref, glu_ref, qg_ref, kvg_ref):
    hb = h_ref[0].astype(BF16)
    z = jnp.dot(hb, wext_ref[...], preferred_element_type=F32)
    cm = cos_ref[:, 0:LANE]
    cg = cos_ref[:, LANE:2 * LANE]
    sm = sin_ref[:, 0:LANE]
    sg = sin_ref[:, LANE:2 * LANE]

    cq = z[:, _OFF_CQ:_OFF_CQ + MLA_Q_LORA]
    cqn = cq * _rms_scale(jnp.sum(cq * cq, axis=-1, keepdims=True), MLA_Q_LORA) * qn_ref[...]
    a = jnp.dot(cqn.astype(BF16), wuq_ref[...], preferred_element_type=F32)
    rot0 = MLA_HEADS * LANE
    for h in range(MLA_HEADS):
        roped = a[:, h * LANE:(h + 1) * LANE] * cm + a[:, rot0 + h * LANE:rot0 + (h + 1) * LANE] * sm
        qa_ref[0, :, h * LANE:(h + 1) * LANE] = (roped * MLA_SCALE).astype(BF16)

    ckv = z[:, _OFF_CKV:_OFF_CKV + MLA_KV_LORA]
    ckvn = ckv * _rms_scale(jnp.sum(ckv * ckv, axis=-1, keepdims=True), MLA_KV_LORA) * kvn_ref[...]
    kvm = jnp.dot(ckvn.astype(BF16), wukv_ref[...], preferred_element_type=F32)
    kr = z[:, _OFF_KR:_OFF_KR + LANE] * cm + z[:, _OFF_KRR:_OFF_KRR + LANE] * sm
    lane = lax.broadcasted_iota(jnp.int32, (1, LANE), 1)
    one_at = [jnp.where(lane == L, 1.0, 0.0) for L in _SUM_LANE]
    for h in range(MLA_HEADS):
        c0 = h * 2 * LANE
        kva_ref[0, :, c0:c0 + LANE] = (kvm[:, c0:c0 + LANE] + one_at[h % 2]).astype(BF16)
        kva_ref[0, :, c0 + LANE:c0 + 2 * LANE] = (kvm[:, c0 + LANE:c0 + 2 * LANE] + kr).astype(BF16)

    glu_ref[0] = z[:, _OFF_GA:_OFF_GA + CONV_CH] * jax.nn.sigmoid(z[:, _OFF_GG:_OFF_GG + CONV_CH])

    low = lax.broadcasted_iota(jnp.int32, (h_ref.shape[1], LANE), 1) < HALF

    def norm_rope(x, x_rot, g, g_rot):
        sq = x * x
        r0 = _rms_scale(jnp.sum(jnp.where(low, sq, 0.0), axis=-1, keepdims=True), GQA_HD)
        r1 = _rms_scale(jnp.sum(jnp.where(low, 0.0, sq), axis=-1, keepdims=True), GQA_HD)
        return (x * g * cg + x_rot * g_rot * sg) * jnp.where(low, r0, r1)

    gq = gq_ref[...]
    gqp = gqp_ref[...]
    for p in range(GQA_HEADS // 2):
        qn = norm_rope(z[:, _OFF_QC + p * LANE:_OFF_QC + (p + 1) * LANE],
                       z[:, _OFF_QCR + p * LANE:_OFF_QCR + (p + 1) * LANE], gq, gqp) * GQA_SCALE
        qg_ref[0, :, 2 * p * LANE:(2 * p + 1) * LANE] = jnp.where(low, qn, 0.0).astype(BF16)
        qg_ref[0, :, (2 * p + 1) * LANE:(2 * p + 2) * LANE] = jnp.where(low, 0.0, qn).astype(BF16)

    kvg_ref[0, :, 0:LANE] = norm_rope(z[:, _OFF_KC:_OFF_KC + LANE], z[:, _OFF_KCR:_OFF_KCR + LANE],
                                      gk_ref[...], gkp_ref[...]).astype(BF16)
    kvg_ref[0, :, LANE:2 * LANE] = (z[:, _OFF_VC:_OFF_VC + LANE] + one_at[0]).astype(BF16)
    kvg_ref[0, :, 2 * LANE:3 * LANE] = (z[:, _OFF_VC + LANE:_OFF_VC + 2 * LANE] + one_at[1]).astype(BF16)


def _mixer_pre(h, tab, lw):
    B, S, D = h.shape
    tm = _row_tile(S, 512)
    const = lambda b, i: (0, 0)
    row3 = lambda b, i: (b, i, 0)
    outs = [(MLA_HEADS * LANE, BF16), (MLA_HEADS * 2 * LANE, BF16), (CONV_CH, F32),
            (GQA_HEADS * LANE, BF16), (3 * LANE, BF16)]
    return pl.pallas_call(
        _pre_kernel,
        grid=(B, S // tm),
        in_specs=[pl.BlockSpec((1, tm, D), row3),
                  pl.BlockSpec((tm, 2 * LANE), lambda b, i: (i, 0)),
                  pl.BlockSpec((tm, 2 * LANE), lambda b, i: (i, 0)),
                  pl.BlockSpec(lw["w_ext"].shape, const),
                  pl.BlockSpec(lw["w_uq"].shape, const),
                  pl.BlockSpec(lw["w_ukv"].shape, const),
                  pl.BlockSpec((1, MLA_Q_LORA), const),
                  pl.BlockSpec((1, MLA_KV_LORA), const),
                  pl.BlockSpec((1, LANE), const),
                  pl.BlockSpec((1, LANE), const),
                  pl.BlockSpec((1, LANE), const),
                  pl.BlockSpec((1, LANE), const)],
        out_specs=[pl.BlockSpec((1, tm, w), row3) for w, _ in outs],
        out_shape=[jax.ShapeDtypeStruct((B, S, w), dt) for w, dt in outs],
        compiler_params=_cparams("parallel", "parallel"),
    )(h, tab[0], tab[1], lw["w_ext"], lw["w_uq"], lw["w_ukv"], lw["qn"], lw["kvn"],
      lw["gq"], lw["gqp"], lw["gk"], lw["gkp"])


def _attn_kernel(q_ref, kv_ref, o_ref, m_sc, acc_sc, s_sc, *, k_off, v_off, tkc):
    j = pl.program_id(3)
    tk = kv_ref.shape[1]
    units = [(i, c) for c in range(tk // tkc) for i in range(2)]
    n_col = tkc // LANE

    @pl.when(j == 0)
    def _():
        m_sc[...] = jnp.full(m_sc.shape, -jnp.inf, F32)
        acc_sc[...] = jnp.zeros(acc_sc.shape, F32)

    def scores(u):
        i, c = units[u]
        q = q_ref[0, :, i * LANE:(i + 1) * LANE]
        k = kv_ref[0, c * tkc:(c + 1) * tkc, k_off[i]:k_off[i] + LANE]
        return lax.dot_general(q, k, (((1,), (1,)), ((), ())), preferred_element_type=F32)

    s_sc[0] = scores(0)
    for u, (i, c) in enumerate(units):
        if u + 1 < len(units):
            s_sc[(u + 1) % 2] = scores(u + 1)
        cols = [s_sc[u % 2, :, cb * LANE:(cb + 1) * LANE] for cb in range(n_col)]
        mx = cols[0]
        for x in cols[1:]:
            mx = jnp.maximum(mx, x)
        m_prev = m_sc[i]
        m_new = jnp.maximum(m_prev, jnp.max(mx, axis=1, keepdims=True))
        alpha = jnp.exp2(m_prev - m_new)
        p = jnp.concatenate([jnp.exp2(x - m_new).astype(BF16) for x in cols], axis=1)
        v = kv_ref[0, c * tkc:(c + 1) * tkc, v_off[i]:v_off[i] + LANE]
        acc_sc[i] = alpha * acc_sc[i] + jnp.dot(p, v, preferred_element_type=F32)
        m_sc[i] = m_new

    @pl.when(j == pl.num_programs(3) - 1)
    def _():
        o0 = acc_sc[0] / acc_sc[0, :, _SUM_LANE[0]:_SUM_LANE[0] + 1]
        o1 = acc_sc[1] / acc_sc[1, :, _SUM_LANE[1]:_SUM_LANE[1] + 1]
        low = lax.broadcasted_iota(jnp.int32, o0.shape, 1) < HALF
        o_ref[0] = jnp.where(low, o0, o1).astype(o_ref.dtype)


ATTN_TQ = 1024
ATTN_TK = 4096
ATTN_TKC = 512


def _attention(q, kv, *, kvw_block, kv_block_of_pair, k_off, v_off):
    B, S, _ = q.shape
    n_pairs = 3
    tq = _row_tile(S, ATTN_TQ)
    tk = _row_tile(S, ATTN_TK)
    tkc = _row_tile(tk, ATTN_TKC)
    kern = functools.partial(_attn_kernel, k_off=k_off, v_off=v_off, tkc=tkc)
    return pl.pallas_call(
        kern,
        grid=(B, n_pairs, S // tq, S // tk),
        in_specs=[pl.BlockSpec((1, tq, 2 * LANE), lambda b, p, i, j: (b, i, p)),
                  pl.BlockSpec((1, tk, kvw_block), lambda b, p, i, j: (b, j, kv_block_of_pair(p)))],
        out_specs=pl.BlockSpec((1, tq, LANE), lambda b, p, i, j: (b, i, p)),
        out_shape=jax.ShapeDtypeStruct((B, S, n_pairs * LANE), BF16),
        scratch_shapes=[pltpu.VMEM((2, tq, LANE), F32),
                        pltpu.VMEM((2, tq, LANE), F32),
                        pltpu.VMEM((2, tq, tkc), F32)],
        compiler_params=_cparams("parallel", "parallel", "parallel", "arbitrary"),
    )(q, kv)


def _mla_attention(qa, kva):
    return _attention(qa, kva, kvw_block=4 * LANE, kv_block_of_pair=lambda p: p,
                      k_off=(LANE, 3 * LANE), v_off=(0, 2 * LANE))


def _gqa_attention(qg, kvg):
    return _attention(qg, kvg, kvw_block=3 * LANE, kv_block_of_pair=lambda p: 0,
                      k_off=(0, 0), v_off=(LANE, 2 * LANE))


_HALO = 16


def _conv_kernel(prev_ref, cur_ref, next_ref, w_ref, b_ref, g_ref, beta_ref, o_ref, xe_sc, sh_sc):
    i = pl.program_id(1)
    tc = cur_ref.shape[1]
    first = i == 0
    last = i == pl.num_programs(1) - 1
    xe_sc[0:_HALO, :] = jnp.where(first, 0.0, prev_ref[0])
    xe_sc[_HALO:_HALO + tc, :] = cur_ref[0]
    xe_sc[_HALO + tc:_HALO + tc + _HALO, :] = jnp.where(last, 0.0, next_ref[0])
    acc = jnp.zeros((tc, CONV_CH), F32)
    span = sh_sc.shape[0]
    for shift in range(SUBLANE):
        taps = [t for t in range(CONV_K) if (_HALO - CONV_K // 2 + t) % SUBLANE == shift]
        if not taps:
            continue
        sh_sc[...] = xe_sc[shift:shift + span, :]
        for t in taps:
            base = _HALO - CONV_K // 2 + t - shift
            acc = acc + sh_sc[base:base + tc, :] * w_ref[t:t + 1, :]
    u = acc + b_ref[...]
    y = _ln_rows(u, g_ref[...], beta_ref[...])
    o_ref[0] = (y * jax.nn.sigmoid(y)).astype(o_ref.dtype)


def _conv_module(glu, lw):
    B, S, C = glu.shape
    tc = _row_tile(S, 512)
    nh = tc // _HALO
    n_halo_blocks = S // _HALO
    const = lambda b, i: (0, 0)
    return pl.pallas_call(
        _conv_kernel,
        grid=(B, S // tc),
        in_specs=[pl.BlockSpec((1, _HALO, C), lambda b, i: (b, jnp.maximum(i * nh - 1, 0), 0)),
                  pl.BlockSpec((1, tc, C), lambda b, i: (b, i, 0)),
                  pl.BlockSpec((1, _HALO, C), lambda b, i: (b, jnp.minimum((i + 1) * nh, n_halo_blocks - 1), 0)),
                  pl.BlockSpec((CONV_K, C), const),
                  pl.BlockSpec((1, C), const),
                  pl.BlockSpec((1, C), const),
                  pl.BlockSpec((1, C), const)],
        out_specs=pl.BlockSpec((1, tc, C), lambda b, i: (b, i, 0)),
        out_shape=jax.ShapeDtypeStruct((B, S, C), BF16),
        scratch_shapes=[pltpu.VMEM((tc + 2 * _HALO, C), F32),
                        pltpu.VMEM((tc + 2 * _HALO - SUBLANE, C), F32)],
        compiler_params=_cparams("parallel", "parallel"),
    )(glu, glu, glu, lw["conv_w"], lw["conv_b"], lw["conv_ln_g"], lw["conv_ln_b"])


def _memkv_kernel(x_ref, w_ref, o_ref):
    o_ref[...] = jnp.dot(x_ref[...].astype(BF16), w_ref[...], preferred_element_type=F32).astype(o_ref.dtype)


def _mem_kv(mem2d, w_ckv):
    n, d = mem2d.shape
    tm = _row_tile(n, 256)
    return pl.pallas_call(
        _memkv_kernel,
        grid=(n // tm,),
        in_specs=[pl.BlockSpec((tm, d), lambda i: (i, 0)),
                  pl.BlockSpec(w_ckv.shape, lambda i: (0, 0))],
        out_specs=pl.BlockSpec((tm, w_ckv.shape[1]), lambda i: (i, 0)),
        out_shape=jax.ShapeDtypeStruct((n, w_ckv.shape[1]), BF16),
        compiler_params=_cparams("parallel"),
    )(mem2d, w_ckv)


def _mixout_xattn_kernel(h_ref, oa_ref, ob_ref, oc_ref, wa_ref, wb_ref, wc_ref, g1_ref, b1_ref,
                         k_ref, v_ref, wq_ref, wo_ref, g2_ref, b2_ref, o_ref):
    mix = jnp.dot(oa_ref[0], wa_ref[...], preferred_element_type=F32)
    mix = mix + jnp.dot(ob_ref[0], wb_ref[...], preferred_element_type=F32)
    mix = mix + jnp.dot(oc_ref[0], wc_ref[...], preferred_element_type=F32)
    h = _ln_rows(DN_ALPHA * h_ref[0] + mix, g1_ref[...], b1_ref[...])
    q = (jnp.dot(h.astype(BF16), wq_ref[...], preferred_element_type=F32) * XA_SCALE).astype(BF16)
    heads = []
    for hd in range(XA_HEADS):
        sl = slice(hd * XA_HD, (hd + 1) * XA_HD)
        s = lax.dot_general(q[:, sl], k_ref[0, :, sl], (((1,), (1,)), ((), ())),
                            preferred_element_type=F32)
        p = jnp.exp(s - jnp.max(s, axis=1, keepdims=True))
        o = jnp.dot(p.astype(BF16), v_ref[0, :, sl], preferred_element_type=F32)
        heads.append((o / jnp.sum(p, axis=1, keepdims=True)).astype(BF16))
    o = jnp.concatenate(heads, axis=1)
    y = jnp.dot(o, wo_ref[...], preferred_element_type=F32)
    o_ref[0] = _ln_rows(DN_ALPHA * h + y, g2_ref[...], b2_ref[...])


def _mixout_cross_attention(h, oa, ob, oc, kvmem, lw):
    B, S, D = h.shape
    M = kvmem.shape[1]
    tm = _row_tile(S, 512)
    const = lambda b, i: (0, 0)
    row = lambda b, i: (b, i, 0)
    return pl.pallas_call(
        _mixout_xattn_kernel,
        grid=(B, S // tm),
        in_specs=[pl.BlockSpec((1, tm, D), row),
                  pl.BlockSpec((1, tm, oa.shape[2]), row),
                  pl.BlockSpec((1, tm, ob.shape[2]), row),
                  pl.BlockSpec((1, tm, oc.shape[2]), row),
                  pl.BlockSpec(lw["wo_a"].shape, const),
                  pl.BlockSpec(lw["wo_b"].shape, const),
                  pl.BlockSpec(lw["wo_c"].shape, const),
                  pl.BlockSpec((1, D), const),
                  pl.BlockSpec((1, D), const),
                  pl.BlockSpec((1, M, D), lambda b, i: (b, 0, 0)),
                  pl.BlockSpec((1, M, D), lambda b, i: (b, 0, 1)),
                  pl.BlockSpec((D, D), const),
                  pl.BlockSpec((D, D), const),
                  pl.BlockSpec((1, D), const),
                  pl.BlockSpec((1, D), const)],
        out_specs=pl.BlockSpec((1, tm, D), row),
        out_shape=jax.ShapeDtypeStruct((B, S, D), F32),
        compiler_params=_cparams("parallel", "parallel"),
    )(h, oa, ob, oc, lw["wo_a"], lw["wo_b"], lw["wo_c"], lw["ln1_g"], lw["ln1_b"],
      kvmem, kvmem, lw["w_cq"], lw["w_co"], lw["ln2_g"], lw["ln2_b"])


MOE_TILE = 1024
MOE_SUB = 256
MOE_SUB_SHIFT = 8
MOE_SEG = 64
MOE_ALIGN = 16
MOE_EXPERTS_PER_STEP = 2


def _router_kernel(x_ref, wr_ref, bias_ref, gate_ref, pos_ref, dd_ref, cnt_ref, start_ref, sel_sc):
    T = x_ref.shape[0]
    x = x_ref[...]
    x_hi = x.astype(BF16)
    x_lo = (x - x_hi.astype(F32)).astype(BF16)
    w = wr_ref[...]
    w_hi = w.astype(BF16)
    w_lo = (w - w_hi.astype(F32)).astype(BF16)
    nt = (((1,), (1,)), ((), ()))
    both = lax.dot_general(jnp.concatenate([w_hi, w_lo], axis=0), x_hi, nt, preferred_element_type=F32)
    logits = (both[:N_EXPERTS] + both[N_EXPERTS:]
              + lax.dot_general(w_hi, x_lo, nt, preferred_element_type=F32))
    scores = jax.nn.sigmoid(logits)
    sel = scores + bias_ref[...]
    sc = [scores[e:e + 1, :] for e in range(N_EXPERTS)]
    se = [sel[e:e + 1, :] for e in range(N_EXPERTS)]

    gs = []
    for g in range(N_GROUPS):
        m = [se[g * EXPERTS_PER_GROUP + k] for k in range(EXPERTS_PER_GROUP)]
        best = None
        for a in range(EXPERTS_PER_GROUP):
            for b in range(a + 1, EXPERTS_PER_GROUP):
                pair = m[a] + m[b]
                best = pair if best is None else jnp.maximum(best, pair)
        gs.append(best)
    in_group = []
    for g in range(N_GROUPS):
        ok = None
        for o in range(N_GROUPS):
            if o == g:
                continue
            c = (gs[g] > gs[o]) if o < g else (gs[g] >= gs[o])
            ok = c if ok is None else (ok & c)
        in_group.append(ok)
    chosen = []
    for e in range(N_EXPERTS):
        g = e // EXPERTS_PER_GROUP
        beaten = jnp.zeros((1, T), jnp.int32)
        for o in range(g * EXPERTS_PER_GROUP, (g + 1) * EXPERTS_PER_GROUP):
            if o == e:
                continue
            c = (se[o] > se[e]) if o > e else (se[o] >= se[e])
            beaten = beaten + c.astype(jnp.int32)
        chosen.append(in_group[g] & (beaten < 2))
    denom = jnp.zeros((1, T), F32)
    for e in range(N_EXPERTS):
        denom = denom + jnp.where(chosen[e], sc[e], 0.0)
    for e in range(N_EXPERTS):
        gate_ref[e:e + 1, :] = jnp.where(chosen[e], sc[e] / denom, 0.0)
        sel_sc[e:e + 1, :] = chosen[e].astype(F32)

    selm = sel_sc[...].astype(BF16)
    sub_r = jnp.right_shift(lax.broadcasted_iota(jnp.int32, (T, T), 0), MOE_SUB_SHIFT)
    sub_c = jnp.right_shift(lax.broadcasted_iota(jnp.int32, (T, T), 1), MOE_SUB_SHIFT)
    same = sub_r == sub_c
    earlier = same & (lax.broadcasted_iota(jnp.int32, (T, T), 0) < lax.broadcasted_iota(jnp.int32, (T, T), 1))
    rank = jnp.dot(selm, earlier.astype(BF16), preferred_element_type=F32)
    cnt = jnp.dot(selm, same.astype(BF16), preferred_element_type=F32)
    cnt_ref[0] = cnt.astype(jnp.int32)

    seg = jnp.floor((cnt + (MOE_ALIGN - 1)) * (1.0 / MOE_ALIGN)) * MOE_ALIGN
    start = jnp.zeros((1, T), F32)
    d_lo = jnp.full((1, T), float(4 * T), F32)
    d_hi = jnp.full((1, T), -1.0, F32)
    for e in range(N_EXPERTS):
        start_ref[0, e:e + 1, :] = start.astype(jnp.int32)
        dest = jnp.where(chosen[e], rank[e:e + 1, :] + start, -1.0)
        pos_ref[e:e + 1, :] = dest
        d_lo = jnp.where(chosen[e], jnp.minimum(d_lo, dest), d_lo)
        d_hi = jnp.maximum(d_hi, dest)
        start = start + seg[e:e + 1, :]
    dd_ref[...] = jnp.zeros(dd_ref.shape, F32)
    dd_ref[0:1, :] = d_lo
    dd_ref[1:2, :] = d_hi


def _router(x, w_rt, bias):
    n, d = x.shape
    T = _row_tile(n, MOE_TILE)
    nt = n // T
    gate, pos, dd, cnt, start = pl.pallas_call(
        _router_kernel,
        grid=(nt,),
        in_specs=[pl.BlockSpec((T, d), lambda i: (i, 0)),
                  pl.BlockSpec((N_EXPERTS, d), lambda i: (0, 0)),
                  pl.BlockSpec((N_EXPERTS, 1), lambda i: (0, 0))],
        out_specs=[pl.BlockSpec((N_EXPERTS, T), lambda i: (0, i)),
                   pl.BlockSpec((N_EXPERTS, T), lambda i: (0, i)),
                   pl.BlockSpec((8, T), lambda i: (0, i)),
                   pl.BlockSpec((1, N_EXPERTS, T), lambda i: (i, 0, 0)),
                   pl.BlockSpec((1, N_EXPERTS, T), lambda i: (i, 0, 0))],
        out_shape=[jax.ShapeDtypeStruct((N_EXPERTS, n), F32),
                   jax.ShapeDtypeStruct((N_EXPERTS, n), F32),
                   jax.ShapeDtypeStruct((8, n), F32),
                   jax.ShapeDtypeStruct((nt, N_EXPERTS, T), jnp.int32),
                   jax.ShapeDtypeStruct((nt, N_EXPERTS, T), jnp.int32)],
        scratch_shapes=[pltpu.VMEM((N_EXPERTS, T), F32)],
        compiler_params=_cparams("parallel"),
    )(x, w_rt, bias)
    return gate, pos, dd, cnt[:, :, ::MOE_SUB].reshape(-1), start[:, :, ::MOE_SUB].reshape(-1)


MOE_SORTED_ROWS = 2 * MOE_SUB + N_EXPERTS * MOE_ALIGN + MOE_SEG


def _moe_kernel(cnt_ref, start_ref, h_ref, gate_ref, pos_ref, dd_ref, wg_ref, wu_ref, wd_ref, g_ref, b_ref, o_ref,
                p_sc, xs_sc, ys_sc):
    i = pl.program_id(0)
    step = pl.program_id(1)
    T = h_ref.shape[0]
    n_sub = p_sc.shape[0]
    R = p_sc.shape[1]
    half = R // 2

    @pl.when(step == 0)
    def _():
        for s in range(n_sub):
            tok = slice(s * MOE_SUB, (s + 1) * MOE_SUB)
            d_lo = dd_ref[0:1, tok].astype(jnp.int32)
            d_hi = dd_ref[1:2, tok].astype(jnp.int32)
            xb = h_ref[tok, :].astype(BF16)
            for r0 in (0, half):
                row = lax.broadcasted_iota(jnp.int32, (half, MOE_SUB), 0) + r0
                onehot = ((row == d_lo) | (row == d_hi)).astype(BF16)
                p_sc[s, r0:r0 + half, :] = onehot
                xs_sc[s, r0:r0 + half, :] = jnp.dot(onehot, xb, preferred_element_type=F32).astype(BF16)
        ys_sc[...] = jnp.zeros(ys_sc.shape, BF16)

    slot = lax.broadcasted_iota(jnp.int32, (MOE_SEG, MOE_SUB), 0)
    for k in range(MOE_EXPERTS_PER_STEP):
        e = step * MOE_EXPERTS_PER_STEP + k
        flat = (i * N_EXPERTS + e) * n_sub
        n_rows = [cnt_ref[flat + s] for s in range(n_sub)]
        seg0 = [start_ref[flat + s] for s in range(n_sub)]
        most = n_rows[0]
        for c in n_rows[1:]:
            most = jnp.maximum(most, c)
        n_blocks = (most + MOE_SEG - 1) // MOE_SEG
        pos_row = pos_ref[pl.ds(e, 1), :].astype(jnp.int32)
        gate_row = gate_ref[pl.ds(e, 1), :]

        def block(jb, carry, k=k, n_rows=n_rows, seg0=seg0, pos_row=pos_row, gate_row=gate_row):
            base = [pl.multiple_of(jnp.minimum(seg0[s] + jb * MOE_SEG, R - MOE_SEG), MOE_ALIGN)
                    for s in range(n_sub)]
            dest = [pl.multiple_of(jnp.where(jb * MOE_SEG < n_rows[s], base[s], R), MOE_ALIGN)
                    for s in range(n_sub)]
            xg = jnp.concatenate([xs_sc[s, pl.ds(base[s], MOE_SEG), :] for s in range(n_sub)], axis=0)
            hg = jnp.dot(xg, wg_ref[k], preferred_element_type=F32)
            hu = jnp.dot(xg, wu_ref[k], preferred_element_type=F32)
            act = (hg * jax.nn.sigmoid(hg) * hu).astype(BF16)
            y = jnp.dot(act, wd_ref[k], preferred_element_type=F32)
            for s in range(n_sub):
                tok = slice(s * MOE_SUB, (s + 1) * MOE_SUB)
                hit = (pos_row[:, tok] - base[s]) == slot
                g_rows = jnp.sum(jnp.where(hit, gate_row[:, tok], 0.0), axis=1, keepdims=True)
                ys_sc[s, pl.ds(dest[s], MOE_SEG), :] = (y[s * MOE_SEG:(s + 1) * MOE_SEG, :] * g_rows).astype(BF16)
            return carry

        lax.fori_loop(0, n_blocks, block, 0)

    @pl.when(step == N_EXPERTS // MOE_EXPERTS_PER_STEP - 1)
    def _():
        for s in range(n_sub):
            tok = slice(s * MOE_SUB, (s + 1) * MOE_SUB)
            moe = lax.dot_general(p_sc[s], ys_sc[s, 0:R, :], (((0,), (0,)), ((), ())), preferred_element_type=F32)
            o_ref[tok, :] = _ln_rows(DN_ALPHA * h_ref[tok, :] + moe, g_ref[...], b_ref[...])


def _moe(h, gate, pos, dd, cnt, start, lw):
    n, d = h.shape
    T = _row_tile(n, MOE_TILE)
    n_sub = T // MOE_SUB
    R = MOE_SORTED_ROWS
    f = D_FF_EXPERT
    ne = MOE_EXPERTS_PER_STEP
    grid_spec = pltpu.PrefetchScalarGridSpec(
        num_scalar_prefetch=2,
        grid=(n // T, N_EXPERTS // ne),
        in_specs=[pl.BlockSpec((T, d), lambda i, e, c, s: (i, 0)),
                  pl.BlockSpec((N_EXPERTS, T), lambda i, e, c, s: (0, i)),
                  pl.BlockSpec((N_EXPERTS, T), lambda i, e, c, s: (0, i)),
                  pl.BlockSpec((8, T), lambda i, e, c, s: (0, i)),
                  pl.BlockSpec((ne, d, f), lambda i, e, c, s: (e, 0, 0)),
                  pl.BlockSpec((ne, d, f), lambda i, e, c, s: (e, 0, 0)),
                  pl.BlockSpec((ne, f, d), lambda i, e, c, s: (e, 0, 0)),
                  pl.BlockSpec((1, d), lambda i, e, c, s: (0, 0)),
                  pl.BlockSpec((1, d), lambda i, e, c, s: (0, 0))],
        out_specs=pl.BlockSpec((T, d), lambda i, e, c, s: (i, 0)),
        scratch_shapes=[pltpu.VMEM((n_sub, R, MOE_SUB), BF16), pltpu.VMEM((n_sub, R, d), BF16),
                        pltpu.VMEM((n_sub, R + MOE_SEG, d), BF16)],
    )
    return pl.pallas_call(
        _moe_kernel,
        grid_spec=grid_spec,
        out_shape=jax.ShapeDtypeStruct((n, d), F32),
        compiler_params=_cparams("parallel", "arbitrary"),
    )(cnt, start, h, gate, pos, dd, lw["w_gate"], lw["w_up"], lw["w_down"], lw["ln3_g"], lw["ln3_b"])


def _prep_layer(l, w):
    ext_idx, ext_sgn = _w_ext_index()
    uq_idx, uq_sgn = _w_uq_index()
    mla_w = MLA_HEADS * MLA_V
    gq = w["gqa_q_norm"][l]
    gk = w["gqa_k_norm"][l]
    row = lambda v: v.reshape(1, -1).astype(F32)
    return {
        "w_ext": _take_cols(w["w_in"][l], ext_idx, ext_sgn).astype(BF16),
        "w_uq": _take_cols(w["mla_w_uq"][l], uq_idx, uq_sgn).astype(BF16),
        "w_ukv": _take_cols(w["mla_w_ukv"][l], _w_ukv_index()).astype(BF16),
        "qn": row(w["mla_q_norm"][l]),
        "kvn": row(w["mla_kv_norm"][l]),
        "gq": row(jnp.tile(gq, 2)),
        "gqp": row(jnp.tile(_perm_gain(gq, GQA_HD // 4), 2)),
        "gk": row(jnp.tile(gk, 2)),
        "gkp": row(jnp.tile(_perm_gain(gk, GQA_HD // 4), 2)),
        "conv_w": w["conv_w"][l].astype(F32),
        "conv_b": row(w["conv_b"][l]),
        "conv_ln_g": row(w["conv_ln_g"][l]),
        "conv_ln_b": row(w["conv_ln_b"][l]),
        "wo_a": w["w_o"][l][:mla_w].astype(BF16),
        "wo_b": w["w_o"][l][mla_w:mla_w + CONV_CH].astype(BF16),
        "wo_c": w["w_o"][l][mla_w + CONV_CH:][_w_oc_index()].astype(BF16),
        "ln1_g": row(w["ln1_g"][l]), "ln1_b": row(w["ln1_b"][l]),
        "w_cq": w["w_cq"][l].astype(BF16),
        "w_ckv": w["w_ckv"][l].astype(BF16),
        "w_co": w["w_co"][l].astype(BF16),
        "ln2_g": row(w["ln2_g"][l]), "ln2_b": row(w["ln2_b"][l]),
        "w_gate": w["w_gate"][l].astype(BF16),
        "w_up": w["w_up"][l].astype(BF16),
        "w_down": w["w_down"][l].astype(BF16),
        "ln3_g": row(w["ln3_g"][l]), "ln3_b": row(w["ln3_b"][l]),
    }


def _trunk(x, mem, w, layers):
    B, S, D = x.shape
    n = B * S
    tab = _rope_tables(S)
    w_rt = w["w_router"].T.astype(F32)
    r_bias = w["router_bias"].reshape(N_EXPERTS, 1).astype(F32)
    h = _layer_norm_rows(x.reshape(n, D), w["ln_in_g"], w["ln_in_b"])
    mem2d = mem.reshape(B * mem.shape[1], D)
    for lw in layers:
        qa, kva, glu, qg, kvg = _mixer_pre(h.reshape(B, S, D), tab, lw)
        oa = _mla_attention(qa, kva)
        ob = _conv_module(glu, lw)
        oc = _gqa_attention(qg, kvg)
        kvmem = _mem_kv(mem2d, lw["w_ckv"]).reshape(B, mem.shape[1], 2 * D)
        h = _mixout_cross_attention(h.reshape(B, S, D), oa, ob, oc, kvmem, lw).reshape(n, D)
        gate, pos, dd, cnt, start = _router(h, w_rt, r_bias)
        h = _moe(h, gate, pos, dd, cnt, start, lw)
    return h.reshape(B, S, D)


def kernel(x_prompt, x_sample, mem_prompt, mem_sample, ln_in_g, ln_in_b, w_in, mla_q_norm, mla_w_uq,
           mla_kv_norm, mla_w_ukv, conv_w, conv_b, conv_ln_g, conv_ln_b, gqa_q_norm, gqa_k_norm, w_o,
           ln1_g, ln1_b, w_cq, w_ckv, w_co, ln2_g, ln2_b, w_router, router_bias, w_gate, w_up, w_down,
           ln3_g, ln3_b):
    w = dict(ln_in_g=ln_in_g, ln_in_b=ln_in_b, w_in=w_in, mla_q_norm=mla_q_norm, mla_w_uq=mla_w_uq,
             mla_kv_norm=mla_kv_norm, mla_w_ukv=mla_w_ukv, conv_w=conv_w, conv_b=conv_b,
             conv_ln_g=conv_ln_g, conv_ln_b=conv_ln_b, gqa_q_norm=gqa_q_norm, gqa_k_norm=gqa_k_norm,
             w_o=w_o, ln1_g=ln1_g, ln1_b=ln1_b, w_cq=w_cq, w_ckv=w_ckv, w_co=w_co, ln2_g=ln2_g,
             ln2_b=ln2_b, w_router=w_router, router_bias=router_bias, w_gate=w_gate, w_up=w_up,
             w_down=w_down, ln3_g=ln3_g, ln3_b=ln3_b)
    layers = [_prep_layer(l, w) for l in range(w_in.shape[0])]
    y_prompt = _trunk(x_prompt, mem_prompt, w, layers)
    y_sample = _trunk(x_sample, mem_sample, w, layers)
    return (y_prompt, y_sample)
```

```python
import functools

import numpy as np
import jax
import jax.numpy as jnp
from jax import lax
from jax.experimental import pallas as pl
from jax.experimental.pallas import tpu as pltpu

F32 = jnp.float32
BF16 = jnp.bfloat16

D_MODEL = 1024
DEPTH = 4
GRID_W = 64
ROPE_THETA = 10000.0
LN_EPS = 1e-5
RMS_EPS = 1e-6
MLA_HEADS = 6
MLA_Q_LORA = 256
MLA_KV_LORA = 128
MLA_NOPE = 64
MLA_ROPE = 32
MLA_V = 64
CONV_CH = 256
CONV_K = 31
GQA_HEADS = 6
GQA_KV_HEADS = 2
GQA_HD = 64
SPLITS = (MLA_Q_LORA, MLA_KV_LORA, MLA_ROPE, 2 * CONV_CH,
          GQA_HEADS * GQA_HD, GQA_KV_HEADS * GQA_HD, GQA_KV_HEADS * GQA_HD)
XA_HEADS = 4
XA_HD = D_MODEL // XA_HEADS
N_EXPERTS = 16
N_GROUPS = 4
EXPERTS_PER_GROUP = N_EXPERTS // N_GROUPS
D_FF_EXPERT = 512
DN_ALPHA = (2 * DEPTH) ** 0.25
LOG2E = 1.4426950408889634
MLA_SCALE = (MLA_NOPE + MLA_ROPE) ** -0.5 * LOG2E
GQA_SCALE = GQA_HD ** -0.5 * LOG2E
XA_SCALE = XA_HD ** -0.5
XA_ROW_GROUPS = 4

LANE = 128
HALF = LANE // 2
SUBLANE = 8
VMEM_LIMIT = 56 * 1024 * 1024

_OFF_CQ = 0
_OFF_CKV = 256
_OFF_KR = 384
_OFF_KRR = 512
_OFF_GA = 640
_OFF_GG = 896
_OFF_QC = 1152
_OFF_QCR = 1536
_OFF_KC = 1920
_OFF_KCR = 2048
_OFF_VC = 2176
_W_EXT = 2432
_GQA_BLOCK_HEADS = (0, 3, 1, 4, 2, 5)
_SUM_LANE = (HALF, 0)


def _cparams(*sem):
    return pltpu.CompilerParams(dimension_semantics=sem, vmem_limit_bytes=VMEM_LIMIT)


def _row_tile(n, want):
    t = min(n, want)
    while n % t:
        t //= 2
    return t


def _ln_rows(x, g, b):
    mu = jnp.mean(x, axis=-1, keepdims=True)
    xc = x - mu
    var = jnp.mean(xc * xc, axis=-1, keepdims=True)
    return xc * lax.rsqrt(var + LN_EPS) * g + b


def _take_cols(w, idx, sign=None):
    w_aug = jnp.concatenate([w, jnp.zeros((w.shape[0], 1), w.dtype)], axis=1)
    idx = np.where(idx < 0, w.shape[1], idx)
    out = w_aug[:, idx]
    if sign is not None:
        out = out * jnp.asarray(sign, w.dtype)[None, :]
    return out


def _rot_src(d, half):
    r = d % (2 * half)
    base = d - r
    if r < half:
        return base + r + half, -1.0
    return base + r - half, 1.0


def _w_ext_index():
    cut = np.concatenate([[0], np.cumsum(SPLITS)])
    c_cq, c_ckv, c_kr, c_glu, c_qc, c_kc, c_vc = [int(c) for c in cut[:-1]]
    idx = -np.ones((_W_EXT,), np.int64)
    sgn = np.ones((_W_EXT,), np.float32)
    idx[_OFF_CQ:_OFF_CQ + 256] = c_cq + np.arange(256)
    idx[_OFF_CKV:_OFF_CKV + 128] = c_ckv + np.arange(128)
    for d in range(MLA_ROPE):
        idx[_OFF_KR + MLA_NOPE + d] = c_kr + d
        s, sg = _rot_src(d, MLA_ROPE // 2)
        idx[_OFF_KRR + MLA_NOPE + d] = c_kr + s
        sgn[_OFF_KRR + MLA_NOPE + d] = sg
    idx[_OFF_GA:_OFF_GA + 256] = c_glu + np.arange(256)
    idx[_OFF_GG:_OFF_GG + 256] = c_glu + 256 + np.arange(256)
    for blk, hq in enumerate(_GQA_BLOCK_HEADS):
        col0 = (blk // 2) * LANE + (blk % 2) * HALF
        for d in range(GQA_HD):
            idx[_OFF_QC + col0 + d] = c_qc + hq * GQA_HD + d
            s, sg = _rot_src(d, GQA_HD // 4)
            idx[_OFF_QCR + col0 + d] = c_qc + hq * GQA_HD + s
            sgn[_OFF_QCR + col0 + d] = sg
    for c in range(GQA_KV_HEADS * GQA_HD):
        hk, d = divmod(c, GQA_HD)
        idx[_OFF_KC + c] = c_kc + c
        s, sg = _rot_src(d, GQA_HD // 4)
        idx[_OFF_KCR + c] = c_kc + hk * GQA_HD + s
        sgn[_OFF_KCR + c] = sg
    idx[_OFF_VC:_OFF_VC + HALF] = c_vc + np.arange(HALF)
    idx[_OFF_VC + LANE + HALF:_OFF_VC + 2 * LANE] = c_vc + HALF + np.arange(HALF)
    return idx, sgn


def _w_uq_index():
    per = MLA_NOPE + MLA_ROPE
    idx = -np.ones((2 * MLA_HEADS * LANE,), np.int64)
    sgn = np.ones(idx.shape, np.float32)
    for h in range(MLA_HEADS):
        idx[h * LANE:h * LANE + per] = h * per + np.arange(per)
        for d in range(MLA_ROPE):
            s, sg = _rot_src(d, MLA_ROPE // 2)
            idx[(MLA_HEADS + h) * LANE + MLA_NOPE + d] = h * per + MLA_NOPE + s
            sgn[(MLA_HEADS + h) * LANE + MLA_NOPE + d] = sg
    return idx, sgn


def _w_ukv_index():
    per = MLA_NOPE + MLA_V
    idx = -np.ones((MLA_HEADS * 2 * LANE,), np.int64)
    for h in range(MLA_HEADS):
        v0 = h * 2 * LANE + (0 if h % 2 == 0 else HALF)
        idx[v0:v0 + MLA_V] = h * per + MLA_NOPE + np.arange(MLA_V)
        idx[h * 2 * LANE + LANE:h * 2 * LANE + LANE + MLA_NOPE] = h * per + np.arange(MLA_NOPE)
    return idx


def _w_oc_index():
    idx = np.zeros((GQA_HEADS * GQA_HD,), np.int64)
    for blk, hq in enumerate(_GQA_BLOCK_HEADS):
        c0 = (blk // 2) * LANE + (blk % 2) * HALF
        idx[c0:c0 + GQA_HD] = hq * GQA_HD + np.arange(GQA_HD)
    return idx


def _perm_gain(g, half):
    src = np.array([_rot_src(d, half)[0] for d in range(g.shape[-1])])
    return g[src]


def _rope_tables(S):
    assert MLA_ROPE == GQA_HD // 2
    inv = 1.0 / (ROPE_THETA ** (jnp.arange(0, MLA_ROPE, 2, dtype=F32) / MLA_ROPE))
    z = lambda k: jnp.zeros((k,), F32)
    per_head = [jnp.concatenate([inv, inv, z(GQA_HD // 2)]), jnp.concatenate([z(GQA_HD // 2), inv, inv])]
    f_pos = jnp.concatenate([z(MLA_NOPE), inv, inv, z(LANE - MLA_NOPE - MLA_ROPE), z(LANE)])
    f_row = jnp.concatenate([z(LANE), per_head[0], per_head[0]])
    f_col = jnp.concatenate([z(LANE), per_head[1], per_head[1]])
    t = jnp.arange(S, dtype=jnp.int32)
    pos = t.astype(F32)[:, None]
    row = (t // GRID_W).astype(F32)[:, None]
    col = (t % GRID_W).astype(F32)[:, None]
    ang = pos * f_pos[None, :] + row * f_row[None, :] + col * f_col[None, :]
    return jnp.cos(ang), jnp.sin(ang)


def _ln_kernel(x_ref, g_ref, b_ref, o_ref):
    o_ref[...] = _ln_rows(x_ref[...], g_ref[...], b_ref[...])


def _layer_norm_rows(x, g, b):
    n, d = x.shape
    tm = _row_tile(n, 1024)
    return pl.pallas_call(
        _ln_kernel,
        grid=(n // tm,),
        in_specs=[pl.BlockSpec((tm, d), lambda i: (i, 0)),
                  pl.BlockSpec((1, d), lambda i: (0, 0)),
                  pl.BlockSpec((1, d), lambda i: (0, 0))],
        out_specs=pl.BlockSpec((tm, d), lambda i: (i, 0)),
        out_shape=jax.ShapeDtypeStruct((n, d), F32),
        compiler_params=_cparams("parallel"),
    )(x, g.reshape(1, d), b.reshape(1, d))


def _rms_scale(ss, width):
    return lax.rsqrt(ss * (1.0 / width) + RMS_EPS)


def _pre_kernel(h_ref, cos_ref, sin_ref, wext_ref, wuq_ref, wukv_ref, qn_ref, kvn_ref,
                gq_ref, gqp_ref, gk_ref, gkp_ref,
                qa_ref, kva_ref, glu_ref, qg_ref, kvg_ref):
    hb = h_ref[0].astype(BF16)
    z = jnp.dot(hb, wext_ref[...], preferred_element_type=F32)
    cm = cos_ref[:, 0:LANE]
    cg = cos_ref[:, LANE:2 * LANE]
    sm = sin_ref[:, 0:LANE]
    sg = sin_ref[:, LANE:2 * LANE]

    cq = z[:, _OFF_CQ:_OFF_CQ + MLA_Q_LORA]
    cqn = cq * _rms_scale(jnp.sum(cq * cq, axis=-1, keepdims=True), MLA_Q_LORA) * qn_ref[...]
    a = jnp.dot(cqn.astype(BF16), wuq_ref[...], preferred_element_type=F32)
    rot0 = MLA_HEADS * LANE
    for h in range(MLA_HEADS):
        roped = a[:, h * LANE:(h + 1) * LANE] * cm + a[:, rot0 + h * LANE:rot0 + (h + 1) * LANE] * sm
        qa_ref[0, :, h * LANE:(h + 1) * LANE] = (roped * MLA_SCALE).astype(BF16)

    ckv = z[:, _OFF_CKV:_OFF_CKV + MLA_KV_LORA]
    ckvn = ckv * _rms_scale(jnp.sum(ckv * ckv, axis=-1, keepdims=True), MLA_KV_LORA) * kvn_ref[...]
    kvm = jnp.dot(ckvn.astype(BF16), wukv_ref[...], preferred_element_type=F32)
    kr = z[:, _OFF_KR:_OFF_KR + LANE] * cm + z[:, _OFF_KRR:_OFF_KRR + LANE] * sm
    lane = lax.broadcasted_iota(jnp.int32, (1, LANE), 1)
    one_at = [jnp.where(lane == L, 1.0, 0.0) for L in _SUM_LANE]
    for h in range(MLA_HEADS):
        c0 = h * 2 * LANE
        kva_ref[0, :, c0:c0 + LANE] = (kvm[:, c0:c0 + LANE] + one_at[h % 2]).astype(BF16)
        kva_ref[0, :, c0 + LANE:c0 + 2 * LANE] = (kvm[:, c0 + LANE:c0 + 2 * LANE] + kr).astype(BF16)

    glu_ref[0] = z[:, _OFF_GA:_OFF_GA + CONV_CH] * jax.nn.sigmoid(z[:, _OFF_GG:_OFF_GG + CONV_CH])

    low = lax.broadcasted_iota(jnp.int32, (h_ref.shape[1], LANE), 1) < HALF

    def norm_rope(x, x_rot, g, g_rot):
        sq = x * x
        r0 = _rms_scale(jnp.sum(jnp.where(low, sq, 0.0), axis=-1, keepdims=True), GQA_HD)
        r1 = _rms_scale(jnp.sum(jnp.where(low, 0.0, sq), axis=-1, keepdims=True), GQA_HD)
        return (x * g * cg + x_rot * g_rot * sg) * jnp.where(low, r0, r1)

    gq = gq_ref[...]
    gqp = gqp_ref[...]
    for p in range(GQA_HEADS // 2):
        qn = norm_rope(z[:, _OFF_QC + p * LANE:_OFF_QC + (p + 1) * LANE],
                       z[:, _OFF_QCR + p * LANE:_OFF_QCR + (p + 1) * LANE], gq, gqp) * GQA_SCALE
        qg_ref[0, :, 2 * p * LANE:(2 * p + 1) * LANE] = jnp.where(low, qn, 0.0).astype(BF16)
        qg_ref[0, :, (2 * p + 1) * LANE:(2 * p + 2) * LANE] = jnp.where(low, 0.0, qn).astype(BF16)

    kvg_ref[0, :, 0:LANE] = norm_rope(z[:, _OFF_KC:_OFF_KC + LANE], z[:, _OFF_KCR:_OFF_KCR + LANE],
                                      gk_ref[...], gkp_ref[...]).astype(BF16)
    kvg_ref[0, :, LANE:2 * LANE] = (z[:, _OFF_VC:_OFF_VC + LANE] + one_at[0]).astype(BF16)
    kvg_ref[0, :, 2 * LANE:3 * LANE] = (z[:, _OFF_VC + LANE:_OFF_VC + 2 * LANE] + one_at[1]).astype(BF16)


def _mixer_pre(h, tab, lw):
    B, S, D = h.shape
    tm = _row_tile(S, 512)
    const = lambda b, i: (0, 0)
    row3 = lambda b, i: (b, i, 0)
    outs = [(MLA_HEADS * LANE, BF16), (MLA_HEADS * 2 * LANE, BF16), (CONV_CH, F32),
            (GQA_HEADS * LANE, BF16), (3 * LANE, BF16)]
    return pl.pallas_call(
        _pre_kernel,
        grid=(B, S // tm),
        in_specs=[pl.BlockSpec((1, tm, D), row3),
                  pl.BlockSpec((tm, 2 * LANE), lambda b, i: (i, 0)),
                  pl.BlockSpec((tm, 2 * LANE), lambda b, i: (i, 0)),
                  pl.BlockSpec(lw["w_ext"].shape, const),
                  pl.BlockSpec(lw["w_uq"].shape, const),
                  pl.BlockSpec(lw["w_ukv"].shape, const),
                  pl.BlockSpec((1, MLA_Q_LORA), const),
                  pl.BlockSpec((1, MLA_KV_LORA), const),
                  pl.BlockSpec((1, LANE), const),
                  pl.BlockSpec((1, LANE), const),
                  pl.BlockSpec((1, LANE), const),
                  pl.BlockSpec((1, LANE), const)],
        out_specs=[pl.BlockSpec((1, tm, w), row3) for w, _ in outs],
        out_shape=[jax.ShapeDtypeStruct((B, S, w), dt) for w, dt in outs],
        compiler_params=_cparams("parallel", "parallel"),
    )(h, tab[0], tab[1], lw["w_ext"], lw["w_uq"], lw["w_ukv"], lw["qn"], lw["kvn"],
      lw["gq"], lw["gqp"], lw["gk"], lw["gkp"])


def _attn_kernel(q_ref, kv_ref, o_ref, m_sc, acc_sc, s_sc, *, k_off, v_off, tkc):
    j = pl.program_id(3)
    tk = kv_ref.shape[1]
    units = [(i, c) for c in range(tk // tkc) for i in range(2)]
    n_col = tkc // LANE

    @pl.when(j == 0)
    def _():
        m_sc[...] = jnp.full(m_sc.shape, -jnp.inf, F32)
        acc_sc[...] = jnp.zeros(acc_sc.shape, F32)

    def scores(u):
        i, c = units[u]
        q = q_ref[0, :, i * LANE:(i + 1) * LANE]
        k = kv_ref[0, c * tkc:(c + 1) * tkc, k_off[i]:k_off[i] + LANE]
        return lax.dot_general(q, k, (((1,), (1,)), ((), ())), preferred_element_type=F32)

    s_sc[0] = scores(0)
    for u, (i, c) in enumerate(units):
        if u + 1 < len(units):
            s_sc[(u + 1) % 2] = scores(u + 1)
        cols = [s_sc[u % 2, :, cb * LANE:(cb + 1) * LANE] for cb in range(n_col)]
        mx = cols[0]
        for x in cols[1:]:
            mx = jnp.maximum(mx, x)
        m_prev = m_sc[i]
        m_new = jnp.maximum(m_prev, jnp.max(mx, axis=1, keepdims=True))
        alpha = jnp.exp2(m_prev - m_new)
        p = jnp.concatenate([jnp.exp2(x - m_new).astype(BF16) for x in cols], axis=1)
        v = kv_ref[0, c * tkc:(c + 1) * tkc, v_off[i]:v_off[i] + LANE]
        acc_sc[i] = alpha * acc_sc[i] + jnp.dot(p, v, preferred_element_type=F32)
        m_sc[i] = m_new

    @pl.when(j == pl.num_programs(3) - 1)
    def _():
        o0 = acc_sc[0] / acc_sc[0, :, _SUM_LANE[0]:_SUM_LANE[0] + 1]
        o1 = acc_sc[1] / acc_sc[1, :, _SUM_LANE[1]:_SUM_LANE[1] + 1]
        low = lax.broadcasted_iota(jnp.int32, o0.shape, 1) < HALF
        o_ref[0] = jnp.where(low, o0, o1).astype(o_ref.dtype)


ATTN_TQ = 1024
ATTN_TK = 4096
ATTN_TKC = 512


def _attention(q, kv, *, kvw_block, kv_block_of_pair, k_off, v_off):
    B, S, _ = q.shape
    n_pairs = 3
    tq = _row_tile(S, ATTN_TQ)
    tk = _row_tile(S, ATTN_TK)
    tkc = _row_tile(tk, ATTN_TKC)
    kern = functools.partial(_attn_kernel, k_off=k_off, v_off=v_off, tkc=tkc)
    return pl.pallas_call(
        kern,
        grid=(B, n_pairs, S // tq, S // tk),
        in_specs=[pl.BlockSpec((1, tq, 2 * LANE), lambda b, p, i, j: (b, i, p)),
                  pl.BlockSpec((1, tk, kvw_block), lambda b, p, i, j: (b, j, kv_block_of_pair(p)))],
        out_specs=pl.BlockSpec((1, tq, LANE), lambda b, p, i, j: (b, i, p)),
        out_shape=jax.ShapeDtypeStruct((B, S, n_pairs * LANE), BF16),
        scratch_shapes=[pltpu.VMEM((2, tq, LANE), F32),
                        pltpu.VMEM((2, tq, LANE), F32),
                        pltpu.VMEM((2, tq, tkc), F32)],
        compiler_params=_cparams("parallel", "parallel", "parallel", "arbitrary"),
    )(q, kv)


def _mla_attention(qa, kva):
    return _attention(qa, kva, kvw_block=4 * LANE, kv_block_of_pair=lambda p: p,
                      k_off=(LANE, 3 * LANE), v_off=(0, 2 * LANE))


def _gqa_attention(qg, kvg):
    return _attention(qg, kvg, kvw_block=3 * LANE, kv_block_of_pair=lambda p: 0,
                      k_off=(0, 0), v_off=(LANE, 2 * LANE))


_HALO = 16


def _conv_kernel(prev_ref, cur_ref, next_ref, w_ref, b_ref, g_ref, beta_ref, o_ref, xe_sc, sh_sc):
    i = pl.program_id(1)
    tc = cur_ref.shape[1]
    first = i == 0
    last = i == pl.num_programs(1) - 1
    xe_sc[0:_HALO, :] = jnp.where(first, 0.0, prev_ref[0])
    xe_sc[_HALO:_HALO + tc, :] = cur_ref[0]
    xe_sc[_HALO + tc:_HALO + tc + _HALO, :] = jnp.where(last, 0.0, next_ref[0])
    acc = jnp.zeros((tc, CONV_CH), F32)
    span = sh_sc.shape[0]
    for shift in range(SUBLANE):
        taps = [t for t in range(CONV_K) if (_HALO - CONV_K // 2 + t) % SUBLANE == shift]
        if not taps:
            continue
        sh_sc[...] = xe_sc[shift:shift + span, :]
        for t in taps:
            base = _HALO - CONV_K // 2 + t - shift
            acc = acc + sh_sc[base:base + tc, :] * w_ref[t:t + 1, :]
    u = acc + b_ref[...]
    y = _ln_rows(u, g_ref[...], beta_ref[...])
    o_ref[0] = (y * jax.nn.sigmoid(y)).astype(o_ref.dtype)


def _conv_module(glu, lw):
    B, S, C = glu.shape
    tc = _row_tile(S, 512)
    nh = tc // _HALO
    n_halo_blocks = S // _HALO
    const = lambda b, i: (0, 0)
    return pl.pallas_call(
        _conv_kernel,
        grid=(B, S // tc),
        in_specs=[pl.BlockSpec((1, _HALO, C), lambda b, i: (b, jnp.maximum(i * nh - 1, 0), 0)),
                  pl.BlockSpec((1, tc, C), lambda b, i: (b, i, 0)),
                  pl.BlockSpec((1, _HALO, C), lambda b, i: (b, jnp.minimum((i + 1) * nh, n_halo_blocks - 1), 0)),
                  pl.BlockSpec((CONV_K, C), const),
                  pl.BlockSpec((1, C), const),
                  pl.BlockSpec((1, C), const),
                  pl.BlockSpec((1, C), const)],
        out_specs=pl.BlockSpec((1, tc, C), lambda b, i: (b, i, 0)),
        out_shape=jax.ShapeDtypeStruct((B, S, C), BF16),
        scratch_shapes=[pltpu.VMEM((tc + 2 * _HALO, C), F32),
                        pltpu.VMEM((tc + 2 * _HALO - SUBLANE, C), F32)],
        compiler_params=_cparams("parallel", "parallel"),
    )(glu, glu, glu, lw["conv_w"], lw["conv_b"], lw["conv_ln_g"], lw["conv_ln_b"])


def _memkv_kernel(x_ref, w_ref, o_ref):
    o_ref[...] = jnp.dot(x_ref[...].astype(BF16), w_ref[...], preferred_element_type=F32).astype(o_ref.dtype)


def _mem_kv(mem2d, w_ckv):
    n, d = mem2d.shape
    tm = _row_tile(n, 256)
    return pl.pallas_call(
        _memkv_kernel,
        grid=(n // tm,),
        in_specs=[pl.BlockSpec((tm, d), lambda i: (i, 0)),
                  pl.BlockSpec(w_ckv.shape, lambda i: (0, 0))],
        out_specs=pl.BlockSpec((tm, w_ckv.shape[1]), lambda i: (i, 0)),
        out_shape=jax.ShapeDtypeStruct((n, w_ckv.shape[1]), BF16),
        compiler_params=_cparams("parallel"),
    )(mem2d, w_ckv)


def _mixout_xattn_kernel(h_ref, oa_ref, ob_ref, oc_ref, wmix_ref, g1_ref, b1_ref,
                         k_ref, v_ref, wq_ref, wo_ref, g2_ref, b2_ref, o_ref):
    tm = h_ref.shape[1]
    groups = [slice(r * (tm // XA_ROW_GROUPS), (r + 1) * (tm // XA_ROW_GROUPS)) for r in range(XA_ROW_GROUPS)]
    mix = [jnp.dot(jnp.concatenate([oa_ref[0, r, :], ob_ref[0, r, :], oc_ref[0, r, :]], axis=1), wmix_ref[...],
                   preferred_element_type=F32) for r in groups]
    h1 = [_ln_rows(DN_ALPHA * h_ref[0, r, :] + m, g1_ref[...], b1_ref[...]) for r, m in zip(groups, mix)]
    q = [(jnp.dot(h.astype(BF16), wq_ref[...], preferred_element_type=F32) * XA_SCALE).astype(BF16) for h in h1]
    att = []
    for qg in q:
        heads = []
        for hd in range(XA_HEADS):
            sl = slice(hd * XA_HD, (hd + 1) * XA_HD)
            s = lax.dot_general(qg[:, sl], k_ref[0, :, sl], (((1,), (1,)), ((), ())),
                                preferred_element_type=F32)
            p = jnp.exp(s - jnp.max(s, axis=1, keepdims=True))
            o = jnp.dot(p.astype(BF16), v_ref[0, :, sl], preferred_element_type=F32)
            heads.append((o / jnp.sum(p, axis=1, keepdims=True)).astype(BF16))
        att.append(jnp.concatenate(heads, axis=1))
    y = [jnp.dot(o, wo_ref[...], preferred_element_type=F32) for o in att]
    for r, h, yy in zip(groups, h1, y):
        o_ref[0, r, :] = _ln_rows(DN_ALPHA * h + yy, g2_ref[...], b2_ref[...])


def _mixout_cross_attention(h, oa, ob, oc, kvmem, lw):
    B, S, D = h.shape
    M = kvmem.shape[1]
    tm = _row_tile(S, 1024)
    const = lambda b, i: (0, 0)
    row = lambda b, i: (b, i, 0)
    return pl.pallas_call(
        _mixout_xattn_kernel,
        grid=(B, S // tm),
        in_specs=[pl.BlockSpec((1, tm, D), row),
                  pl.BlockSpec((1, tm, oa.shape[2]), row),
                  pl.BlockSpec((1, tm, ob.shape[2]), row),
                  pl.BlockSpec((1, tm, oc.shape[2]), row),
                  pl.BlockSpec(lw["w_mix"].shape, const),
                  pl.BlockSpec((1, D), const),
                  pl.BlockSpec((1, D), const),
                  pl.BlockSpec((1, M, D), lambda b, i: (b, 0, 0)),
                  pl.BlockSpec((1, M, D), lambda b, i: (b, 0, 1)),
                  pl.BlockSpec((D, D), const),
                  pl.BlockSpec((D, D), const),
                  pl.BlockSpec((1, D), const),
                  pl.BlockSpec((1, D), const)],
        out_specs=pl.BlockSpec((1, tm, D), row),
        out_shape=jax.ShapeDtypeStruct((B, S, D), F32),
        compiler_params=_cparams("parallel", "parallel"),
    )(h, oa, ob, oc, lw["w_mix"], lw["ln1_g"], lw["ln1_b"],
      kvmem, kvmem, lw["w_cq"], lw["w_co"], lw["ln2_g"], lw["ln2_b"])


MOE_TILE = 1024
MOE_SUB = 256
MOE_SUB_SHIFT = 8
MOE_SEG = 64
MOE_ALIGN = 16
MOE_EXPERTS_PER_STEP = 2


def _router_kernel(x_ref, wr_ref, bias_ref, gate_ref, pos_ref, dd_ref, cnt_ref, start_ref, sel_sc):
    T = x_ref.shape[0]
    x = x_ref[...]
    x_hi = x.astype(BF16)
    x_lo = (x - x_hi.astype(F32)).astype(BF16)
    w = wr_ref[...]
    w_hi = w.astype(BF16)
    w_lo = (w - w_hi.astype(F32)).astype(BF16)
    nt = (((1,), (1,)), ((), ()))
    both = lax.dot_general(jnp.concatenate([w_hi, w_lo], axis=0), x_hi, nt, preferred_element_type=F32)
    logits = (both[:N_EXPERTS] + both[N_EXPERTS:]
              + lax.dot_general(w_hi, x_lo, nt, preferred_element_type=F32))
    scores = jax.nn.sigmoid(logits)
    sel = scores + bias_ref[...]
    sc = [scores[e:e + 1, :] for e in range(N_EXPERTS)]
    se = [sel[e:e + 1, :] for e in range(N_EXPERTS)]

    gs = []
    for g in range(N_GROUPS):
        m = [se[g * EXPERTS_PER_GROUP + k] for k in range(EXPERTS_PER_GROUP)]
        best = None
        for a in range(EXPERTS_PER_GROUP):
            for b in range(a + 1, EXPERTS_PER_GROUP):
                pair = m[a] + m[b]
                best = pair if best is None else jnp.maximum(best, pair)
        gs.append(best)
    in_group = []
    for g in range(N_GROUPS):
        ok = None
        for o in range(N_GROUPS):
            if o == g:
                continue
            c = (gs[g] > gs[o]) if o < g else (gs[g] >= gs[o])
            ok = c if ok is None else (ok & c)
        in_group.append(ok)
    chosen = []
    for e in range(N_EXPERTS):
        g = e // EXPERTS_PER_GROUP
        beaten = jnp.zeros((1, T), jnp.int32)
        for o in range(g * EXPERTS_PER_GROUP, (g + 1) * EXPERTS_PER_GROUP):
            if o == e:
                continue
            c = (se[o] > se[e]) if o > e else (se[o] >= se[e])
            beaten = beaten + c.astype(jnp.int32)
        chosen.append(in_group[g] & (beaten < 2))
    denom = jnp.zeros((1, T), F32)
    for e in range(N_EXPERTS):
        denom = denom + jnp.where(chosen[e], sc[e], 0.0)
    for e in range(N_EXPERTS):
        gate_ref[e:e + 1, :] = jnp.where(chosen[e], sc[e] / denom, 0.0)
        sel_sc[e:e + 1, :] = chosen[e].astype(F32)

    selm = sel_sc[...].astype(BF16)
    sub_r = jnp.right_shift(lax.broadcasted_iota(jnp.int32, (T, T), 0), MOE_SUB_SHIFT)
    sub_c = jnp.right_shift(lax.broadcasted_iota(jnp.int32, (T, T), 1), MOE_SUB_SHIFT)
    same = sub_r == sub_c
    earlier = same & (lax.broadcasted_iota(jnp.int32, (T, T), 0) < lax.broadcasted_iota(jnp.int32, (T, T), 1))
    rank = jnp.dot(selm, earlier.astype(BF16), preferred_element_type=F32)
    cnt = jnp.dot(selm, same.astype(BF16), preferred_element_type=F32)
    cnt_ref[0] = cnt.astype(jnp.int32)

    seg = jnp.floor((cnt + (MOE_ALIGN - 1)) * (1.0 / MOE_ALIGN)) * MOE_ALIGN
    start = jnp.zeros((1, T), F32)
    d_lo = jnp.full((1, T), float(4 * T), F32)
    d_hi = jnp.full((1, T), -1.0, F32)
    for e in range(N_EXPERTS):
        start_ref[0, e:e + 1, :] = start.astype(jnp.int32)
        dest = jnp.where(chosen[e], rank[e:e + 1, :] + start, -1.0)
        pos_ref[e:e + 1, :] = dest
        d_lo = jnp.where(chosen[e], jnp.minimum(d_lo, dest), d_lo)
        d_hi = jnp.maximum(d_hi, dest)
        start = start + seg[e:e + 1, :]
    dd_ref[...] = jnp.zeros(dd_ref.shape, F32)
    dd_ref[0:1, :] = d_lo
    dd_ref[1:2, :] = d_hi


def _router(x, w_rt, bias):
    n, d = x.shape
    T = _row_tile(n, MOE_TILE)
    nt = n // T
    gate, pos, dd, cnt, start = pl.pallas_call(
        _router_kernel,
        grid=(nt,),
        in_specs=[pl.BlockSpec((T, d), lambda i: (i, 0)),
                  pl.BlockSpec((N_EXPERTS, d), lambda i: (0, 0)),
                  pl.BlockSpec((N_EXPERTS, 1), lambda i: (0, 0))],
        out_specs=[pl.BlockSpec((N_EXPERTS, T), lambda i: (0, i)),
                   pl.BlockSpec((N_EXPERTS, T), lambda i: (0, i)),
                   pl.BlockSpec((8, T), lambda i: (0, i)),
                   pl.BlockSpec((1, N_EXPERTS, T), lambda i: (i, 0, 0)),
                   pl.BlockSpec((1, N_EXPERTS, T), lambda i: (i, 0, 0))],
        out_shape=[jax.ShapeDtypeStruct((N_EXPERTS, n), F32),
                   jax.ShapeDtypeStruct((N_EXPERTS, n), F32),
                   jax.ShapeDtypeStruct((8, n), F32),
                   jax.ShapeDtypeStruct((nt, N_EXPERTS, T), jnp.int32),
                   jax.ShapeDtypeStruct((nt, N_EXPERTS, T), jnp.int32)],
        scratch_shapes=[pltpu.VMEM((N_EXPERTS, T), F32)],
        compiler_params=_cparams("parallel"),
    )(x, w_rt, bias)
    return gate, pos, dd, cnt[:, :, ::MOE_SUB].reshape(-1), start[:, :, ::MOE_SUB].reshape(-1)


MOE_SORTED_ROWS = 2 * MOE_SUB + N_EXPERTS * MOE_ALIGN + MOE_SEG


def _moe_kernel(cnt_ref, start_ref, h_ref, gate_ref, pos_ref, dd_ref, wg_ref, wu_ref, wd_ref, g_ref, b_ref, o_ref,
                p_sc, xs_sc, ys_sc):
    i = pl.program_id(0)
    step = pl.program_id(1)
    T = h_ref.shape[0]
    n_sub = p_sc.shape[0]
    R = p_sc.shape[1]
    half = R // 2

    @pl.when(step == 0)
    def _():
        for s in range(n_sub):
            tok = slice(s * MOE_SUB, (s + 1) * MOE_SUB)
            d_lo = dd_ref[0:1, tok].astype(jnp.int32)
            d_hi = dd_ref[1:2, tok].astype(jnp.int32)
            xb = h_ref[tok, :].astype(BF16)
            for r0 in (0, half):
                row = lax.broadcasted_iota(jnp.int32, (half, MOE_SUB), 0) + r0
                onehot = ((row == d_lo) | (row == d_hi)).astype(BF16)
                p_sc[s, r0:r0 + half, :] = onehot
                xs_sc[s, r0:r0 + half, :] = jnp.dot(onehot, xb, preferred_element_type=F32).astype(BF16)
        ys_sc[...] = jnp.zeros(ys_sc.shape, BF16)

    slot = lax.broadcasted_iota(jnp.int32, (MOE_SEG, MOE_SUB), 0)
    for k in range(MOE_EXPERTS_PER_STEP):
        e = step * MOE_EXPERTS_PER_STEP + k
        flat = (i * N_EXPERTS + e) * n_sub
        n_rows = [cnt_ref[flat + s] for s in range(n_sub)]
        seg0 = [start_ref[flat + s] for s in range(n_sub)]
        most = n_rows[0]
        for c in n_rows[1:]:
            most = jnp.maximum(most, c)
        n_blocks = (most + MOE_SEG - 1) // MOE_SEG
        pos_row = pos_ref[pl.ds(e, 1), :].astype(jnp.int32)
        gate_row = gate_ref[pl.ds(e, 1), :]

        def block(jb, carry, k=k, n_rows=n_rows, seg0=seg0, pos_row=pos_row, gate_row=gate_row):
            base = [pl.multiple_of(jnp.minimum(seg0[s] + jb * MOE_SEG, R - MOE_SEG), MOE_ALIGN)
                    for s in range(n_sub)]
            dest = [pl.multiple_of(jnp.where(jb * MOE_SEG < n_rows[s], base[s], R), MOE_ALIGN)
                    for s in range(n_sub)]
            xg = jnp.concatenate([xs_sc[s, pl.ds(base[s], MOE_SEG), :] for s in range(n_sub)], axis=0)
            hg = jnp.dot(xg, wg_ref[k], preferred_element_type=F32)
            hu = jnp.dot(xg, wu_ref[k], preferred_element_type=F32)
            act = (hg * jax.nn.sigmoid(hg) * hu).astype(BF16)
            y = jnp.dot(act, wd_ref[k], preferred_element_type=F32)
            for s in range(n_sub):
                tok = slice(s * MOE_SUB, (s + 1) * MOE_SUB)
                hit = (pos_row[:, tok] - base[s]) == slot
                g_rows = jnp.sum(jnp.where(hit, gate_row[:, tok], 0.0), axis=1, keepdims=True)
                ys_sc[s, pl.ds(dest[s], MOE_SEG), :] = (y[s * MOE_SEG:(s + 1) * MOE_SEG, :] * g_rows).astype(BF16)
            return carry

        lax.fori_loop(0, n_blocks, block, 0)

    @pl.when(step == N_EXPERTS // MOE_EXPERTS_PER_STEP - 1)
    def _():
        for s in range(n_sub):
            tok = slice(s * MOE_SUB, (s + 1) * MOE_SUB)
            moe = lax.dot_general(p_sc[s], ys_sc[s, 0:R, :], (((0,), (0,)), ((), ())), preferred_element_type=F32)
            o_ref[tok, :] = _ln_rows(DN_ALPHA * h_ref[tok, :] + moe, g_ref[...], b_ref[...])


def _moe(h, gate, pos, dd, cnt, start, lw):
    n, d = h.shape
    T = _row_tile(n, MOE_TILE)
    n_sub = T // MOE_SUB
    R = MOE_SORTED_ROWS
    f = D_FF_EXPERT
    ne = MOE_EXPERTS_PER_STEP
    grid_spec = pltpu.PrefetchScalarGridSpec(
        num_scalar_prefetch=2,
        grid=(n // T, N_EXPERTS // ne),
        in_specs=[pl.BlockSpec((T, d), lambda i, e, c, s: (i, 0)),
                  pl.BlockSpec((N_EXPERTS, T), lambda i, e, c, s: (0, i)),
                  pl.BlockSpec((N_EXPERTS, T), lambda i, e, c, s: (0, i)),
                  pl.BlockSpec((8, T), lambda i, e, c, s: (0, i)),
                  pl.BlockSpec((ne, d, f), lambda i, e, c, s: (e, 0, 0)),
                  pl.BlockSpec((ne, d, f), lambda i, e, c, s: (e, 0, 0)),
                  pl.BlockSpec((ne, f, d), lambda i, e, c, s: (e, 0, 0)),
                  pl.BlockSpec((1, d), lambda i, e, c, s: (0, 0)),
                  pl.BlockSpec((1, d), lambda i, e, c, s: (0, 0))],
        out_specs=pl.BlockSpec((T, d), lambda i, e, c, s: (i, 0)),
        scratch_shapes=[pltpu.VMEM((n_sub, R, MOE_SUB), BF16), pltpu.VMEM((n_sub, R, d), BF16),
                        pltpu.VMEM((n_sub, R + MOE_SEG, d), BF16)],
    )
    return pl.pallas_call(
        _moe_kernel,
        grid_spec=grid_spec,
        out_shape=jax.ShapeDtypeStruct((n, d), F32),
        compiler_params=_cparams("parallel", "arbitrary"),
    )(cnt, start, h, gate, pos, dd, lw["w_gate"], lw["w_up"], lw["w_down"], lw["ln3_g"], lw["ln3_b"])


def _prep_layer(l, w):
    ext_idx, ext_sgn = _w_ext_index()
    uq_idx, uq_sgn = _w_uq_index()
    mla_w = MLA_HEADS * MLA_V
    gq = w["gqa_q_norm"][l]
    gk = w["gqa_k_norm"][l]
    row = lambda v: v.reshape(1, -1).astype(F32)
    return {
        "w_ext": _take_cols(w["w_in"][l], ext_idx, ext_sgn).astype(BF16),
        "w_uq": _take_cols(w["mla_w_uq"][l], uq_idx, uq_sgn).astype(BF16),
        "w_ukv": _take_cols(w["mla_w_ukv"][l], _w_ukv_index()).astype(BF16),
        "qn": row(w["mla_q_norm"][l]),
        "kvn": row(w["mla_kv_norm"][l]),
        "gq": row(jnp.tile(gq, 2)),
        "gqp": row(jnp.tile(_perm_gain(gq, GQA_HD // 4), 2)),
        "gk": row(jnp.tile(gk, 2)),
        "gkp": row(jnp.tile(_perm_gain(gk, GQA_HD // 4), 2)),
        "conv_w": w["conv_w"][l].astype(F32),
        "conv_b": row(w["conv_b"][l]),
        "conv_ln_g": row(w["conv_ln_g"][l]),
        "conv_ln_b": row(w["conv_ln_b"][l]),
        "w_mix": jnp.concatenate([w["w_o"][l][:mla_w + CONV_CH],
                                  w["w_o"][l][mla_w + CONV_CH:][_w_oc_index()]], axis=0).astype(BF16),
        "ln1_g": row(w["ln1_g"][l]), "ln1_b": row(w["ln1_b"][l]),
        "w_cq": w["w_cq"][l].astype(BF16),
        "w_ckv": w["w_ckv"][l].astype(BF16),
        "w_co": w["w_co"][l].astype(BF16),
        "ln2_g": row(w["ln2_g"][l]), "ln2_b": row(w["ln2_b"][l]),
        "w_gate": w["w_gate"][l].astype(BF16),
        "w_up": w["w_up"][l].astype(BF16),
        "w_down": w["w_down"][l].astype(BF16),
        "ln3_g": row(w["ln3_g"][l]), "ln3_b": row(w["ln3_b"][l]),
    }


def _trunk(x, mem, w, layers):
    B, S, D = x.shape
    n = B * S
    tab = _rope_tables(S)
    w_rt = w["w_router"].T.astype(F32)
    r_bias = w["router_bias"].reshape(N_EXPERTS, 1).astype(F32)
    h = _layer_norm_rows(x.reshape(n, D), w["ln_in_g"], w["ln_in_b"])
    mem2d = mem.reshape(B * mem.shape[1], D)
    for lw in layers:
        qa, kva, glu, qg, kvg = _mixer_pre(h.reshape(B, S, D), tab, lw)
        oa = _mla_attention(qa, kva)
        ob = _conv_module(glu, lw)
        oc = _gqa_attention(qg, kvg)
        kvmem = _mem_kv(mem2d, lw["w_ckv"]).reshape(B, mem.shape[1], 2 * D)
        h = _mixout_cross_attention(h.reshape(B, S, D), oa, ob, oc, kvmem, lw).reshape(n, D)
        gate, pos, dd, cnt, start = _router(h, w_rt, r_bias)
        h = _moe(h, gate, pos, dd, cnt, start, lw)
    return h.reshape(B, S, D)


def kernel(x_prompt, x_sample, mem_prompt, mem_sample, ln_in_g, ln_in_b, w_in, mla_q_norm, mla_w_uq,
           mla_kv_norm, mla_w_ukv, conv_w, conv_b, conv_ln_g, conv_ln_b, gqa_q_norm, gqa_k_norm, w_o,
           ln1_g, ln1_b, w_cq, w_ckv, w_co, ln2_g, ln2_b, w_router, router_bias, w_gate, w_up, w_down,
           ln3_g, ln3_b):
    w = dict(ln_in_g=ln_in_g, ln_in_b=ln_in_b, w_in=w_in, mla_q_norm=mla_q_norm, mla_w_uq=mla_w_uq,
             mla_kv_norm=mla_kv_norm, mla_w_ukv=mla_w_ukv, conv_w=conv_w, conv_b=conv_b,
             conv_ln_g=conv_ln_g, conv_ln_b=conv_ln_b, gqa_q_norm=gqa_q_norm, gqa_k_norm=gqa_k_norm,
             w_o=w_o, ln1_g=ln1_g, ln1_b=ln1_b, w_cq=w_cq, w_ckv=w_ckv, w_co=w_co, ln2_g=ln2_g,
             ln2_b=ln2_b, w_router=w_router, router_bias=router_bias, w_gate=w_gate, w_up=w_up,
             w_down=w_down, ln3_g=ln3_g, ln3_b=ln3_b)
    layers = [_prep_layer(l, w) for l in range(w_in.shape[0])]
    y_prompt = _trunk(x_prompt, mem_prompt, w, layers)
    y_sample = _trunk(x_sample, mem_sample, w, layers)
    return (y_prompt, y_sample)
```

```python
import functools

import numpy as np
import jax
import jax.numpy as jnp
from jax import lax
from jax.experimental import pallas as pl
from jax.experimental.pallas import tpu as pltpu

F32 = jnp.float32
BF16 = jnp.bfloat16

D_MODEL = 1024
DEPTH = 4
GRID_W = 64
ROPE_THETA = 10000.0
LN_EPS = 1e-5
RMS_EPS = 1e-6
MLA_HEADS = 6
MLA_Q_LORA = 256
MLA_KV_LORA = 128
MLA_NOPE = 64
MLA_ROPE = 32
MLA_V = 64
CONV_CH = 256
CONV_K = 31
GQA_HEADS = 6
GQA_KV_HEADS = 2
GQA_HD = 64
SPLITS = (MLA_Q_LORA, MLA_KV_LORA, MLA_ROPE, 2 * CONV_CH,
          GQA_HEADS * GQA_HD, GQA_KV_HEADS * GQA_HD, GQA_KV_HEADS * GQA_HD)
XA_HEADS = 4
XA_HD = D_MODEL // XA_HEADS
N_EXPERTS = 16
N_GROUPS = 4
EXPERTS_PER_GROUP = N_EXPERTS // N_GROUPS
D_FF_EXPERT = 512
DN_ALPHA = (2 * DEPTH) ** 0.25
LOG2E = 1.4426950408889634
MLA_SCALE = (MLA_NOPE + MLA_ROPE) ** -0.5 * LOG2E
GQA_SCALE = GQA_HD ** -0.5 * LOG2E
XA_SCALE = XA_HD ** -0.5
XA_ROW_GROUPS = 4

LANE = 128
HALF = LANE // 2
SUBLANE = 8
VMEM_LIMIT = 56 * 1024 * 1024

_OFF_CQ = 0
_OFF_CKV = 256
_OFF_KR = 384
_OFF_KRR = 512
_OFF_GA = 640
_OFF_GG = 896
_OFF_QC = 1152
_OFF_QCR = 1536
_OFF_KC = 1920
_OFF_KCR = 2048
_OFF_VC = 2176
_W_EXT = 2432
_GQA_BLOCK_HEADS = (0, 3, 1, 4, 2, 5)
_SUM_LANE = (HALF, 0)


def _cparams(*sem):
    return pltpu.CompilerParams(dimension_semantics=sem, vmem_limit_bytes=VMEM_LIMIT)


def _row_tile(n, want):
    t = min(n, want)
    while n % t:
        t //= 2
    return t


def _ln_rows(x, g, b):
    mu = jnp.mean(x, axis=-1, keepdims=True)
    xc = x - mu
    var = jnp.mean(xc * xc, axis=-1, keepdims=True)
    return xc * lax.rsqrt(var + LN_EPS) * g + b


def _take_cols(w, idx, sign=None):
    w_aug = jnp.concatenate([w, jnp.zeros((w.shape[0], 1), w.dtype)], axis=1)
    idx = np.where(idx < 0, w.shape[1], idx)
    out = w_aug[:, idx]
    if sign is not None:
        out = out * jnp.asarray(sign, w.dtype)[None, :]
    return out


def _rot_src(d, half):
    r = d % (2 * half)
    base = d - r
    if r < half:
        return base + r + half, -1.0
    return base + r - half, 1.0


def _w_ext_index():
    cut = np.concatenate([[0], np.cumsum(SPLITS)])
    c_cq, c_ckv, c_kr, c_glu, c_qc, c_kc, c_vc = [int(c) for c in cut[:-1]]
    idx = -np.ones((_W_EXT,), np.int64)
    sgn = np.ones((_W_EXT,), np.float32)
    idx[_OFF_CQ:_OFF_CQ + 256] = c_cq + np.arange(256)
    idx[_OFF_CKV:_OFF_CKV + 128] = c_ckv + np.arange(128)
    for d in range(MLA_ROPE):
        idx[_OFF_KR + MLA_NOPE + d] = c_kr + d
        s, sg = _rot_src(d, MLA_ROPE // 2)
        idx[_OFF_KRR + MLA_NOPE + d] = c_kr + s
        sgn[_OFF_KRR + MLA_NOPE + d] = sg
    idx[_OFF_GA:_OFF_GA + 256] = c_glu + np.arange(256)
    idx[_OFF_GG:_OFF_GG + 256] = c_glu + 256 + np.arange(256)
    for blk, hq in enumerate(_GQA_BLOCK_HEADS):
        col0 = (blk // 2) * LANE + (blk % 2) * HALF
        for d in range(GQA_HD):
            idx[_OFF_QC + col0 + d] = c_qc + hq * GQA_HD + d
            s, sg = _rot_src(d, GQA_HD // 4)
            idx[_OFF_QCR + col0 + d] = c_qc + hq * GQA_HD + s
            sgn[_OFF_QCR + col0 + d] = sg
    for c in range(GQA_KV_HEADS * GQA_HD):
        hk, d = divmod(c, GQA_HD)
        idx[_OFF_KC + c] = c_kc + c
        s, sg = _rot_src(d, GQA_HD // 4)
        idx[_OFF_KCR + c] = c_kc + hk * GQA_HD + s
        sgn[_OFF_KCR + c] = sg
    idx[_OFF_VC:_OFF_VC + HALF] = c_vc + np.arange(HALF)
    idx[_OFF_VC + LANE + HALF:_OFF_VC + 2 * LANE] = c_vc + HALF + np.arange(HALF)
    return idx, sgn


def _w_uq_index():
    per = MLA_NOPE + MLA_ROPE
    idx = -np.ones((2 * MLA_HEADS * LANE,), np.int64)
    sgn = np.ones(idx.shape, np.float32)
    for h in range(MLA_HEADS):
        idx[h * LANE:h * LANE + per] = h * per + np.arange(per)
        for d in range(MLA_ROPE):
            s, sg = _rot_src(d, MLA_ROPE // 2)
            idx[(MLA_HEADS + h) * LANE + MLA_NOPE + d] = h * per + MLA_NOPE + s
            sgn[(MLA_HEADS + h) * LANE + MLA_NOPE + d] = sg
    return idx, sgn


def _w_ukv_index():
    per = MLA_NOPE + MLA_V
    idx = -np.ones((MLA_HEADS * 2 * LANE,), np.int64)
    for h in range(MLA_HEADS):
        v0 = h * 2 * LANE + (0 if h % 2 == 0 else HALF)
        idx[v0:v0 + MLA_V] = h * per + MLA_NOPE + np.arange(MLA_V)
        idx[h * 2 * LANE + LANE:h * 2 * LANE + LANE + MLA_NOPE] = h * per + np.arange(MLA_NOPE)
    return idx


def _w_oc_index():
    idx = np.zeros((GQA_HEADS * GQA_HD,), np.int64)
    for blk, hq in enumerate(_GQA_BLOCK_HEADS):
        c0 = (blk // 2) * LANE + (blk % 2) * HALF
        idx[c0:c0 + GQA_HD] = hq * GQA_HD + np.arange(GQA_HD)
    return idx


def _perm_gain(g, half):
    src = np.array([_rot_src(d, half)[0] for d in range(g.shape[-1])])
    return g[src]


def _rope_tables(S):
    assert MLA_ROPE == GQA_HD // 2
    inv = 1.0 / (ROPE_THETA ** (jnp.arange(0, MLA_ROPE, 2, dtype=F32) / MLA_ROPE))
    z = lambda k: jnp.zeros((k,), F32)
    per_head = [jnp.concatenate([inv, inv, z(GQA_HD // 2)]), jnp.concatenate([z(GQA_HD // 2), inv, inv])]
    f_pos = jnp.concatenate([z(MLA_NOPE), inv, inv, z(LANE - MLA_NOPE - MLA_ROPE), z(LANE)])
    f_row = jnp.concatenate([z(LANE), per_head[0], per_head[0]])
    f_col = jnp.concatenate([z(LANE), per_head[1], per_head[1]])
    t = jnp.arange(S, dtype=jnp.int32)
    pos = t.astype(F32)[:, None]
    row = (t // GRID_W).astype(F32)[:, None]
    col = (t % GRID_W).astype(F32)[:, None]
    ang = pos * f_pos[None, :] + row * f_row[None, :] + col * f_col[None, :]
    return jnp.cos(ang), jnp.sin(ang)


def _ln_kernel(x_ref, g_ref, b_ref, o_ref):
    o_ref[...] = _ln_rows(x_ref[...], g_ref[...], b_ref[...])


def _layer_norm_rows(x, g, b):
    n, d = x.shape
    tm = _row_tile(n, 1024)
    return pl.pallas_call(
        _ln_kernel,
        grid=(n // tm,),
        in_specs=[pl.BlockSpec((tm, d), lambda i: (i, 0)),
                  pl.BlockSpec((1, d), lambda i: (0, 0)),
                  pl.BlockSpec((1, d), lambda i: (0, 0))],
        out_specs=pl.BlockSpec((tm, d), lambda i: (i, 0)),
        out_shape=jax.ShapeDtypeStruct((n, d), F32),
        compiler_params=_cparams("parallel"),
    )(x, g.reshape(1, d), b.reshape(1, d))


def _rms_scale(ss, width):
    return lax.rsqrt(ss * (1.0 / width) + RMS_EPS)


def _pre_kernel(h_ref, cos_ref, sin_ref, wext_ref, wuq_ref, wukv_ref, qn_ref, kvn_ref,
                gq_ref, gqp_ref, gk_ref, gkp_ref,
                qa_ref, kva_ref, glu_ref, qg_ref, kvg_ref):
    hb = h_ref[0].astype(BF16)
    z = jnp.dot(hb, wext_ref[...], preferred_element_type=F32)
    cm = cos_ref[:, 0:LANE]
    cg = cos_ref[:, LANE:2 * LANE]
    sm = sin_ref[:, 0:LANE]
    sg = sin_ref[:, LANE:2 * LANE]

    cq = z[:, _OFF_CQ:_OFF_CQ + MLA_Q_LORA]
    cqn = cq * _rms_scale(jnp.sum(cq * cq, axis=-1, keepdims=True), MLA_Q_LORA) * qn_ref[...]
    a = jnp.dot(cqn.astype(BF16), wuq_ref[...], preferred_element_type=F32)
    rot0 = MLA_HEADS * LANE
    for h in range(MLA_HEADS):
        roped = a[:, h * LANE:(h + 1) * LANE] * cm + a[:, rot0 + h * LANE:rot0 + (h + 1) * LANE] * sm
        qa_ref[0, :, h * LANE:(h + 1) * LANE] = (roped * MLA_SCALE).astype(BF16)

    ckv = z[:, _OFF_CKV:_OFF_CKV + MLA_KV_LORA]
    ckvn = ckv * _rms_scale(jnp.sum(ckv * ckv, axis=-1, keepdims=True), MLA_KV_LORA) * kvn_ref[...]
    kvm = jnp.dot(ckvn.astype(BF16), wukv_ref[...], preferred_element_type=F32)
    kr = z[:, _OFF_KR:_OFF_KR + LANE] * cm + z[:, _OFF_KRR:_OFF_KRR + LANE] * sm
    lane = lax.broadcasted_iota(jnp.int32, (1, LANE), 1)
    one_at = [jnp.where(lane == L, 1.0, 0.0) for L in _SUM_LANE]
    for h in range(MLA_HEADS):
        c0 = h * 2 * LANE
        kva_ref[0, :, c0:c0 + LANE] = (kvm[:, c0:c0 + LANE] + one_at[h % 2]).astype(BF16)
        kva_ref[0, :, c0 + LANE:c0 + 2 * LANE] = (kvm[:, c0 + LANE:c0 + 2 * LANE] + kr).astype(BF16)

    glu_ref[0] = z[:, _OFF_GA:_OFF_GA + CONV_CH] * jax.nn.sigmoid(z[:, _OFF_GG:_OFF_GG + CONV_CH])

    low = lax.broadcasted_iota(jnp.int32, (h_ref.shape[1], LANE), 1) < HALF

    def norm_rope(x, x_rot, g, g_rot):
        sq = x * x
        r0 = _rms_scale(jnp.sum(jnp.where(low, sq, 0.0), axis=-1, keepdims=True), GQA_HD)
        r1 = _rms_scale(jnp.sum(jnp.where(low, 0.0, sq), axis=-1, keepdims=True), GQA_HD)
        return (x * g * cg + x_rot * g_rot * sg) * jnp.where(low, r0, r1)

    gq = gq_ref[...]
    gqp = gqp_ref[...]
    for p in range(GQA_HEADS // 2):
        qn = norm_rope(z[:, _OFF_QC + p * LANE:_OFF_QC + (p + 1) * LANE],
                       z[:, _OFF_QCR + p * LANE:_OFF_QCR + (p + 1) * LANE], gq, gqp) * GQA_SCALE
        qg_ref[0, :, 2 * p * LANE:(2 * p + 1) * LANE] = jnp.where(low, qn, 0.0).astype(BF16)
        qg_ref[0, :, (2 * p + 1) * LANE:(2 * p + 2) * LANE] = jnp.where(low, 0.0, qn).astype(BF16)

    kvg_ref[0, :, 0:LANE] = norm_rope(z[:, _OFF_KC:_OFF_KC + LANE], z[:, _OFF_KCR:_OFF_KCR + LANE],
                                      gk_ref[...], gkp_ref[...]).astype(BF16)
    kvg_ref[0, :, LANE:2 * LANE] = (z[:, _OFF_VC:_OFF_VC + LANE] + one_at[0]).astype(BF16)
    kvg_ref[0, :, 2 * LANE:3 * LANE] = (z[:, _OFF_VC + LANE:_OFF_VC + 2 * LANE] + one_at[1]).astype(BF16)


def _mixer_pre(h, tab, lw):
    B, S, D = h.shape
    tm = _row_tile(S, 512)
    const = lambda b, i: (0, 0)
    row3 = lambda b, i: (b, i, 0)
    outs = [(MLA_HEADS * LANE, BF16), (MLA_HEADS * 2 * LANE, BF16), (CONV_CH, F32),
            (GQA_HEADS * LANE, BF16), (3 * LANE, BF16)]
    return pl.pallas_call(
        _pre_kernel,
        grid=(B, S // tm),
        in_specs=[pl.BlockSpec((1, tm, D), row3),
                  pl.BlockSpec((tm, 2 * LANE), lambda b, i: (i, 0)),
                  pl.BlockSpec((tm, 2 * LANE), lambda b, i: (i, 0)),
                  pl.BlockSpec(lw["w_ext"].shape, const),
                  pl.BlockSpec(lw["w_uq"].shape, const),
                  pl.BlockSpec(lw["w_ukv"].shape, const),
                  pl.BlockSpec((1, MLA_Q_LORA), const),
                  pl.BlockSpec((1, MLA_KV_LORA), const),
                  pl.BlockSpec((1, LANE), const),
                  pl.BlockSpec((1, LANE), const),
                  pl.BlockSpec((1, LANE), const),
                  pl.BlockSpec((1, LANE), const)],
        out_specs=[pl.BlockSpec((1, tm, w), row3) for w, _ in outs],
        out_shape=[jax.ShapeDtypeStruct((B, S, w), dt) for w, dt in outs],
        compiler_params=_cparams("parallel", "parallel"),
    )(h, tab[0], tab[1], lw["w_ext"], lw["w_uq"], lw["w_ukv"], lw["qn"], lw["kvn"],
      lw["gq"], lw["gqp"], lw["gk"], lw["gkp"])


def _attn_kernel(q_ref, kv_ref, o_ref, m_sc, acc_sc, s_sc, *, k_off, v_off, tkc):
    j = pl.program_id(3)
    tk = kv_ref.shape[1]
    units = [(i, c) for c in range(tk // tkc) for i in range(2)]
    n_col = tkc // LANE

    @pl.when(j == 0)
    def _():
        m_sc[...] = jnp.full(m_sc.shape, -jnp.inf, F32)
        acc_sc[...] = jnp.zeros(acc_sc.shape, F32)

    def scores(u):
        i, c = units[u]
        q = q_ref[0, :, i * LANE:(i + 1) * LANE]
        k = kv_ref[0, c * tkc:(c + 1) * tkc, k_off[i]:k_off[i] + LANE]
        return lax.dot_general(q, k, (((1,), (1,)), ((), ())), preferred_element_type=F32)

    s_sc[0] = scores(0)
    for u, (i, c) in enumerate(units):
        if u + 1 < len(units):
            s_sc[(u + 1) % 2] = scores(u + 1)
        cols = [s_sc[u % 2, :, cb * LANE:(cb + 1) * LANE] for cb in range(n_col)]
        mx = cols[0]
        for x in cols[1:]:
            mx = jnp.maximum(mx, x)
        m_prev = m_sc[i]
        m_new = jnp.maximum(m_prev, jnp.max(mx, axis=1, keepdims=True))
        alpha = jnp.exp2(m_prev - m_new)
        p = jnp.concatenate([jnp.exp2(x - m_new).astype(BF16) for x in cols], axis=1)
        v = kv_ref[0, c * tkc:(c + 1) * tkc, v_off[i]:v_off[i] + LANE]
        acc_sc[i] = alpha * acc_sc[i] + jnp.dot(p, v, preferred_element_type=F32)
        m_sc[i] = m_new

    @pl.when(j == pl.num_programs(3) - 1)
    def _():
        o0 = acc_sc[0] / acc_sc[0, :, _SUM_LANE[0]:_SUM_LANE[0] + 1]
        o1 = acc_sc[1] / acc_sc[1, :, _SUM_LANE[1]:_SUM_LANE[1] + 1]
        low = lax.broadcasted_iota(jnp.int32, o0.shape, 1) < HALF
        o_ref[0] = jnp.where(low, o0, o1).astype(o_ref.dtype)


ATTN_TQ = 1024
ATTN_TK = 4096
ATTN_TKC = 512


def _attention(q, kv, *, kvw_block, kv_block_of_pair, k_off, v_off):
    B, S, _ = q.shape
    n_pairs = 3
    tq = _row_tile(S, ATTN_TQ)
    tk = _row_tile(S, ATTN_TK)
    tkc = _row_tile(tk, ATTN_TKC)
    kern = functools.partial(_attn_kernel, k_off=k_off, v_off=v_off, tkc=tkc)
    return pl.pallas_call(
        kern,
        grid=(B, n_pairs, S // tq, S // tk),
        in_specs=[pl.BlockSpec((1, tq, 2 * LANE), lambda b, p, i, j: (b, i, p)),
                  pl.BlockSpec((1, tk, kvw_block), lambda b, p, i, j: (b, j, kv_block_of_pair(p)))],
        out_specs=pl.BlockSpec((1, tq, LANE), lambda b, p, i, j: (b, i, p)),
        out_shape=jax.ShapeDtypeStruct((B, S, n_pairs * LANE), BF16),
        scratch_shapes=[pltpu.VMEM((2, tq, LANE), F32),
                        pltpu.VMEM((2, tq, LANE), F32),
                        pltpu.VMEM((2, tq, tkc), F32)],
        compiler_params=_cparams("parallel", "parallel", "parallel", "arbitrary"),
    )(q, kv)


def _mla_attention(qa, kva):
    return _attention(qa, kva, kvw_block=4 * LANE, kv_block_of_pair=lambda p: p,
                      k_off=(LANE, 3 * LANE), v_off=(0, 2 * LANE))


def _gqa_attention(qg, kvg):
    return _attention(qg, kvg, kvw_block=3 * LANE, kv_block_of_pair=lambda p: 0,
                      k_off=(0, 0), v_off=(LANE, 2 * LANE))


_HALO = 16


def _conv_kernel(prev_ref, cur_ref, next_ref, w_ref, b_ref, g_ref, beta_ref, o_ref, xe_sc, sh_sc):
    i = pl.program_id(1)
    tc = cur_ref.shape[1]
    first = i == 0
    last = i == pl.num_programs(1) - 1
    xe_sc[0:_HALO, :] = jnp.where(first, 0.0, prev_ref[0])
    xe_sc[_HALO:_HALO + tc, :] = cur_ref[0]
    xe_sc[_HALO + tc:_HALO + tc + _HALO, :] = jnp.where(last, 0.0, next_ref[0])
    acc = jnp.zeros((tc, CONV_CH), F32)
    span = sh_sc.shape[0]
    for shift in range(SUBLANE):
        taps = [t for t in range(CONV_K) if (_HALO - CONV_K // 2 + t) % SUBLANE == shift]
        if not taps:
            continue
        sh_sc[...] = xe_sc[shift:shift + span, :]
        for t in taps:
            base = _HALO - CONV_K // 2 + t - shift
            acc = acc + sh_sc[base:base + tc, :] * w_ref[t:t + 1, :]
    u = acc + b_ref[...]
    y = _ln_rows(u, g_ref[...], beta_ref[...])
    o_ref[0] = (y * jax.nn.sigmoid(y)).astype(o_ref.dtype)


def _conv_module(glu, lw):
    B, S, C = glu.shape
    tc = _row_tile(S, 512)
    nh = tc // _HALO
    n_halo_blocks = S // _HALO
    const = lambda b, i: (0, 0)
    return pl.pallas_call(
        _conv_kernel,
        grid=(B, S // tc),
        in_specs=[pl.BlockSpec((1, _HALO, C), lambda b, i: (b, jnp.maximum(i * nh - 1, 0), 0)),
                  pl.BlockSpec((1, tc, C), lambda b, i: (b, i, 0)),
                  pl.BlockSpec((1, _HALO, C), lambda b, i: (b, jnp.minimum((i + 1) * nh, n_halo_blocks - 1), 0)),
                  pl.BlockSpec((CONV_K, C), const),
                  pl.BlockSpec((1, C), const),
                  pl.BlockSpec((1, C), const),
                  pl.BlockSpec((1, C), const)],
        out_specs=pl.BlockSpec((1, tc, C), lambda b, i: (b, i, 0)),
        out_shape=jax.ShapeDtypeStruct((B, S, C), BF16),
        scratch_shapes=[pltpu.VMEM((tc + 2 * _HALO, C), F32),
                        pltpu.VMEM((tc + 2 * _HALO - SUBLANE, C), F32)],
        compiler_params=_cparams("parallel", "parallel"),
    )(glu, glu, glu, lw["conv_w"], lw["conv_b"], lw["conv_ln_g"], lw["conv_ln_b"])


def _memkv_kernel(x_ref, w_ref, o_ref):
    o_ref[...] = jnp.dot(x_ref[...].astype(BF16), w_ref[...], preferred_element_type=F32).astype(o_ref.dtype)


def _mem_kv(mem2d, w_ckv):
    n, d = mem2d.shape
    tm = _row_tile(n, 256)
    return pl.pallas_call(
        _memkv_kernel,
        grid=(n // tm,),
        in_specs=[pl.BlockSpec((tm, d), lambda i: (i, 0)),
                  pl.BlockSpec(w_ckv.shape, lambda i: (0, 0))],
        out_specs=pl.BlockSpec((tm, w_ckv.shape[1]), lambda i: (i, 0)),
        out_shape=jax.ShapeDtypeStruct((n, w_ckv.shape[1]), BF16),
        compiler_params=_cparams("parallel"),
    )(mem2d, w_ckv)


def _mixout_xattn_kernel(h_ref, oa_ref, ob_ref, oc_ref, wmix_ref, g1_ref, b1_ref,
                         k_ref, v_ref, wq_ref, wo_ref, g2_ref, b2_ref, o_ref):
    tm = h_ref.shape[1]
    groups = [slice(r * (tm // XA_ROW_GROUPS), (r + 1) * (tm // XA_ROW_GROUPS)) for r in range(XA_ROW_GROUPS)]
    mix = [jnp.dot(jnp.concatenate([oa_ref[0, r, :], ob_ref[0, r, :], oc_ref[0, r, :]], axis=1), wmix_ref[...],
                   preferred_element_type=F32) for r in groups]
    h1 = [_ln_rows(DN_ALPHA * h_ref[0, r, :] + m, g1_ref[...], b1_ref[...]) for r, m in zip(groups, mix)]
    q = [(jnp.dot(h.astype(BF16), wq_ref[...], preferred_element_type=F32) * XA_SCALE).astype(BF16) for h in h1]
    att = []
    for qg in q:
        heads = []
        for hd in range(XA_HEADS):
            sl = slice(hd * XA_HD, (hd + 1) * XA_HD)
            s = lax.dot_general(qg[:, sl], k_ref[0, :, sl], (((1,), (1,)), ((), ())),
                                preferred_element_type=F32)
            p = jnp.exp(s - jnp.max(s, axis=1, keepdims=True))
            o = jnp.dot(p.astype(BF16), v_ref[0, :, sl], preferred_element_type=F32)
            heads.append((o / jnp.sum(p, axis=1, keepdims=True)).astype(BF16))
        att.append(jnp.concatenate(heads, axis=1))
    y = [jnp.dot(o, wo_ref[...], preferred_element_type=F32) for o in att]
    for r, h, yy in zip(groups, h1, y):
        o_ref[0, r, :] = _ln_rows(DN_ALPHA * h + yy, g2_ref[...], b2_ref[...])


def _mixout_cross_attention(h, oa, ob, oc, kvmem, lw):
    B, S, D = h.shape
    M = kvmem.shape[1]
    tm = _row_tile(S, 1024)
    const = lambda b, i: (0, 0)
    row = lambda b, i: (b, i, 0)
    return pl.pallas_call(
        _mixout_xattn_kernel,
        grid=(B, S // tm),
        in_specs=[pl.BlockSpec((1, tm, D), row),
                  pl.BlockSpec((1, tm, oa.shape[2]), row),
                  pl.BlockSpec((1, tm, ob.shape[2]), row),
                  pl.BlockSpec((1, tm, oc.shape[2]), row),
                  pl.BlockSpec(lw["w_mix"].shape, const),
                  pl.BlockSpec((1, D), const),
                  pl.BlockSpec((1, D), const),
                  pl.BlockSpec((1, M, D), lambda b, i: (b, 0, 0)),
                  pl.BlockSpec((1, M, D), lambda b, i: (b, 0, 1)),
                  pl.BlockSpec((D, D), const),
                  pl.BlockSpec((D, D), const),
                  pl.BlockSpec((1, D), const),
                  pl.BlockSpec((1, D), const)],
        out_specs=pl.BlockSpec((1, tm, D), row),
        out_shape=jax.ShapeDtypeStruct((B, S, D), F32),
        compiler_params=_cparams("parallel", "parallel"),
    )(h, oa, ob, oc, lw["w_mix"], lw["ln1_g"], lw["ln1_b"],
      kvmem, kvmem, lw["w_cq"], lw["w_co"], lw["ln2_g"], lw["ln2_b"])


MOE_ROUTER_TILE = 1024
MOE_TILE = 2048
MOE_SUB = 256
MOE_SUB_SHIFT = 8
MOE_SEG = 48
MOE_ALIGN = 16
MOE_EXPERTS_PER_STEP = 1


def _router_kernel(x_ref, wr_ref, bias_ref, gate_ref, pos_ref, dd_ref, cnt_ref, start_ref, sel_sc):
    T = x_ref.shape[0]
    x = x_ref[...]
    x_hi = x.astype(BF16)
    x_lo = (x - x_hi.astype(F32)).astype(BF16)
    w = wr_ref[...]
    w_hi = w.astype(BF16)
    w_lo = (w - w_hi.astype(F32)).astype(BF16)
    nt = (((1,), (1,)), ((), ()))
    both = lax.dot_general(jnp.concatenate([w_hi, w_lo], axis=0), x_hi, nt, preferred_element_type=F32)
    logits = (both[:N_EXPERTS] + both[N_EXPERTS:]
              + lax.dot_general(w_hi, x_lo, nt, preferred_element_type=F32))
    scores = jax.nn.sigmoid(logits)
    sel = scores + bias_ref[...]
    sc = [scores[e:e + 1, :] for e in range(N_EXPERTS)]
    se = [sel[e:e + 1, :] for e in range(N_EXPERTS)]

    gs = []
    for g in range(N_GROUPS):
        m = [se[g * EXPERTS_PER_GROUP + k] for k in range(EXPERTS_PER_GROUP)]
        best = None
        for a in range(EXPERTS_PER_GROUP):
            for b in range(a + 1, EXPERTS_PER_GROUP):
                pair = m[a] + m[b]
                best = pair if best is None else jnp.maximum(best, pair)
        gs.append(best)
    in_group = []
    for g in range(N_GROUPS):
        ok = None
        for o in range(N_GROUPS):
            if o == g:
                continue
            c = (gs[g] > gs[o]) if o < g else (gs[g] >= gs[o])
            ok = c if ok is None else (ok & c)
        in_group.append(ok)
    chosen = []
    for e in range(N_EXPERTS):
        g = e // EXPERTS_PER_GROUP
        beaten = jnp.zeros((1, T), jnp.int32)
        for o in range(g * EXPERTS_PER_GROUP, (g + 1) * EXPERTS_PER_GROUP):
            if o == e:
                continue
            c = (se[o] > se[e]) if o > e else (se[o] >= se[e])
            beaten = beaten + c.astype(jnp.int32)
        chosen.append(in_group[g] & (beaten < 2))
    denom = jnp.zeros((1, T), F32)
    for e in range(N_EXPERTS):
        denom = denom + jnp.where(chosen[e], sc[e], 0.0)
    for e in range(N_EXPERTS):
        gate_ref[e:e + 1, :] = jnp.where(chosen[e], sc[e] / denom, 0.0)
        sel_sc[e:e + 1, :] = chosen[e].astype(F32)

    selm = sel_sc[...].astype(BF16)
    sub_r = jnp.right_shift(lax.broadcasted_iota(jnp.int32, (T, T), 0), MOE_SUB_SHIFT)
    sub_c = jnp.right_shift(lax.broadcasted_iota(jnp.int32, (T, T), 1), MOE_SUB_SHIFT)
    same = sub_r == sub_c
    earlier = same & (lax.broadcasted_iota(jnp.int32, (T, T), 0) < lax.broadcasted_iota(jnp.int32, (T, T), 1))
    rank = jnp.dot(selm, earlier.astype(BF16), preferred_element_type=F32)
    cnt = jnp.dot(selm, same.astype(BF16), preferred_element_type=F32)
    cnt_ref[0] = cnt.astype(jnp.int32)

    seg = jnp.floor((cnt + (MOE_ALIGN - 1)) * (1.0 / MOE_ALIGN)) * MOE_ALIGN
    start = jnp.zeros((1, T), F32)
    d_lo = jnp.full((1, T), float(4 * T), F32)
    d_hi = jnp.full((1, T), -1.0, F32)
    for e in range(N_EXPERTS):
        start_ref[0, e:e + 1, :] = start.astype(jnp.int32)
        dest = jnp.where(chosen[e], rank[e:e + 1, :] + start, -1.0)
        pos_ref[e:e + 1, :] = dest
        d_lo = jnp.where(chosen[e], jnp.minimum(d_lo, dest), d_lo)
        d_hi = jnp.maximum(d_hi, dest)
        start = start + seg[e:e + 1, :]
    dd_ref[...] = jnp.zeros(dd_ref.shape, F32)
    dd_ref[0:1, :] = d_lo
    dd_ref[1:2, :] = d_hi


def _router(x, w_rt, bias):
    n, d = x.shape
    T = _row_tile(n, MOE_ROUTER_TILE)
    nt = n // T
    gate, pos, dd, cnt, start = pl.pallas_call(
        _router_kernel,
        grid=(nt,),
        in_specs=[pl.BlockSpec((T, d), lambda i: (i, 0)),
                  pl.BlockSpec((N_EXPERTS, d), lambda i: (0, 0)),
                  pl.BlockSpec((N_EXPERTS, 1), lambda i: (0, 0))],
        out_specs=[pl.BlockSpec((N_EXPERTS, T), lambda i: (0, i)),
                   pl.BlockSpec((N_EXPERTS, T), lambda i: (0, i)),
                   pl.BlockSpec((8, T), lambda i: (0, i)),
                   pl.BlockSpec((1, N_EXPERTS, T), lambda i: (i, 0, 0)),
                   pl.BlockSpec((1, N_EXPERTS, T), lambda i: (i, 0, 0))],
        out_shape=[jax.ShapeDtypeStruct((N_EXPERTS, n), F32),
                   jax.ShapeDtypeStruct((N_EXPERTS, n), F32),
                   jax.ShapeDtypeStruct((8, n), F32),
                   jax.ShapeDtypeStruct((nt, N_EXPERTS, T), jnp.int32),
                   jax.ShapeDtypeStruct((nt, N_EXPERTS, T), jnp.int32)],
        scratch_shapes=[pltpu.VMEM((N_EXPERTS, T), F32)],
        compiler_params=_cparams("parallel"),
    )(x, w_rt, bias)
    return gate, pos, dd, cnt[:, :, ::MOE_SUB].reshape(-1), start[:, :, ::MOE_SUB].reshape(-1)


MOE_SORTED_ROWS = -(-(2 * MOE_SUB + N_EXPERTS * MOE_ALIGN + MOE_SEG) // 32) * 32


def _moe_kernel(cnt_ref, start_ref, h_ref, gate_ref, pos_ref, dd_ref, wg_ref, wu_ref, wd_ref, g_ref, b_ref, o_ref,
                p_sc, xs_sc, *, router_subs):
    i = pl.program_id(0)
    step = pl.program_id(1)
    n_sub = p_sc.shape[0]
    R = p_sc.shape[1]
    half = R // 2

    @pl.when(step == 0)
    def _():
        for s in range(n_sub):
            tok = slice(s * MOE_SUB, (s + 1) * MOE_SUB)
            d_lo = dd_ref[0:1, tok].astype(jnp.int32)
            d_hi = dd_ref[1:2, tok].astype(jnp.int32)
            xb = h_ref[tok, :].astype(BF16)
            for r0 in (0, half):
                row = lax.broadcasted_iota(jnp.int32, (half, MOE_SUB), 0) + r0
                onehot = ((row == d_lo) | (row == d_hi)).astype(BF16)
                p_sc[s, r0:r0 + half, :] = onehot
                xs_sc[s, r0:r0 + half, :] = jnp.dot(onehot, xb, preferred_element_type=F32).astype(BF16)

    slot = lax.broadcasted_iota(jnp.int32, (MOE_SEG, MOE_SUB), 0)
    for k in range(MOE_EXPERTS_PER_STEP):
        e = step * MOE_EXPERTS_PER_STEP + k
        flat = [((i * (n_sub // router_subs) + s // router_subs) * N_EXPERTS + e) * router_subs + s % router_subs
                for s in range(n_sub)]
        n_rows = [cnt_ref[f] for f in flat]
        seg0 = [start_ref[f] for f in flat]
        most = n_rows[0]
        for c in n_rows[1:]:
            most = jnp.maximum(most, c)
        n_blocks = (most + MOE_SEG - 1) // MOE_SEG
        pos_row = pos_ref[pl.ds(e, 1), :].astype(jnp.int32)
        gate_row = gate_ref[pl.ds(e, 1), :]

        def block(jb, carry, k=k, n_rows=n_rows, seg0=seg0, pos_row=pos_row, gate_row=gate_row):
            base = [pl.multiple_of(jnp.minimum(seg0[s] + jb * MOE_SEG, R - MOE_SEG), MOE_ALIGN)
                    for s in range(n_sub)]
            dest = [pl.multiple_of(jnp.where(jb * MOE_SEG < n_rows[s], base[s], R), MOE_ALIGN)
                    for s in range(n_sub)]
            xin = [xs_sc[s, pl.ds(base[s], MOE_SEG), :] for s in range(n_sub)]
            xg = jnp.concatenate(xin, axis=0)
            hg = jnp.dot(xg, wg_ref[k], preferred_element_type=F32)
            hu = jnp.dot(xg, wu_ref[k], preferred_element_type=F32)
            act = (hg * jax.nn.sigmoid(hg) * hu).astype(BF16)
            y = jnp.dot(act, wd_ref[k], preferred_element_type=F32)
            for s in range(n_sub):
                tok = slice(s * MOE_SUB, (s + 1) * MOE_SUB)
                hit = (pos_row[:, tok] - base[s]) == slot
                g_rows = jnp.sum(jnp.where(hit, gate_row[:, tok], 0.0), axis=1, keepdims=True)
                own = jnp.sum(jnp.where(hit, 1.0, 0.0), axis=1, keepdims=True) > 0.0
                out_rows = (y[s * MOE_SEG:(s + 1) * MOE_SEG, :] * g_rows).astype(BF16)
                xs_sc[s, pl.ds(dest[s], MOE_SEG), :] = jnp.where(own, out_rows, xin[s])
            return carry

        lax.fori_loop(0, n_blocks, block, 0)

    @pl.when(step == N_EXPERTS // MOE_EXPERTS_PER_STEP - 1)
    def _():
        for s in range(n_sub):
            tok = slice(s * MOE_SUB, (s + 1) * MOE_SUB)
            moe = lax.dot_general(p_sc[s], xs_sc[s, 0:R, :], (((0,), (0,)), ((), ())), preferred_element_type=F32)
            o_ref[tok, :] = _ln_rows(DN_ALPHA * h_ref[tok, :] + moe, g_ref[...], b_ref[...])


def _moe(h, gate, pos, dd, cnt, start, lw):
    n, d = h.shape
    T = _row_tile(n, MOE_TILE)
    n_sub = T // MOE_SUB
    router_subs = _row_tile(n, MOE_ROUTER_TILE) // MOE_SUB
    R = MOE_SORTED_ROWS
    f = D_FF_EXPERT
    ne = MOE_EXPERTS_PER_STEP
    grid_spec = pltpu.PrefetchScalarGridSpec(
        num_scalar_prefetch=2,
        grid=(n // T, N_EXPERTS // ne),
        in_specs=[pl.BlockSpec((T, d), lambda i, e, c, s: (i, 0)),
                  pl.BlockSpec((N_EXPERTS, T), lambda i, e, c, s: (0, i)),
                  pl.BlockSpec((N_EXPERTS, T), lambda i, e, c, s: (0, i)),
                  pl.BlockSpec((8, T), lambda i, e, c, s: (0, i)),
                  pl.BlockSpec((ne, d, f), lambda i, e, c, s: (e, 0, 0)),
                  pl.BlockSpec((ne, d, f), lambda i, e, c, s: (e, 0, 0)),
                  pl.BlockSpec((ne, f, d), lambda i, e, c, s: (e, 0, 0)),
                  pl.BlockSpec((1, d), lambda i, e, c, s: (0, 0)),
                  pl.BlockSpec((1, d), lambda i, e, c, s: (0, 0))],
        out_specs=pl.BlockSpec((T, d), lambda i, e, c, s: (i, 0), pipeline_mode=pl.Buffered(1)),
        scratch_shapes=[pltpu.VMEM((n_sub, R, MOE_SUB), BF16), pltpu.VMEM((n_sub, R + MOE_SEG, d), BF16)],
    )
    return pl.pallas_call(
        functools.partial(_moe_kernel, router_subs=router_subs),
        grid_spec=grid_spec,
        out_shape=jax.ShapeDtypeStruct((n, d), F32),
        compiler_params=_cparams("parallel", "arbitrary"),
    )(cnt, start, h, gate, pos, dd, lw["w_gate"], lw["w_up"], lw["w_down"], lw["ln3_g"], lw["ln3_b"])


def _prep_layer(l, w):
    ext_idx, ext_sgn = _w_ext_index()
    uq_idx, uq_sgn = _w_uq_index()
    mla_w = MLA_HEADS * MLA_V
    gq = w["gqa_q_norm"][l]
    gk = w["gqa_k_norm"][l]
    row = lambda v: v.reshape(1, -1).astype(F32)
    return {
        "w_ext": _take_cols(w["w_in"][l], ext_idx, ext_sgn).astype(BF16),
        "w_uq": _take_cols(w["mla_w_uq"][l], uq_idx, uq_sgn).astype(BF16),
        "w_ukv": _take_cols(w["mla_w_ukv"][l], _w_ukv_index()).astype(BF16),
        "qn": row(w["mla_q_norm"][l]),
        "kvn": row(w["mla_kv_norm"][l]),
        "gq": row(jnp.tile(gq, 2)),
        "gqp": row(jnp.tile(_perm_gain(gq, GQA_HD // 4), 2)),
        "gk": row(jnp.tile(gk, 2)),
        "gkp": row(jnp.tile(_perm_gain(gk, GQA_HD // 4), 2)),
        "conv_w": w["conv_w"][l].astype(F32),
        "conv_b": row(w["conv_b"][l]),
        "conv_ln_g": row(w["conv_ln_g"][l]),
        "conv_ln_b": row(w["conv_ln_b"][l]),
        "w_mix": jnp.concatenate([w["w_o"][l][:mla_w + CONV_CH],
                                  w["w_o"][l][mla_w + CONV_CH:][_w_oc_index()]], axis=0).astype(BF16),
        "ln1_g": row(w["ln1_g"][l]), "ln1_b": row(w["ln1_b"][l]),
        "w_cq": w["w_cq"][l].astype(BF16),
        "w_ckv": w["w_ckv"][l].astype(BF16),
        "w_co": w["w_co"][l].astype(BF16),
        "ln2_g": row(w["ln2_g"][l]), "ln2_b": row(w["ln2_b"][l]),
        "w_gate": w["w_gate"][l].astype(BF16),
        "w_up": w["w_up"][l].astype(BF16),
        "w_down": w["w_down"][l].astype(BF16),
        "ln3_g": row(w["ln3_g"][l]), "ln3_b": row(w["ln3_b"][l]),
    }


def _trunk(x, mem, w, layers):
    B, S, D = x.shape
    n = B * S
    tab = _rope_tables(S)
    w_rt = w["w_router"].T.astype(F32)
    r_bias = w["router_bias"].reshape(N_EXPERTS, 1).astype(F32)
    h = _layer_norm_rows(x.reshape(n, D), w["ln_in_g"], w["ln_in_b"])
    mem2d = mem.reshape(B * mem.shape[1], D)
    for lw in layers:
        qa, kva, glu, qg, kvg = _mixer_pre(h.reshape(B, S, D), tab, lw)
        oa = _mla_attention(qa, kva)
        ob = _conv_module(glu, lw)
        oc = _gqa_attention(qg, kvg)
        kvmem = _mem_kv(mem2d, lw["w_ckv"]).reshape(B, mem.shape[1], 2 * D)
        h = _mixout_cross_attention(h.reshape(B, S, D), oa, ob, oc, kvmem, lw).reshape(n, D)
        gate, pos, dd, cnt, start = _router(h, w_rt, r_bias)
        h = _moe(h, gate, pos, dd, cnt, start, lw)
    return h.reshape(B, S, D)


def kernel(x_prompt, x_sample, mem_prompt, mem_sample, ln_in_g, ln_in_b, w_in, mla_q_norm, mla_w_uq,
           mla_kv_norm, mla_w_ukv, conv_w, conv_b, conv_ln_g, conv_ln_b, gqa_q_norm, gqa_k_norm, w_o,
           ln1_g, ln1_b, w_cq, w_ckv, w_co, ln2_g, ln2_b, w_router, router_bias, w_gate, w_up, w_down,
           ln3_g, ln3_b):
    w = dict(ln_in_g=ln_in_g, ln_in_b=ln_in_b, w_in=w_in, mla_q_norm=mla_q_norm, mla_w_uq=mla_w_uq,
             mla_kv_norm=mla_kv_norm, mla_w_ukv=mla_w_ukv, conv_w=conv_w, conv_b=conv_b,
             conv_ln_g=conv_ln_g, conv_ln_b=conv_ln_b, gqa_q_norm=gqa_q_norm, gqa_k_norm=gqa_k_norm,
             w_o=w_o, ln1_g=ln1_g, ln1_b=ln1_b, w_cq=w_cq, w_ckv=w_ckv, w_co=w_co, ln2_g=ln2_g,
             ln2_b=ln2_b, w_router=w_router, router_bias=router_bias, w_gate=w_gate, w_up=w_up,
             w_down=w_down, ln3_g=ln3_g, ln3_b=ln3_b)
    layers = [_prep_layer(l, w) for l in range(w_in.shape[0])]
    y_prompt = _trunk(x_prompt, mem_prompt, w, layers)
    y_sample = _trunk(x_sample, mem_sample, w, layers)
    return (y_prompt, y_sample)
```

```python
import functools

import numpy as np
import jax
import jax.numpy as jnp
from jax import lax
from jax.experimental import pallas as pl
from jax.experimental.pallas import tpu as pltpu

F32 = jnp.float32
BF16 = jnp.bfloat16

D_MODEL = 1024
DEPTH = 4
GRID_W = 64
ROPE_THETA = 10000.0
LN_EPS = 1e-5
RMS_EPS = 1e-6
MLA_HEADS = 6
MLA_Q_LORA = 256
MLA_KV_LORA = 128
MLA_NOPE = 64
MLA_ROPE = 32
MLA_V = 64
CONV_CH = 256
CONV_K = 31
GQA_HEADS = 6
GQA_KV_HEADS = 2
GQA_HD = 64
SPLITS = (MLA_Q_LORA, MLA_KV_LORA, MLA_ROPE, 2 * CONV_CH,
          GQA_HEADS * GQA_HD, GQA_KV_HEADS * GQA_HD, GQA_KV_HEADS * GQA_HD)
XA_HEADS = 4
XA_HD = D_MODEL // XA_HEADS
N_EXPERTS = 16
N_GROUPS = 4
EXPERTS_PER_GROUP = N_EXPERTS // N_GROUPS
D_FF_EXPERT = 512
DN_ALPHA = (2 * DEPTH) ** 0.25
LOG2E = 1.4426950408889634
MLA_SCALE = (MLA_NOPE + MLA_ROPE) ** -0.5 * LOG2E
GQA_SCALE = GQA_HD ** -0.5 * LOG2E
XA_SCALE = XA_HD ** -0.5
XA_ROW_GROUPS = 4

LANE = 128
HALF = LANE // 2
SUBLANE = 8
VMEM_LIMIT = 62 * 1024 * 1024

_OFF_CQ = 0
_OFF_CKV = 256
_OFF_KR = 384
_OFF_KRR = 512
_OFF_GA = 640
_OFF_GG = 896
_OFF_QC = 1152
_OFF_QCR = 1536
_OFF_KC = 1920
_OFF_KCR = 2048
_OFF_VC = 2176
_W_EXT = 2432
_GQA_BLOCK_HEADS = (0, 3, 1, 4, 2, 5)
_SUM_LANE = (HALF, 0)


def _cparams(*sem):
    return pltpu.CompilerParams(dimension_semantics=sem, vmem_limit_bytes=VMEM_LIMIT)


def _row_tile(n, want):
    t = min(n, want)
    while n % t:
        t //= 2
    return t


def _ln_rows(x, g, b):
    mu = jnp.mean(x, axis=-1, keepdims=True)
    xc = x - mu
    var = jnp.mean(xc * xc, axis=-1, keepdims=True)
    return xc * lax.rsqrt(var + LN_EPS) * g + b


def _take_cols(w, idx, sign=None):
    w_aug = jnp.concatenate([w, jnp.zeros((w.shape[0], 1), w.dtype)], axis=1)
    idx = np.where(idx < 0, w.shape[1], idx)
    out = w_aug[:, idx]
    if sign is not None:
        out = out * jnp.asarray(sign, w.dtype)[None, :]
    return out


def _rot_src(d, half):
    r = d % (2 * half)
    base = d - r
    if r < half:
        return base + r + half, -1.0
    return base + r - half, 1.0


def _w_ext_index():
    cut = np.concatenate([[0], np.cumsum(SPLITS)])
    c_cq, c_ckv, c_kr, c_glu, c_qc, c_kc, c_vc = [int(c) for c in cut[:-1]]
    idx = -np.ones((_W_EXT,), np.int64)
    sgn = np.ones((_W_EXT,), np.float32)
    idx[_OFF_CQ:_OFF_CQ + 256] = c_cq + np.arange(256)
    idx[_OFF_CKV:_OFF_CKV + 128] = c_ckv + np.arange(128)
    for d in range(MLA_ROPE):
        idx[_OFF_KR + MLA_NOPE + d] = c_kr + d
        s, sg = _rot_src(d, MLA_ROPE // 2)
        idx[_OFF_KRR + MLA_NOPE + d] = c_kr + s
        sgn[_OFF_KRR + MLA_NOPE + d] = sg
    idx[_OFF_GA:_OFF_GA + 256] = c_glu + np.arange(256)
    idx[_OFF_GG:_OFF_GG + 256] = c_glu + 256 + np.arange(256)
    for blk, hq in enumerate(_GQA_BLOCK_HEADS):
        col0 = (blk // 2) * LANE + (blk % 2) * HALF
        for d in range(GQA_HD):
            idx[_OFF_QC + col0 + d] = c_qc + hq * GQA_HD + d
            s, sg = _rot_src(d, GQA_HD // 4)
            idx[_OFF_QCR + col0 + d] = c_qc + hq * GQA_HD + s
            sgn[_OFF_QCR + col0 + d] = sg
    for c in range(GQA_KV_HEADS * GQA_HD):
        hk, d = divmod(c, GQA_HD)
        idx[_OFF_KC + c] = c_kc + c
        s, sg = _rot_src(d, GQA_HD // 4)
        idx[_OFF_KCR + c] = c_kc + hk * GQA_HD + s
        sgn[_OFF_KCR + c] = sg
    idx[_OFF_VC:_OFF_VC + HALF] = c_vc + np.arange(HALF)
    idx[_OFF_VC + LANE + HALF:_OFF_VC + 2 * LANE] = c_vc + HALF + np.arange(HALF)
    return idx, sgn


def _w_uq_index():
    per = MLA_NOPE + MLA_ROPE
    idx = -np.ones((2 * MLA_HEADS * LANE,), np.int64)
    sgn = np.ones(idx.shape, np.float32)
    for h in range(MLA_HEADS):
        idx[h * LANE:h * LANE + per] = h * per + np.arange(per)
        for d in range(MLA_ROPE):
            s, sg = _rot_src(d, MLA_ROPE // 2)
            idx[(MLA_HEADS + h) * LANE + MLA_NOPE + d] = h * per + MLA_NOPE + s
            sgn[(MLA_HEADS + h) * LANE + MLA_NOPE + d] = sg
    return idx, sgn


def _w_ukv_index():
    per = MLA_NOPE + MLA_V
    idx = -np.ones((MLA_HEADS * 2 * LANE,), np.int64)
    for h in range(MLA_HEADS):
        v0 = h * 2 * LANE + (0 if h % 2 == 0 else HALF)
        idx[v0:v0 + MLA_V] = h * per + MLA_NOPE + np.arange(MLA_V)
        idx[h * 2 * LANE + LANE:h * 2 * LANE + LANE + MLA_NOPE] = h * per + np.arange(MLA_NOPE)
    return idx


def _w_oc_index():
    idx = np.zeros((GQA_HEADS * GQA_HD,), np.int64)
    for blk, hq in enumerate(_GQA_BLOCK_HEADS):
        c0 = (blk // 2) * LANE + (blk % 2) * HALF
        idx[c0:c0 + GQA_HD] = hq * GQA_HD + np.arange(GQA_HD)
    return idx


def _perm_gain(g, half):
    src = np.array([_rot_src(d, half)[0] for d in range(g.shape[-1])])
    return g[src]


def _rope_tables(S):
    assert MLA_ROPE == GQA_HD // 2
    inv = 1.0 / (ROPE_THETA ** (jnp.arange(0, MLA_ROPE, 2, dtype=F32) / MLA_ROPE))
    z = lambda k: jnp.zeros((k,), F32)
    per_head = [jnp.concatenate([inv, inv, z(GQA_HD // 2)]), jnp.concatenate([z(GQA_HD // 2), inv, inv])]
    f_pos = jnp.concatenate([z(MLA_NOPE), inv, inv, z(LANE - MLA_NOPE - MLA_ROPE), z(LANE)])
    f_row = jnp.concatenate([z(LANE), per_head[0], per_head[0]])
    f_col = jnp.concatenate([z(LANE), per_head[1], per_head[1]])
    t = jnp.arange(S, dtype=jnp.int32)
    pos = t.astype(F32)[:, None]
    row = (t // GRID_W).astype(F32)[:, None]
    col = (t % GRID_W).astype(F32)[:, None]
    ang = pos * f_pos[None, :] + row * f_row[None, :] + col * f_col[None, :]
    return jnp.cos(ang), jnp.sin(ang)


def _ln_kernel(x_ref, g_ref, b_ref, o_ref):
    o_ref[...] = _ln_rows(x_ref[...], g_ref[...], b_ref[...])


def _layer_norm_rows(x, g, b):
    n, d = x.shape
    tm = _row_tile(n, 1024)
    return pl.pallas_call(
        _ln_kernel,
        grid=(n // tm,),
        in_specs=[pl.BlockSpec((tm, d), lambda i: (i, 0)),
                  pl.BlockSpec((1, d), lambda i: (0, 0)),
                  pl.BlockSpec((1, d), lambda i: (0, 0))],
        out_specs=pl.BlockSpec((tm, d), lambda i: (i, 0)),
        out_shape=jax.ShapeDtypeStruct((n, d), F32),
        compiler_params=_cparams("parallel"),
    )(x, g.reshape(1, d), b.reshape(1, d))


def _rms_scale(ss, width):
    return lax.rsqrt(ss * (1.0 / width) + RMS_EPS)


def _pre_kernel(h_ref, cos_ref, sin_ref, wext_ref, wuq_ref, wukv_ref, qn_ref, kvn_ref,
                gq_ref, gqp_ref, gk_ref, gkp_ref,
                qa_ref, kva_ref, glu_ref, qg_ref, kvg_ref):
    hb = h_ref[0].astype(BF16)
    z = jnp.dot(hb, wext_ref[...], preferred_element_type=F32)
    cm = cos_ref[:, 0:LANE]
    cg = cos_ref[:, LANE:2 * LANE]
    sm = sin_ref[:, 0:LANE]
    sg = sin_ref[:, LANE:2 * LANE]

    cq = z[:, _OFF_CQ:_OFF_CQ + MLA_Q_LORA]
    cqn = cq * _rms_scale(jnp.sum(cq * cq, axis=-1, keepdims=True), MLA_Q_LORA) * qn_ref[...]
    a = jnp.dot(cqn.astype(BF16), wuq_ref[...], preferred_element_type=F32)
    rot0 = MLA_HEADS * LANE
    for h in range(MLA_HEADS):
        roped = a[:, h * LANE:(h + 1) * LANE] * cm + a[:, rot0 + h * LANE:rot0 + (h + 1) * LANE] * sm
        qa_ref[0, :, h * LANE:(h + 1) * LANE] = (roped * MLA_SCALE).astype(BF16)

    ckv = z[:, _OFF_CKV:_OFF_CKV + MLA_KV_LORA]
    ckvn = ckv * _rms_scale(jnp.sum(ckv * ckv, axis=-1, keepdims=True), MLA_KV_LORA) * kvn_ref[...]
    kvm = jnp.dot(ckvn.astype(BF16), wukv_ref[...], preferred_element_type=F32)
    kr = z[:, _OFF_KR:_OFF_KR + LANE] * cm + z[:, _OFF_KRR:_OFF_KRR + LANE] * sm
    lane = lax.broadcasted_iota(jnp.int32, (1, LANE), 1)
    one_at = [jnp.where(lane == L, 1.0, 0.0) for L in _SUM_LANE]
    for h in range(MLA_HEADS):
        c0 = h * 2 * LANE
        kva_ref[0, :, c0:c0 + LANE] = (kvm[:, c0:c0 + LANE] + one_at[h % 2]).astype(BF16)
        kva_ref[0, :, c0 + LANE:c0 + 2 * LANE] = (kvm[:, c0 + LANE:c0 + 2 * LANE] + kr).astype(BF16)

    glu_ref[0] = z[:, _OFF_GA:_OFF_GA + CONV_CH] * jax.nn.sigmoid(z[:, _OFF_GG:_OFF_GG + CONV_CH])

    low = lax.broadcasted_iota(jnp.int32, (h_ref.shape[1], LANE), 1) < HALF

    def norm_rope(x, x_rot, g, g_rot):
        sq = x * x
        r0 = _rms_scale(jnp.sum(jnp.where(low, sq, 0.0), axis=-1, keepdims=True), GQA_HD)
        r1 = _rms_scale(jnp.sum(jnp.where(low, 0.0, sq), axis=-1, keepdims=True), GQA_HD)
        return (x * g * cg + x_rot * g_rot * sg) * jnp.where(low, r0, r1)

    gq = gq_ref[...]
    gqp = gqp_ref[...]
    for p in range(GQA_HEADS // 2):
        qn = norm_rope(z[:, _OFF_QC + p * LANE:_OFF_QC + (p + 1) * LANE],
                       z[:, _OFF_QCR + p * LANE:_OFF_QCR + (p + 1) * LANE], gq, gqp) * GQA_SCALE
        qg_ref[0, :, 2 * p * LANE:(2 * p + 1) * LANE] = jnp.where(low, qn, 0.0).astype(BF16)
        qg_ref[0, :, (2 * p + 1) * LANE:(2 * p + 2) * LANE] = jnp.where(low, 0.0, qn).astype(BF16)

    kvg_ref[0, :, 0:LANE] = norm_rope(z[:, _OFF_KC:_OFF_KC + LANE], z[:, _OFF_KCR:_OFF_KCR + LANE],
                                      gk_ref[...], gkp_ref[...]).astype(BF16)
    kvg_ref[0, :, LANE:2 * LANE] = (z[:, _OFF_VC:_OFF_VC + LANE] + one_at[0]).astype(BF16)
    kvg_ref[0, :, 2 * LANE:3 * LANE] = (z[:, _OFF_VC + LANE:_OFF_VC + 2 * LANE] + one_at[1]).astype(BF16)


def _mixer_pre(h, tab, lw):
    B, S, D = h.shape
    tm = _row_tile(S, 512)
    const = lambda b, i: (0, 0)
    row3 = lambda b, i: (b, i, 0)
    outs = [(MLA_HEADS * LANE, BF16), (MLA_HEADS * 2 * LANE, BF16), (CONV_CH, F32),
            (GQA_HEADS * LANE, BF16), (3 * LANE, BF16)]
    return pl.pallas_call(
        _pre_kernel,
        grid=(B, S // tm),
        in_specs=[pl.BlockSpec((1, tm, D), row3),
                  pl.BlockSpec((tm, 2 * LANE), lambda b, i: (i, 0)),
                  pl.BlockSpec((tm, 2 * LANE), lambda b, i: (i, 0)),
                  pl.BlockSpec(lw["w_ext"].shape, const),
                  pl.BlockSpec(lw["w_uq"].shape, const),
                  pl.BlockSpec(lw["w_ukv"].shape, const),
                  pl.BlockSpec((1, MLA_Q_LORA), const),
                  pl.BlockSpec((1, MLA_KV_LORA), const),
                  pl.BlockSpec((1, LANE), const),
                  pl.BlockSpec((1, LANE), const),
                  pl.BlockSpec((1, LANE), const),
                  pl.BlockSpec((1, LANE), const)],
        out_specs=[pl.BlockSpec((1, tm, w), row3) for w, _ in outs],
        out_shape=[jax.ShapeDtypeStruct((B, S, w), dt) for w, dt in outs],
        compiler_params=_cparams("parallel", "parallel"),
    )(h, tab[0], tab[1], lw["w_ext"], lw["w_uq"], lw["w_ukv"], lw["qn"], lw["kvn"],
      lw["gq"], lw["gqp"], lw["gk"], lw["gkp"])


def _attn_kernel(q_ref, kv_ref, o_ref, m_sc, acc_sc, s_sc, *, k_off, v_off, tkc):
    j = pl.program_id(3)
    tk = kv_ref.shape[1]
    units = [(i, c) for c in range(tk // tkc) for i in range(2)]
    n_col = tkc // LANE

    @pl.when(j == 0)
    def _():
        m_sc[...] = jnp.full(m_sc.shape, -jnp.inf, F32)
        acc_sc[...] = jnp.zeros(acc_sc.shape, F32)

    def scores(u):
        i, c = units[u]
        q = q_ref[0, :, i * LANE:(i + 1) * LANE]
        k = kv_ref[0, c * tkc:(c + 1) * tkc, k_off[i]:k_off[i] + LANE]
        return lax.dot_general(q, k, (((1,), (1,)), ((), ())), preferred_element_type=F32)

    s_sc[0] = scores(0)
    for u, (i, c) in enumerate(units):
        if u + 1 < len(units):
            s_sc[(u + 1) % 2] = scores(u + 1)
        cols = [s_sc[u % 2, :, cb * LANE:(cb + 1) * LANE] for cb in range(n_col)]
        mx = cols[0]
        for x in cols[1:]:
            mx = jnp.maximum(mx, x)
        m_prev = m_sc[i]
        m_new = jnp.maximum(m_prev, jnp.max(mx, axis=1, keepdims=True))
        alpha = jnp.exp2(m_prev - m_new)
        p = jnp.concatenate([jnp.exp2(x - m_new).astype(BF16) for x in cols], axis=1)
        v = kv_ref[0, c * tkc:(c + 1) * tkc, v_off[i]:v_off[i] + LANE]
        acc_sc[i] = alpha * acc_sc[i] + jnp.dot(p, v, preferred_element_type=F32)
        m_sc[i] = m_new

    @pl.when(j == pl.num_programs(3) - 1)
    def _():
        o0 = acc_sc[0] / acc_sc[0, :, _SUM_LANE[0]:_SUM_LANE[0] + 1]
        o1 = acc_sc[1] / acc_sc[1, :, _SUM_LANE[1]:_SUM_LANE[1] + 1]
        low = lax.broadcasted_iota(jnp.int32, o0.shape, 1) < HALF
        o_ref[0] = jnp.where(low, o0, o1).astype(o_ref.dtype)


ATTN_TQ = 1024
ATTN_TK = 4096
ATTN_TKC = 512


def _attention(q, kv, *, kvw_block, kv_block_of_pair, k_off, v_off):
    B, S, _ = q.shape
    n_pairs = 3
    tq = _row_tile(S, ATTN_TQ)
    tk = _row_tile(S, ATTN_TK)
    tkc = _row_tile(tk, ATTN_TKC)
    kern = functools.partial(_attn_kernel, k_off=k_off, v_off=v_off, tkc=tkc)
    return pl.pallas_call(
        kern,
        grid=(B, n_pairs, S // tq, S // tk),
        in_specs=[pl.BlockSpec((1, tq, 2 * LANE), lambda b, p, i, j: (b, i, p)),
                  pl.BlockSpec((1, tk, kvw_block), lambda b, p, i, j: (b, j, kv_block_of_pair(p)))],
        out_specs=pl.BlockSpec((1, tq, LANE), lambda b, p, i, j: (b, i, p)),
        out_shape=jax.ShapeDtypeStruct((B, S, n_pairs * LANE), BF16),
        scratch_shapes=[pltpu.VMEM((2, tq, LANE), F32),
                        pltpu.VMEM((2, tq, LANE), F32),
                        pltpu.VMEM((2, tq, tkc), F32)],
        compiler_params=_cparams("parallel", "parallel", "parallel", "arbitrary"),
    )(q, kv)


def _mla_attention(qa, kva):
    return _attention(qa, kva, kvw_block=4 * LANE, kv_block_of_pair=lambda p: p,
                      k_off=(LANE, 3 * LANE), v_off=(0, 2 * LANE))


def _gqa_attention(qg, kvg):
    return _attention(qg, kvg, kvw_block=3 * LANE, kv_block_of_pair=lambda p: 0,
                      k_off=(0, 0), v_off=(LANE, 2 * LANE))


_HALO = 16


def _conv_kernel(prev_ref, cur_ref, next_ref, w_ref, b_ref, g_ref, beta_ref, o_ref, xe_sc, sh_sc):
    i = pl.program_id(1)
    tc = cur_ref.shape[1]
    first = i == 0
    last = i == pl.num_programs(1) - 1
    xe_sc[0:_HALO, :] = jnp.where(first, 0.0, prev_ref[0])
    xe_sc[_HALO:_HALO + tc, :] = cur_ref[0]
    xe_sc[_HALO + tc:_HALO + tc + _HALO, :] = jnp.where(last, 0.0, next_ref[0])
    acc = jnp.zeros((tc, CONV_CH), F32)
    span = sh_sc.shape[0]
    for shift in range(SUBLANE):
        taps = [t for t in range(CONV_K) if (_HALO - CONV_K // 2 + t) % SUBLANE == shift]
        if not taps:
            continue
        sh_sc[...] = xe_sc[shift:shift + span, :]
        for t in taps:
            base = _HALO - CONV_K // 2 + t - shift
            acc = acc + sh_sc[base:base + tc, :] * w_ref[t:t + 1, :]
    u = acc + b_ref[...]
    y = _ln_rows(u, g_ref[...], beta_ref[...])
    o_ref[0] = (y * jax.nn.sigmoid(y)).astype(o_ref.dtype)


def _conv_module(glu, lw):
    B, S, C = glu.shape
    tc = _row_tile(S, 512)
    nh = tc // _HALO
    n_halo_blocks = S // _HALO
    const = lambda b, i: (0, 0)
    return pl.pallas_call(
        _conv_kernel,
        grid=(B, S // tc),
        in_specs=[pl.BlockSpec((1, _HALO, C), lambda b, i: (b, jnp.maximum(i * nh - 1, 0), 0)),
                  pl.BlockSpec((1, tc, C), lambda b, i: (b, i, 0)),
                  pl.BlockSpec((1, _HALO, C), lambda b, i: (b, jnp.minimum((i + 1) * nh, n_halo_blocks - 1), 0)),
                  pl.BlockSpec((CONV_K, C), const),
                  pl.BlockSpec((1, C), const),
                  pl.BlockSpec((1, C), const),
                  pl.BlockSpec((1, C), const)],
        out_specs=pl.BlockSpec((1, tc, C), lambda b, i: (b, i, 0)),
        out_shape=jax.ShapeDtypeStruct((B, S, C), BF16),
        scratch_shapes=[pltpu.VMEM((tc + 2 * _HALO, C), F32),
                        pltpu.VMEM((tc + 2 * _HALO - SUBLANE, C), F32)],
        compiler_params=_cparams("parallel", "parallel"),
    )(glu, glu, glu, lw["conv_w"], lw["conv_b"], lw["conv_ln_g"], lw["conv_ln_b"])


def _memkv_kernel(x_ref, w_ref, o_ref):
    o_ref[...] = jnp.dot(x_ref[...].astype(BF16), w_ref[...], preferred_element_type=F32).astype(o_ref.dtype)


def _mem_kv(mem2d, w_ckv):
    n, d = mem2d.shape
    tm = _row_tile(n, 256)
    return pl.pallas_call(
        _memkv_kernel,
        grid=(n // tm,),
        in_specs=[pl.BlockSpec((tm, d), lambda i: (i, 0)),
                  pl.BlockSpec(w_ckv.shape, lambda i: (0, 0))],
        out_specs=pl.BlockSpec((tm, w_ckv.shape[1]), lambda i: (i, 0)),
        out_shape=jax.ShapeDtypeStruct((n, w_ckv.shape[1]), BF16),
        compiler_params=_cparams("parallel"),
    )(mem2d, w_ckv)


def _mixout_xattn_kernel(h_ref, oa_ref, ob_ref, oc_ref, wmix_ref, g1_ref, b1_ref,
                         k_ref, v_ref, wq_ref, wo_ref, g2_ref, b2_ref, o_ref):
    tm = h_ref.shape[1]
    groups = [slice(r * (tm // XA_ROW_GROUPS), (r + 1) * (tm // XA_ROW_GROUPS)) for r in range(XA_ROW_GROUPS)]
    mix = [jnp.dot(jnp.concatenate([oa_ref[0, r, :], ob_ref[0, r, :], oc_ref[0, r, :]], axis=1), wmix_ref[...],
                   preferred_element_type=F32) for r in groups]
    h1 = [_ln_rows(DN_ALPHA * h_ref[0, r, :] + m, g1_ref[...], b1_ref[...]) for r, m in zip(groups, mix)]
    q = [(jnp.dot(h.astype(BF16), wq_ref[...], preferred_element_type=F32) * XA_SCALE).astype(BF16) for h in h1]
    att = []
    for qg in q:
        heads = []
        for hd in range(XA_HEADS):
            sl = slice(hd * XA_HD, (hd + 1) * XA_HD)
            s = lax.dot_general(qg[:, sl], k_ref[0, :, sl], (((1,), (1,)), ((), ())),
                                preferred_element_type=F32)
            p = jnp.exp(s - jnp.max(s, axis=1, keepdims=True))
            o = jnp.dot(p.astype(BF16), v_ref[0, :, sl], preferred_element_type=F32)
            heads.append((o / jnp.sum(p, axis=1, keepdims=True)).astype(BF16))
        att.append(jnp.concatenate(heads, axis=1))
    y = [jnp.dot(o, wo_ref[...], preferred_element_type=F32) for o in att]
    for r, h, yy in zip(groups, h1, y):
        o_ref[0, r, :] = _ln_rows(DN_ALPHA * h + yy, g2_ref[...], b2_ref[...])


def _mixout_cross_attention(h, oa, ob, oc, kvmem, lw):
    B, S, D = h.shape
    M = kvmem.shape[1]
    tm = _row_tile(S, 1024)
    const = lambda b, i: (0, 0)
    row = lambda b, i: (b, i, 0)
    return pl.pallas_call(
        _mixout_xattn_kernel,
        grid=(B, S // tm),
        in_specs=[pl.BlockSpec((1, tm, D), row),
                  pl.BlockSpec((1, tm, oa.shape[2]), row),
                  pl.BlockSpec((1, tm, ob.shape[2]), row),
                  pl.BlockSpec((1, tm, oc.shape[2]), row),
                  pl.BlockSpec(lw["w_mix"].shape, const),
                  pl.BlockSpec((1, D), const),
                  pl.BlockSpec((1, D), const),
                  pl.BlockSpec((1, M, D), lambda b, i: (b, 0, 0)),
                  pl.BlockSpec((1, M, D), lambda b, i: (b, 0, 1)),
                  pl.BlockSpec((D, D), const),
                  pl.BlockSpec((D, D), const),
                  pl.BlockSpec((1, D), const),
                  pl.BlockSpec((1, D), const)],
        out_specs=pl.BlockSpec((1, tm, D), row),
        out_shape=jax.ShapeDtypeStruct((B, S, D), F32),
        compiler_params=_cparams("parallel", "parallel"),
    )(h, oa, ob, oc, lw["w_mix"], lw["ln1_g"], lw["ln1_b"],
      kvmem, kvmem, lw["w_cq"], lw["w_co"], lw["ln2_g"], lw["ln2_b"])


MOE_ROUTER_TILE = 1024
MOE_TILE = 2048
MOE_SUB = 256
MOE_SUB_SHIFT = 8
MOE_SEG = 48
MOE_ALIGN = 16
MOE_EXPERTS_PER_STEP = 2


def _router_kernel(x_ref, wr_ref, bias_ref, gate_ref, pos_ref, dd_ref, cnt_ref, start_ref, sel_sc):
    T = x_ref.shape[0]
    x = x_ref[...]
    x_hi = x.astype(BF16)
    x_lo = (x - x_hi.astype(F32)).astype(BF16)
    w = wr_ref[...]
    w_hi = w.astype(BF16)
    w_lo = (w - w_hi.astype(F32)).astype(BF16)
    nt = (((1,), (1,)), ((), ()))
    both = lax.dot_general(jnp.concatenate([w_hi, w_lo], axis=0), x_hi, nt, preferred_element_type=F32)
    logits = (both[:N_EXPERTS] + both[N_EXPERTS:]
              + lax.dot_general(w_hi, x_lo, nt, preferred_element_type=F32))
    scores = jax.nn.sigmoid(logits)
    sel = scores + bias_ref[...]
    sc = [scores[e:e + 1, :] for e in range(N_EXPERTS)]
    se = [sel[e:e + 1, :] for e in range(N_EXPERTS)]

    gs = []
    for g in range(N_GROUPS):
        m = [se[g * EXPERTS_PER_GROUP + k] for k in range(EXPERTS_PER_GROUP)]
        best = None
        for a in range(EXPERTS_PER_GROUP):
            for b in range(a + 1, EXPERTS_PER_GROUP):
                pair = m[a] + m[b]
                best = pair if best is None else jnp.maximum(best, pair)
        gs.append(best)
    in_group = []
    for g in range(N_GROUPS):
        ok = None
        for o in range(N_GROUPS):
            if o == g:
                continue
            c = (gs[g] > gs[o]) if o < g else (gs[g] >= gs[o])
            ok = c if ok is None else (ok & c)
        in_group.append(ok)
    chosen = []
    for e in range(N_EXPERTS):
        g = e // EXPERTS_PER_GROUP
        beaten = jnp.zeros((1, T), jnp.int32)
        for o in range(g * EXPERTS_PER_GROUP, (g + 1) * EXPERTS_PER_GROUP):
            if o == e:
                continue
            c = (se[o] > se[e]) if o > e else (se[o] >= se[e])
            beaten = beaten + c.astype(jnp.int32)
        chosen.append(in_group[g] & (beaten < 2))
    denom = jnp.zeros((1, T), F32)
    for e in range(N_EXPERTS):
        denom = denom + jnp.where(chosen[e], sc[e], 0.0)
    for e in range(N_EXPERTS):
        gate_ref[e:e + 1, :] = jnp.where(chosen[e], sc[e] / denom, 0.0)
        sel_sc[e:e + 1, :] = chosen[e].astype(F32)

    selm = sel_sc[...].astype(BF16)
    sub_r = jnp.right_shift(lax.broadcasted_iota(jnp.int32, (T, T), 0), MOE_SUB_SHIFT)
    sub_c = jnp.right_shift(lax.broadcasted_iota(jnp.int32, (T, T), 1), MOE_SUB_SHIFT)
    same = sub_r == sub_c
    earlier = same & (lax.broadcasted_iota(jnp.int32, (T, T), 0) < lax.broadcasted_iota(jnp.int32, (T, T), 1))
    rank = jnp.dot(selm, earlier.astype(BF16), preferred_element_type=F32)
    cnt = jnp.dot(selm, same.astype(BF16), preferred_element_type=F32)
    cnt_ref[0] = cnt.astype(jnp.int32)

    seg = jnp.floor((cnt + (MOE_ALIGN - 1)) * (1.0 / MOE_ALIGN)) * MOE_ALIGN
    start = jnp.zeros((1, T), F32)
    d_lo = jnp.full((1, T), float(4 * T), F32)
    d_hi = jnp.full((1, T), -1.0, F32)
    for e in range(N_EXPERTS):
        start_ref[0, e:e + 1, :] = start.astype(jnp.int32)
        dest = jnp.where(chosen[e], rank[e:e + 1, :] + start, -1.0)
        pos_ref[e:e + 1, :] = dest
        d_lo = jnp.where(chosen[e], jnp.minimum(d_lo, dest), d_lo)
        d_hi = jnp.maximum(d_hi, dest)
        start = start + seg[e:e + 1, :]
    dd_ref[...] = jnp.zeros(dd_ref.shape, F32)
    dd_ref[0:1, :] = d_lo
    dd_ref[1:2, :] = d_hi


def _router(x, w_rt, bias):
    n, d = x.shape
    T = _row_tile(n, MOE_ROUTER_TILE)
    nt = n // T
    gate, pos, dd, cnt, start = pl.pallas_call(
        _router_kernel,
        grid=(nt,),
        in_specs=[pl.BlockSpec((T, d), lambda i: (i, 0)),
                  pl.BlockSpec((N_EXPERTS, d), lambda i: (0, 0)),
                  pl.BlockSpec((N_EXPERTS, 1), lambda i: (0, 0))],
        out_specs=[pl.BlockSpec((N_EXPERTS, T), lambda i: (0, i)),
                   pl.BlockSpec((N_EXPERTS, T), lambda i: (0, i)),
                   pl.BlockSpec((8, T), lambda i: (0, i)),
                   pl.BlockSpec((1, N_EXPERTS, T), lambda i: (i, 0, 0)),
                   pl.BlockSpec((1, N_EXPERTS, T), lambda i: (i, 0, 0))],
        out_shape=[jax.ShapeDtypeStruct((N_EXPERTS, n), F32),
                   jax.ShapeDtypeStruct((N_EXPERTS, n), F32),
                   jax.ShapeDtypeStruct((8, n), F32),
                   jax.ShapeDtypeStruct((nt, N_EXPERTS, T), jnp.int32),
                   jax.ShapeDtypeStruct((nt, N_EXPERTS, T), jnp.int32)],
        scratch_shapes=[pltpu.VMEM((N_EXPERTS, T), F32)],
        compiler_params=_cparams("parallel"),
    )(x, w_rt, bias)
    return gate, pos, dd, cnt[:, :, ::MOE_SUB].reshape(-1), start[:, :, ::MOE_SUB].reshape(-1)


MOE_SORTED_ROWS = -(-(2 * MOE_SUB + N_EXPERTS * MOE_ALIGN + MOE_SEG) // 32) * 32


def _moe_kernel(cnt_ref, start_ref, h_ref, gate_ref, pos_ref, dd_ref, wg_ref, wu_ref, wd_ref, g_ref, b_ref, o_ref,
                p_sc, xs_sc, *, router_subs):
    i = pl.program_id(0)
    step = pl.program_id(1)
    n_sub = p_sc.shape[0]
    R = p_sc.shape[1]
    half = R // 2

    @pl.when(step == 0)
    def _():
        for s in range(n_sub):
            tok = slice(s * MOE_SUB, (s + 1) * MOE_SUB)
            d_lo = dd_ref[0:1, tok].astype(jnp.int32)
            d_hi = dd_ref[1:2, tok].astype(jnp.int32)
            xb = h_ref[tok, :].astype(BF16)
            for r0 in (0, half):
                row = lax.broadcasted_iota(jnp.int32, (half, MOE_SUB), 0) + r0
                onehot = ((row == d_lo) | (row == d_hi)).astype(BF16)
                p_sc[s, r0:r0 + half, :] = onehot
                xs_sc[s, r0:r0 + half, :] = jnp.dot(onehot, xb, preferred_element_type=F32).astype(BF16)

    slot = lax.broadcasted_iota(jnp.int32, (MOE_SEG, MOE_SUB), 0)
    for k in range(MOE_EXPERTS_PER_STEP):
        e = step * MOE_EXPERTS_PER_STEP + k
        flat = [((i * (n_sub // router_subs) + s // router_subs) * N_EXPERTS + e) * router_subs + s % router_subs
                for s in range(n_sub)]
        n_rows = [cnt_ref[f] for f in flat]
        seg0 = [start_ref[f] for f in flat]
        most = n_rows[0]
        for c in n_rows[1:]:
            most = jnp.maximum(most, c)
        n_blocks = (most + MOE_SEG - 1) // MOE_SEG
        pos_row = pos_ref[pl.ds(e, 1), :].astype(jnp.int32)
        gate_row = gate_ref[pl.ds(e, 1), :]

        def block(jb, carry, k=k, n_rows=n_rows, seg0=seg0, pos_row=pos_row, gate_row=gate_row):
            base = [pl.multiple_of(jnp.minimum(seg0[s] + jb * MOE_SEG, R - MOE_SEG), MOE_ALIGN)
                    for s in range(n_sub)]
            dest = [pl.multiple_of(jnp.where(jb * MOE_SEG < n_rows[s], base[s], R), MOE_ALIGN)
                    for s in range(n_sub)]
            xin = [xs_sc[s, pl.ds(base[s], MOE_SEG), :] for s in range(n_sub)]
            xg = jnp.concatenate(xin, axis=0)
            hg = jnp.dot(xg, wg_ref[k], preferred_element_type=F32)
            hu = jnp.dot(xg, wu_ref[k], preferred_element_type=F32)
            act = (hg * jax.nn.sigmoid(hg) * hu).astype(BF16)
            y = jnp.dot(act, wd_ref[k], preferred_element_type=F32)
            for s in range(n_sub):
                tok = slice(s * MOE_SUB, (s + 1) * MOE_SUB)
                hit = (pos_row[:, tok] - base[s]) == slot
                g_rows = jnp.sum(jnp.where(hit, gate_row[:, tok], 0.0), axis=1, keepdims=True)
                own = jnp.sum(jnp.where(hit, 1.0, 0.0), axis=1, keepdims=True) > 0.0
                out_rows = (y[s * MOE_SEG:(s + 1) * MOE_SEG, :] * g_rows).astype(BF16)
                xs_sc[s, pl.ds(dest[s], MOE_SEG), :] = jnp.where(own, out_rows, xin[s])
            return carry

        lax.fori_loop(0, n_blocks, block, 0)

    @pl.when(step == N_EXPERTS // MOE_EXPERTS_PER_STEP - 1)
    def _():
        for s in range(n_sub):
            tok = slice(s * MOE_SUB, (s + 1) * MOE_SUB)
            moe = lax.dot_general(p_sc[s], xs_sc[s, 0:R, :], (((0,), (0,)), ((), ())), preferred_element_type=F32)
            o_ref[tok, :] = _ln_rows(DN_ALPHA * h_ref[tok, :] + moe, g_ref[...], b_ref[...])


def _moe(h, gate, pos, dd, cnt, start, lw):
    n, d = h.shape
    T = _row_tile(n, MOE_TILE)
    n_sub = T // MOE_SUB
    router_subs = _row_tile(n, MOE_ROUTER_TILE) // MOE_SUB
    R = MOE_SORTED_ROWS
    f = D_FF_EXPERT
    ne = MOE_EXPERTS_PER_STEP
    grid_spec = pltpu.PrefetchScalarGridSpec(
        num_scalar_prefetch=2,
        grid=(n // T, N_EXPERTS // ne),
        in_specs=[pl.BlockSpec((T, d), lambda i, e, c, s: (i, 0)),
                  pl.BlockSpec((N_EXPERTS, T), lambda i, e, c, s: (0, i)),
                  pl.BlockSpec((N_EXPERTS, T), lambda i, e, c, s: (0, i)),
                  pl.BlockSpec((8, T), lambda i, e, c, s: (0, i)),
                  pl.BlockSpec((ne, d, f), lambda i, e, c, s: (e, 0, 0)),
                  pl.BlockSpec((ne, d, f), lambda i, e, c, s: (e, 0, 0)),
                  pl.BlockSpec((ne, f, d), lambda i, e, c, s: (e, 0, 0)),
                  pl.BlockSpec((1, d), lambda i, e, c, s: (0, 0)),
                  pl.BlockSpec((1, d), lambda i, e, c, s: (0, 0))],
        out_specs=pl.BlockSpec((T, d), lambda i, e, c, s: (i, 0), pipeline_mode=pl.Buffered(1)),
        scratch_shapes=[pltpu.VMEM((n_sub, R, MOE_SUB), BF16), pltpu.VMEM((n_sub, R + MOE_SEG, d), BF16)],
    )
    return pl.pallas_call(
        functools.partial(_moe_kernel, router_subs=router_subs),
        grid_spec=grid_spec,
        out_shape=jax.ShapeDtypeStruct((n, d), F32),
        compiler_params=_cparams("parallel", "arbitrary"),
    )(cnt, start, h, gate, pos, dd, lw["w_gate"], lw["w_up"], lw["w_down"], lw["ln3_g"], lw["ln3_b"])


def _prep_layer(l, w):
    ext_idx, ext_sgn = _w_ext_index()
    uq_idx, uq_sgn = _w_uq_index()
    mla_w = MLA_HEADS * MLA_V
    gq = w["gqa_q_norm"][l]
    gk = w["gqa_k_norm"][l]
    row = lambda v: v.reshape(1, -1).astype(F32)
    return {
        "w_ext": _take_cols(w["w_in"][l], ext_idx, ext_sgn).astype(BF16),
        "w_uq": _take_cols(w["mla_w_uq"][l], uq_idx, uq_sgn).astype(BF16),
        "w_ukv": _take_cols(w["mla_w_ukv"][l], _w_ukv_index()).astype(BF16),
        "qn": row(w["mla_q_norm"][l]),
        "kvn": row(w["mla_kv_norm"][l]),
        "gq": row(jnp.tile(gq, 2)),
        "gqp": row(jnp.tile(_perm_gain(gq, GQA_HD // 4), 2)),
        "gk": row(jnp.tile(gk, 2)),
        "gkp": row(jnp.tile(_perm_gain(gk, GQA_HD // 4), 2)),
        "conv_w": w["conv_w"][l].astype(F32),
        "conv_b": row(w["conv_b"][l]),
        "conv_ln_g": row(w["conv_ln_g"][l]),
        "conv_ln_b": row(w["conv_ln_b"][l]),
        "w_mix": jnp.concatenate([w["w_o"][l][:mla_w + CONV_CH],
                                  w["w_o"][l][mla_w + CONV_CH:][_w_oc_index()]], axis=0).astype(BF16),
        "ln1_g": row(w["ln1_g"][l]), "ln1_b": row(w["ln1_b"][l]),
        "w_cq": w["w_cq"][l].astype(BF16),
        "w_ckv": w["w_ckv"][l].astype(BF16),
        "w_co": w["w_co"][l].astype(BF16),
        "ln2_g": row(w["ln2_g"][l]), "ln2_b": row(w["ln2_b"][l]),
        "w_gate": w["w_gate"][l].astype(BF16),
        "w_up": w["w_up"][l].astype(BF16),
        "w_down": w["w_down"][l].astype(BF16),
        "ln3_g": row(w["ln3_g"][l]), "ln3_b": row(w["ln3_b"][l]),
    }


def _trunk(x, mem, w, layers):
    B, S, D = x.shape
    n = B * S
    tab = _rope_tables(S)
    w_rt = w["w_router"].T.astype(F32)
    r_bias = w["router_bias"].reshape(N_EXPERTS, 1).astype(F32)
    h = _layer_norm_rows(x.reshape(n, D), w["ln_in_g"], w["ln_in_b"])
    mem2d = mem.reshape(B * mem.shape[1], D)
    for lw in layers:
        qa, kva, glu, qg, kvg = _mixer_pre(h.reshape(B, S, D), tab, lw)
        oa = _mla_attention(qa, kva)
        ob = _conv_module(glu, lw)
        oc = _gqa_attention(qg, kvg)
        kvmem = _mem_kv(mem2d, lw["w_ckv"]).reshape(B, mem.shape[1], 2 * D)
        h = _mixout_cross_attention(h.reshape(B, S, D), oa, ob, oc, kvmem, lw).reshape(n, D)
        gate, pos, dd, cnt, start = _router(h, w_rt, r_bias)
        h = _moe(h, gate, pos, dd, cnt, start, lw)
    return h.reshape(B, S, D)


def kernel(x_prompt, x_sample, mem_prompt, mem_sample, ln_in_g, ln_in_b, w_in, mla_q_norm, mla_w_uq,
           mla_kv_norm, mla_w_ukv, conv_w, conv_b, conv_ln_g, conv_ln_b, gqa_q_norm, gqa_k_norm, w_o,
           ln1_g, ln1_b, w_cq, w_ckv, w_co, ln2_g, ln2_b, w_router, router_bias, w_gate, w_up, w_down,
           ln3_g, ln3_b):
    w = dict(ln_in_g=ln_in_g, ln_in_b=ln_in_b, w_in=w_in, mla_q_norm=mla_q_norm, mla_w_uq=mla_w_uq,
             mla_kv_norm=mla_kv_norm, mla_w_ukv=mla_w_ukv, conv_w=conv_w, conv_b=conv_b,
             conv_ln_g=conv_ln_g, conv_ln_b=conv_ln_b, gqa_q_norm=gqa_q_norm, gqa_k_norm=gqa_k_norm,
             w_o=w_o, ln1_g=ln1_g, ln1_b=ln1_b, w_cq=w_cq, w_ckv=w_ckv, w_co=w_co, ln2_g=ln2_g,
             ln2_b=ln2_b, w_router=w_router, router_bias=router_bias, w_gate=w_gate, w_up=w_up,
             w_down=w_down, ln3_g=ln3_g, ln3_b=ln3_b)
    layers = [_prep_layer(l, w) for l in range(w_in.shape[0])]
    y_prompt = _trunk(x_prompt, mem_prompt, w, layers)
    y_sample = _trunk(x_sample, mem_sample, w, layers)
    return (y_prompt, y_sample)
```

```python
import functools

import numpy as np
import jax
import jax.numpy as jnp
from jax import lax
from jax.experimental import pallas as pl
from jax.experimental.pallas import tpu as pltpu

F32 = jnp.float32
BF16 = jnp.bfloat16

D_MODEL = 1024
DEPTH = 4
GRID_W = 64
ROPE_THETA = 10000.0
LN_EPS = 1e-5
RMS_EPS = 1e-6
MLA_HEADS = 6
MLA_Q_LORA = 256
MLA_KV_LORA = 128
MLA_NOPE = 64
MLA_ROPE = 32
MLA_V = 64
CONV_CH = 256
CONV_K = 31
GQA_HEADS = 6
GQA_KV_HEADS = 2
GQA_HD = 64
SPLITS = (MLA_Q_LORA, MLA_KV_LORA, MLA_ROPE, 2 * CONV_CH,
          GQA_HEADS * GQA_HD, GQA_KV_HEADS * GQA_HD, GQA_KV_HEADS * GQA_HD)
XA_HEADS = 4
XA_HD = D_MODEL // XA_HEADS
N_EXPERTS = 16
N_GROUPS = 4
EXPERTS_PER_GROUP = N_EXPERTS // N_GROUPS
D_FF_EXPERT = 512
DN_ALPHA = (2 * DEPTH) ** 0.25
LOG2E = 1.4426950408889634
MLA_SCALE = (MLA_NOPE + MLA_ROPE) ** -0.5 * LOG2E
GQA_SCALE = GQA_HD ** -0.5 * LOG2E
XA_SCALE = XA_HD ** -0.5
XA_ROW_GROUPS = 4

LANE = 128
HALF = LANE // 2
SUBLANE = 8
VMEM_LIMIT = 62 * 1024 * 1024

_OFF_CQ = 0
_OFF_CKV = 256
_OFF_KR = 384
_OFF_KRR = 512
_OFF_GA = 640
_OFF_GG = 896
_OFF_QC = 1152
_OFF_QCR = 1536
_OFF_KC = 1920
_OFF_KCR = 2048
_OFF_VC = 2176
_W_EXT = 2432
_GQA_BLOCK_HEADS = (0, 3, 1, 4, 2, 5)
_SUM_LANE = (HALF, 0)


def _cparams(*sem):
    return pltpu.CompilerParams(dimension_semantics=sem, vmem_limit_bytes=VMEM_LIMIT)


def _row_tile(n, want):
    t = min(n, want)
    while n % t:
        t //= 2
    return t


def _ln_rows(x, g, b):
    mu = jnp.mean(x, axis=-1, keepdims=True)
    xc = x - mu
    var = jnp.mean(xc * xc, axis=-1, keepdims=True)
    return xc * lax.rsqrt(var + LN_EPS) * g + b


def _take_cols(w, idx, sign=None):
    w_aug = jnp.concatenate([w, jnp.zeros((w.shape[0], 1), w.dtype)], axis=1)
    idx = np.where(idx < 0, w.shape[1], idx)
    out = w_aug[:, idx]
    if sign is not None:
        out = out * jnp.asarray(sign, w.dtype)[None, :]
    return out


def _rot_src(d, half):
    r = d % (2 * half)
    base = d - r
    if r < half:
        return base + r + half, -1.0
    return base + r - half, 1.0


def _w_ext_index():
    cut = np.concatenate([[0], np.cumsum(SPLITS)])
    c_cq, c_ckv, c_kr, c_glu, c_qc, c_kc, c_vc = [int(c) for c in cut[:-1]]
    idx = -np.ones((_W_EXT,), np.int64)
    sgn = np.ones((_W_EXT,), np.float32)
    idx[_OFF_CQ:_OFF_CQ + 256] = c_cq + np.arange(256)
    idx[_OFF_CKV:_OFF_CKV + 128] = c_ckv + np.arange(128)
    for d in range(MLA_ROPE):
        idx[_OFF_KR + MLA_NOPE + d] = c_kr + d
        s, sg = _rot_src(d, MLA_ROPE // 2)
        idx[_OFF_KRR + MLA_NOPE + d] = c_kr + s
        sgn[_OFF_KRR + MLA_NOPE + d] = sg
    idx[_OFF_GA:_OFF_GA + 256] = c_glu + np.arange(256)
    idx[_OFF_GG:_OFF_GG + 256] = c_glu + 256 + np.arange(256)
    for blk, hq in enumerate(_GQA_BLOCK_HEADS):
        col0 = (blk // 2) * LANE + (blk % 2) * HALF
        for d in range(GQA_HD):
            idx[_OFF_QC + col0 + d] = c_qc + hq * GQA_HD + d
            s, sg = _rot_src(d, GQA_HD // 4)
            idx[_OFF_QCR + col0 + d] = c_qc + hq * GQA_HD + s
            sgn[_OFF_QCR + col0 + d] = sg
    for c in range(GQA_KV_HEADS * GQA_HD):
        hk, d = divmod(c, GQA_HD)
        idx[_OFF_KC + c] = c_kc + c
        s, sg = _rot_src(d, GQA_HD // 4)
        idx[_OFF_KCR + c] = c_kc + hk * GQA_HD + s
        sgn[_OFF_KCR + c] = sg
    idx[_OFF_VC:_OFF_VC + HALF] = c_vc + np.arange(HALF)
    idx[_OFF_VC + LANE + HALF:_OFF_VC + 2 * LANE] = c_vc + HALF + np.arange(HALF)
    return idx, sgn


def _w_uq_index():
    per = MLA_NOPE + MLA_ROPE
    idx = -np.ones((2 * MLA_HEADS * LANE,), np.int64)
    sgn = np.ones(idx.shape, np.float32)
    for h in range(MLA_HEADS):
        idx[h * LANE:h * LANE + per] = h * per + np.arange(per)
        for d in range(MLA_ROPE):
            s, sg = _rot_src(d, MLA_ROPE // 2)
            idx[(MLA_HEADS + h) * LANE + MLA_NOPE + d] = h * per + MLA_NOPE + s
            sgn[(MLA_HEADS + h) * LANE + MLA_NOPE + d] = sg
    return idx, sgn


def _w_ukv_index():
    per = MLA_NOPE + MLA_V
    idx = -np.ones((MLA_HEADS * 2 * LANE,), np.int64)
    for h in range(MLA_HEADS):
        v0 = h * 2 * LANE + (0 if h % 2 == 0 else HALF)
        idx[v0:v0 + MLA_V] = h * per + MLA_NOPE + np.arange(MLA_V)
        idx[h * 2 * LANE + LANE:h * 2 * LANE + LANE + MLA_NOPE] = h * per + np.arange(MLA_NOPE)
    return idx


def _w_oc_index():
    idx = np.zeros((GQA_HEADS * GQA_HD,), np.int64)
    for blk, hq in enumerate(_GQA_BLOCK_HEADS):
        c0 = (blk // 2) * LANE + (blk % 2) * HALF
        idx[c0:c0 + GQA_HD] = hq * GQA_HD + np.arange(GQA_HD)
    return idx


def _perm_gain(g, half):
    src = np.array([_rot_src(d, half)[0] for d in range(g.shape[-1])])
    return g[src]


def _rope_tables(S):
    assert MLA_ROPE == GQA_HD // 2
    inv = 1.0 / (ROPE_THETA ** (jnp.arange(0, MLA_ROPE, 2, dtype=F32) / MLA_ROPE))
    z = lambda k: jnp.zeros((k,), F32)
    per_head = [jnp.concatenate([inv, inv, z(GQA_HD // 2)]), jnp.concatenate([z(GQA_HD // 2), inv, inv])]
    f_pos = jnp.concatenate([z(MLA_NOPE), inv, inv, z(LANE - MLA_NOPE - MLA_ROPE), z(LANE)])
    f_row = jnp.concatenate([z(LANE), per_head[0], per_head[0]])
    f_col = jnp.concatenate([z(LANE), per_head[1], per_head[1]])
    t = jnp.arange(S, dtype=jnp.int32)
    pos = t.astype(F32)[:, None]
    row = (t // GRID_W).astype(F32)[:, None]
    col = (t % GRID_W).astype(F32)[:, None]
    ang = pos * f_pos[None, :] + row * f_row[None, :] + col * f_col[None, :]
    return jnp.cos(ang), jnp.sin(ang)


def _ln_kernel(x_ref, g_ref, b_ref, o_ref):
    o_ref[...] = _ln_rows(x_ref[...], g_ref[...], b_ref[...])


def _layer_norm_rows(x, g, b):
    n, d = x.shape
    tm = _row_tile(n, 1024)
    return pl.pallas_call(
        _ln_kernel,
        grid=(n // tm,),
        in_specs=[pl.BlockSpec((tm, d), lambda i: (i, 0)),
                  pl.BlockSpec((1, d), lambda i: (0, 0)),
                  pl.BlockSpec((1, d), lambda i: (0, 0))],
        out_specs=pl.BlockSpec((tm, d), lambda i: (i, 0)),
        out_shape=jax.ShapeDtypeStruct((n, d), F32),
        compiler_params=_cparams("parallel"),
    )(x, g.reshape(1, d), b.reshape(1, d))


def _rms_scale(ss, width):
    return lax.rsqrt(ss * (1.0 / width) + RMS_EPS)


def _pre_kernel(h_ref, cos_ref, sin_ref, wext_ref, wuq_ref, wukv_ref, qn_ref, kvn_ref,
                gq_ref, gqp_ref, gk_ref, gkp_ref,
                qa_ref, kva_ref, glu_ref, qg_ref, kvg_ref):
    hb = h_ref[0].astype(BF16)
    z = jnp.dot(hb, wext_ref[...], preferred_element_type=F32)
    cm = cos_ref[:, 0:LANE]
    cg = cos_ref[:, LANE:2 * LANE]
    sm = sin_ref[:, 0:LANE]
    sg = sin_ref[:, LANE:2 * LANE]

    cq = z[:, _OFF_CQ:_OFF_CQ + MLA_Q_LORA]
    cqn = cq * _rms_scale(jnp.sum(cq * cq, axis=-1, keepdims=True), MLA_Q_LORA) * qn_ref[...]
    a = jnp.dot(cqn.astype(BF16), wuq_ref[...], preferred_element_type=F32)
    rot0 = MLA_HEADS * LANE
    for h in range(MLA_HEADS):
        roped = a[:, h * LANE:(h + 1) * LANE] * cm + a[:, rot0 + h * LANE:rot0 + (h + 1) * LANE] * sm
        qa_ref[0, :, h * LANE:(h + 1) * LANE] = (roped * MLA_SCALE).astype(BF16)

    ckv = z[:, _OFF_CKV:_OFF_CKV + MLA_KV_LORA]
    ckvn = ckv * _rms_scale(jnp.sum(ckv * ckv, axis=-1, keepdims=True), MLA_KV_LORA) * kvn_ref[...]
    kvm = jnp.dot(ckvn.astype(BF16), wukv_ref[...], preferred_element_type=F32)
    kr = z[:, _OFF_KR:_OFF_KR + LANE] * cm + z[:, _OFF_KRR:_OFF_KRR + LANE] * sm
    lane = lax.broadcasted_iota(jnp.int32, (1, LANE), 1)
    one_at = [jnp.where(lane == L, 1.0, 0.0) for L in _SUM_LANE]
    for h in range(MLA_HEADS):
        c0 = h * 2 * LANE
        kva_ref[0, :, c0:c0 + LANE] = (kvm[:, c0:c0 + LANE] + one_at[h % 2]).astype(BF16)
        kva_ref[0, :, c0 + LANE:c0 + 2 * LANE] = (kvm[:, c0 + LANE:c0 + 2 * LANE] + kr).astype(BF16)

    glu_ref[0] = z[:, _OFF_GA:_OFF_GA + CONV_CH] * jax.nn.sigmoid(z[:, _OFF_GG:_OFF_GG + CONV_CH])

    low = lax.broadcasted_iota(jnp.int32, (h_ref.shape[1], LANE), 1) < HALF

    def norm_rope(x, x_rot, g, g_rot):
        sq = x * x
        r0 = _rms_scale(jnp.sum(jnp.where(low, sq, 0.0), axis=-1, keepdims=True), GQA_HD)
        r1 = _rms_scale(jnp.sum(jnp.where(low, 0.0, sq), axis=-1, keepdims=True), GQA_HD)
        return (x * g * cg + x_rot * g_rot * sg) * jnp.where(low, r0, r1)

    gq = gq_ref[...]
    gqp = gqp_ref[...]
    for p in range(GQA_HEADS // 2):
        qn = norm_rope(z[:, _OFF_QC + p * LANE:_OFF_QC + (p + 1) * LANE],
                       z[:, _OFF_QCR + p * LANE:_OFF_QCR + (p + 1) * LANE], gq, gqp) * GQA_SCALE
        qg_ref[0, :, 2 * p * LANE:(2 * p + 1) * LANE] = jnp.where(low, qn, 0.0).astype(BF16)
        qg_ref[0, :, (2 * p + 1) * LANE:(2 * p + 2) * LANE] = jnp.where(low, 0.0, qn).astype(BF16)

    kvg_ref[0, :, 0:LANE] = norm_rope(z[:, _OFF_KC:_OFF_KC + LANE], z[:, _OFF_KCR:_OFF_KCR + LANE],
                                      gk_ref[...], gkp_ref[...]).astype(BF16)
    kvg_ref[0, :, LANE:2 * LANE] = (z[:, _OFF_VC:_OFF_VC + LANE] + one_at[0]).astype(BF16)
    kvg_ref[0, :, 2 * LANE:3 * LANE] = (z[:, _OFF_VC + LANE:_OFF_VC + 2 * LANE] + one_at[1]).astype(BF16)


def _mixer_pre(h, tab, lw):
    B, S, D = h.shape
    tm = _row_tile(S, 512)
    const = lambda b, i: (0, 0)
    row3 = lambda b, i: (b, i, 0)
    outs = [(MLA_HEADS * LANE, BF16), (MLA_HEADS * 2 * LANE, BF16), (CONV_CH, F32),
            (GQA_HEADS * LANE, BF16), (3 * LANE, BF16)]
    return pl.pallas_call(
        _pre_kernel,
        grid=(B, S // tm),
        in_specs=[pl.BlockSpec((1, tm, D), row3),
                  pl.BlockSpec((tm, 2 * LANE), lambda b, i: (i, 0)),
                  pl.BlockSpec((tm, 2 * LANE), lambda b, i: (i, 0)),
                  pl.BlockSpec(lw["w_ext"].shape, const),
                  pl.BlockSpec(lw["w_uq"].shape, const),
                  pl.BlockSpec(lw["w_ukv"].shape, const),
                  pl.BlockSpec((1, MLA_Q_LORA), const),
                  pl.BlockSpec((1, MLA_KV_LORA), const),
                  pl.BlockSpec((1, LANE), const),
                  pl.BlockSpec((1, LANE), const),
                  pl.BlockSpec((1, LANE), const),
                  pl.BlockSpec((1, LANE), const)],
        out_specs=[pl.BlockSpec((1, tm, w), row3) for w, _ in outs],
        out_shape=[jax.ShapeDtypeStruct((B, S, w), dt) for w, dt in outs],
        compiler_params=_cparams("parallel", "parallel"),
    )(h, tab[0], tab[1], lw["w_ext"], lw["w_uq"], lw["w_ukv"], lw["qn"], lw["kvn"],
      lw["gq"], lw["gqp"], lw["gk"], lw["gkp"])


def _attn_kernel(q_ref, kv_ref, o_ref, m_sc, acc_sc, s_sc, *, k_off, v_off, tkc):
    j = pl.program_id(3)
    tk = kv_ref.shape[1]
    units = [(i, c) for c in range(tk // tkc) for i in range(2)]
    n_col = tkc // LANE

    @pl.when(j == 0)
    def _():
        m_sc[...] = jnp.full(m_sc.shape, -jnp.inf, F32)
        acc_sc[...] = jnp.zeros(acc_sc.shape, F32)

    def scores(u):
        i, c = units[u]
        q = q_ref[0, :, i * LANE:(i + 1) * LANE]
        k = kv_ref[0, c * tkc:(c + 1) * tkc, k_off[i]:k_off[i] + LANE]
        return lax.dot_general(q, k, (((1,), (1,)), ((), ())), preferred_element_type=F32)

    s_sc[0] = scores(0)
    for u, (i, c) in enumerate(units):
        if u + 1 < len(units):
            s_sc[(u + 1) % 2] = scores(u + 1)
        cols = [s_sc[u % 2, :, cb * LANE:(cb + 1) * LANE] for cb in range(n_col)]
        mx = cols[0]
        for x in cols[1:]:
            mx = jnp.maximum(mx, x)
        m_prev = m_sc[i]
        m_new = jnp.maximum(m_prev, jnp.max(mx, axis=1, keepdims=True))
        alpha = jnp.exp2(m_prev - m_new)
        p = jnp.concatenate([jnp.exp2(x - m_new).astype(BF16) for x in cols], axis=1)
        v = kv_ref[0, c * tkc:(c + 1) * tkc, v_off[i]:v_off[i] + LANE]
        acc_sc[i] = alpha * acc_sc[i] + jnp.dot(p, v, preferred_element_type=F32)
        m_sc[i] = m_new

    @pl.when(j == pl.num_programs(3) - 1)
    def _():
        o0 = acc_sc[0] / acc_sc[0, :, _SUM_LANE[0]:_SUM_LANE[0] + 1]
        o1 = acc_sc[1] / acc_sc[1, :, _SUM_LANE[1]:_SUM_LANE[1] + 1]
        low = lax.broadcasted_iota(jnp.int32, o0.shape, 1) < HALF
        o_ref[0] = jnp.where(low, o0, o1).astype(o_ref.dtype)


ATTN_TQ = 1024
ATTN_TK = 4096
ATTN_TKC = 512


def _attention(q, kv, *, kvw_block, kv_block_of_pair, k_off, v_off):
    B, S, _ = q.shape
    n_pairs = 3
    tq = _row_tile(S, ATTN_TQ)
    tk = _row_tile(S, ATTN_TK)
    tkc = _row_tile(tk, ATTN_TKC)
    kern = functools.partial(_attn_kernel, k_off=k_off, v_off=v_off, tkc=tkc)
    return pl.pallas_call(
        kern,
        grid=(B, n_pairs, S // tq, S // tk),
        in_specs=[pl.BlockSpec((1, tq, 2 * LANE), lambda b, p, i, j: (b, i, p)),
                  pl.BlockSpec((1, tk, kvw_block), lambda b, p, i, j: (b, j, kv_block_of_pair(p)))],
        out_specs=pl.BlockSpec((1, tq, LANE), lambda b, p, i, j: (b, i, p)),
        out_shape=jax.ShapeDtypeStruct((B, S, n_pairs * LANE), BF16),
        scratch_shapes=[pltpu.VMEM((2, tq, LANE), F32),
                        pltpu.VMEM((2, tq, LANE), F32),
                        pltpu.VMEM((2, tq, tkc), F32)],
        compiler_params=_cparams("parallel", "parallel", "parallel", "arbitrary"),
    )(q, kv)


def _mla_attention(qa, kva):
    return _attention(qa, kva, kvw_block=4 * LANE, kv_block_of_pair=lambda p: p,
                      k_off=(LANE, 3 * LANE), v_off=(0, 2 * LANE))


def _gqa_attention(qg, kvg):
    return _attention(qg, kvg, kvw_block=3 * LANE, kv_block_of_pair=lambda p: 0,
                      k_off=(0, 0), v_off=(LANE, 2 * LANE))


_HALO = 16


def _conv_kernel(prev_ref, cur_ref, next_ref, w_ref, b_ref, g_ref, beta_ref, o_ref, xe_sc, sh_sc):
    i = pl.program_id(1)
    tc = cur_ref.shape[1]
    first = i == 0
    last = i == pl.num_programs(1) - 1
    xe_sc[0:_HALO, :] = jnp.where(first, 0.0, prev_ref[0])
    xe_sc[_HALO:_HALO + tc, :] = cur_ref[0]
    xe_sc[_HALO + tc:_HALO + tc + _HALO, :] = jnp.where(last, 0.0, next_ref[0])
    acc = jnp.zeros((tc, CONV_CH), F32)
    span = sh_sc.shape[0]
    for shift in range(SUBLANE):
        taps = [t for t in range(CONV_K) if (_HALO - CONV_K // 2 + t) % SUBLANE == shift]
        if not taps:
            continue
        sh_sc[...] = xe_sc[shift:shift + span, :]
        for t in taps:
            base = _HALO - CONV_K // 2 + t - shift
            acc = acc + sh_sc[base:base + tc, :] * w_ref[t:t + 1, :]
    u = acc + b_ref[...]
    y = _ln_rows(u, g_ref[...], beta_ref[...])
    o_ref[0] = (y * jax.nn.sigmoid(y)).astype(o_ref.dtype)


def _conv_module(glu, lw):
    B, S, C = glu.shape
    tc = _row_tile(S, 512)
    nh = tc // _HALO
    n_halo_blocks = S // _HALO
    const = lambda b, i: (0, 0)
    return pl.pallas_call(
        _conv_kernel,
        grid=(B, S // tc),
        in_specs=[pl.BlockSpec((1, _HALO, C), lambda b, i: (b, jnp.maximum(i * nh - 1, 0), 0)),
                  pl.BlockSpec((1, tc, C), lambda b, i: (b, i, 0)),
                  pl.BlockSpec((1, _HALO, C), lambda b, i: (b, jnp.minimum((i + 1) * nh, n_halo_blocks - 1), 0)),
                  pl.BlockSpec((CONV_K, C), const),
                  pl.BlockSpec((1, C), const),
                  pl.BlockSpec((1, C), const),
                  pl.BlockSpec((1, C), const)],
        out_specs=pl.BlockSpec((1, tc, C), lambda b, i: (b, i, 0)),
        out_shape=jax.ShapeDtypeStruct((B, S, C), BF16),
        scratch_shapes=[pltpu.VMEM((tc + 2 * _HALO, C), F32),
                        pltpu.VMEM((tc + 2 * _HALO - SUBLANE, C), F32)],
        compiler_params=_cparams("parallel", "parallel"),
    )(glu, glu, glu, lw["conv_w"], lw["conv_b"], lw["conv_ln_g"], lw["conv_ln_b"])


def _memkv_kernel(x_ref, w_ref, o_ref):
    o_ref[...] = jnp.dot(x_ref[...].astype(BF16), w_ref[...], preferred_element_type=F32).astype(o_ref.dtype)


def _mem_kv(mem2d, w_ckv):
    n, d = mem2d.shape
    tm = _row_tile(n, 256)
    return pl.pallas_call(
        _memkv_kernel,
        grid=(n // tm,),
        in_specs=[pl.BlockSpec((tm, d), lambda i: (i, 0)),
                  pl.BlockSpec(w_ckv.shape, lambda i: (0, 0))],
        out_specs=pl.BlockSpec((tm, w_ckv.shape[1]), lambda i: (i, 0)),
        out_shape=jax.ShapeDtypeStruct((n, w_ckv.shape[1]), BF16),
        compiler_params=_cparams("parallel"),
    )(mem2d, w_ckv)


def _mixout_xattn_kernel(h_ref, oa_ref, ob_ref, oc_ref, wmix_ref, g1_ref, b1_ref,
                         k_ref, v_ref, wq_ref, wo_ref, g2_ref, b2_ref, o_ref):
    tm = h_ref.shape[1]
    groups = [slice(r * (tm // XA_ROW_GROUPS), (r + 1) * (tm // XA_ROW_GROUPS)) for r in range(XA_ROW_GROUPS)]
    mix = [jnp.dot(jnp.concatenate([oa_ref[0, r, :], ob_ref[0, r, :], oc_ref[0, r, :]], axis=1), wmix_ref[...],
                   preferred_element_type=F32) for r in groups]
    h1 = [_ln_rows(DN_ALPHA * h_ref[0, r, :] + m, g1_ref[...], b1_ref[...]) for r, m in zip(groups, mix)]
    q = [(jnp.dot(h.astype(BF16), wq_ref[...], preferred_element_type=F32) * XA_SCALE).astype(BF16) for h in h1]
    att = []
    for qg in q:
        heads = []
        for hd in range(XA_HEADS):
            sl = slice(hd * XA_HD, (hd + 1) * XA_HD)
            s = lax.dot_general(qg[:, sl], k_ref[0, :, sl], (((1,), (1,)), ((), ())),
                                preferred_element_type=F32)
            p = jnp.exp(s - jnp.max(s, axis=1, keepdims=True))
            o = jnp.dot(p.astype(BF16), v_ref[0, :, sl], preferred_element_type=F32)
            heads.append((o / jnp.sum(p, axis=1, keepdims=True)).astype(BF16))
        att.append(jnp.concatenate(heads, axis=1))
    y = [jnp.dot(o, wo_ref[...], preferred_element_type=F32) for o in att]
    for r, h, yy in zip(groups, h1, y):
        o_ref[0, r, :] = _ln_rows(DN_ALPHA * h + yy, g2_ref[...], b2_ref[...])


def _mixout_cross_attention(h, oa, ob, oc, kvmem, lw):
    B, S, D = h.shape
    M = kvmem.shape[1]
    tm = _row_tile(S, 1024)
    const = lambda b, i: (0, 0)
    row = lambda b, i: (b, i, 0)
    return pl.pallas_call(
        _mixout_xattn_kernel,
        grid=(B, S // tm),
        in_specs=[pl.BlockSpec((1, tm, D), row),
                  pl.BlockSpec((1, tm, oa.shape[2]), row),
                  pl.BlockSpec((1, tm, ob.shape[2]), row),
                  pl.BlockSpec((1, tm, oc.shape[2]), row),
                  pl.BlockSpec(lw["w_mix"].shape, const),
                  pl.BlockSpec((1, D), const),
                  pl.BlockSpec((1, D), const),
                  pl.BlockSpec((1, M, D), lambda b, i: (b, 0, 0)),
                  pl.BlockSpec((1, M, D), lambda b, i: (b, 0, 1)),
                  pl.BlockSpec((D, D), const),
                  pl.BlockSpec((D, D), const),
                  pl.BlockSpec((1, D), const),
                  pl.BlockSpec((1, D), const)],
        out_specs=pl.BlockSpec((1, tm, D), row),
        out_shape=jax.ShapeDtypeStruct((B, S, D), F32),
        compiler_params=_cparams("parallel", "parallel"),
    )(h, oa, ob, oc, lw["w_mix"], lw["ln1_g"], lw["ln1_b"],
      kvmem, kvmem, lw["w_cq"], lw["w_co"], lw["ln2_g"], lw["ln2_b"])


MOE_ROUTER_TILE = 1024
MOE_TILE = 2048
MOE_SUB = 256
MOE_SUB_SHIFT = 8
MOE_SEG = 48
MOE_ALIGN = 16
MOE_EXPERTS_PER_STEP = 2


def _router_kernel(x_ref, wr_ref, bias_ref, gate_ref, pos_ref, dd_ref, cnt_ref, start_ref, sel_sc):
    T = x_ref.shape[0]
    x = x_ref[...]
    x_hi = x.astype(BF16)
    x_lo = (x - x_hi.astype(F32)).astype(BF16)
    w = wr_ref[...]
    w_hi = w.astype(BF16)
    w_lo = (w - w_hi.astype(F32)).astype(BF16)
    nt = (((1,), (1,)), ((), ()))
    both = lax.dot_general(jnp.concatenate([w_hi, w_lo], axis=0), x_hi, nt, preferred_element_type=F32)
    logits = (both[:N_EXPERTS] + both[N_EXPERTS:]
              + lax.dot_general(w_hi, x_lo, nt, preferred_element_type=F32))
    scores = jax.nn.sigmoid(logits)
    sel = scores + bias_ref[...]
    sc = [scores[e:e + 1, :] for e in range(N_EXPERTS)]
    se = [sel[e:e + 1, :] for e in range(N_EXPERTS)]

    gs = []
    for g in range(N_GROUPS):
        m = [se[g * EXPERTS_PER_GROUP + k] for k in range(EXPERTS_PER_GROUP)]
        best = None
        for a in range(EXPERTS_PER_GROUP):
            for b in range(a + 1, EXPERTS_PER_GROUP):
                pair = m[a] + m[b]
                best = pair if best is None else jnp.maximum(best, pair)
        gs.append(best)
    in_group = []
    for g in range(N_GROUPS):
        ok = None
        for o in range(N_GROUPS):
            if o == g:
                continue
            c = (gs[g] > gs[o]) if o < g else (gs[g] >= gs[o])
            ok = c if ok is None else (ok & c)
        in_group.append(ok)
    chosen = []
    for e in range(N_EXPERTS):
        g = e // EXPERTS_PER_GROUP
        beaten = jnp.zeros((1, T), jnp.int32)
        for o in range(g * EXPERTS_PER_GROUP, (g + 1) * EXPERTS_PER_GROUP):
            if o == e:
                continue
            c = (se[o] > se[e]) if o > e else (se[o] >= se[e])
            beaten = beaten + c.astype(jnp.int32)
        chosen.append(in_group[g] & (beaten < 2))
    denom = jnp.zeros((1, T), F32)
    for e in range(N_EXPERTS):
        denom = denom + jnp.where(chosen[e], sc[e], 0.0)
    for e in range(N_EXPERTS):
        gate_ref[e:e + 1, :] = jnp.where(chosen[e], sc[e] / denom, 0.0)
        sel_sc[e:e + 1, :] = chosen[e].astype(F32)

    selm = sel_sc[...].astype(BF16)
    sub_r = jnp.right_shift(lax.broadcasted_iota(jnp.int32, (T, T), 0), MOE_SUB_SHIFT)
    sub_c = jnp.right_shift(lax.broadcasted_iota(jnp.int32, (T, T), 1), MOE_SUB_SHIFT)
    same = sub_r == sub_c
    earlier = same & (lax.broadcasted_iota(jnp.int32, (T, T), 0) < lax.broadcasted_iota(jnp.int32, (T, T), 1))
    rank = jnp.dot(selm, earlier.astype(BF16), preferred_element_type=F32)
    cnt = jnp.dot(selm, same.astype(BF16), preferred_element_type=F32)
    cnt_ref[0] = cnt.astype(jnp.int32)

    seg = jnp.floor((cnt + (MOE_ALIGN - 1)) * (1.0 / MOE_ALIGN)) * MOE_ALIGN
    start = jnp.zeros((1, T), F32)
    d_lo = jnp.full((1, T), float(4 * T), F32)
    d_hi = jnp.full((1, T), -1.0, F32)
    for e in range(N_EXPERTS):
        start_ref[0, e:e + 1, :] = start.astype(jnp.int32)
        dest = jnp.where(chosen[e], rank[e:e + 1, :] + start, -1.0)
        pos_ref[e:e + 1, :] = dest
        d_lo = jnp.where(chosen[e], jnp.minimum(d_lo, dest), d_lo)
        d_hi = jnp.maximum(d_hi, dest)
        start = start + seg[e:e + 1, :]
    dd_ref[...] = jnp.zeros(dd_ref.shape, F32)
    dd_ref[0:1, :] = d_lo
    dd_ref[1:2, :] = d_hi


def _router(x, w_rt, bias):
    n, d = x.shape
    T = _row_tile(n, MOE_ROUTER_TILE)
    nt = n // T
    gate, pos, dd, cnt, start = pl.pallas_call(
        _router_kernel,
        grid=(nt,),
        in_specs=[pl.BlockSpec((T, d), lambda i: (i, 0)),
                  pl.BlockSpec((N_EXPERTS, d), lambda i: (0, 0)),
                  pl.BlockSpec((N_EXPERTS, 1), lambda i: (0, 0))],
        out_specs=[pl.BlockSpec((N_EXPERTS, T), lambda i: (0, i)),
                   pl.BlockSpec((N_EXPERTS, T), lambda i: (0, i)),
                   pl.BlockSpec((8, T), lambda i: (0, i)),
                   pl.BlockSpec((1, N_EXPERTS, T), lambda i: (i, 0, 0)),
                   pl.BlockSpec((1, N_EXPERTS, T), lambda i: (i, 0, 0))],
        out_shape=[jax.ShapeDtypeStruct((N_EXPERTS, n), F32),
                   jax.ShapeDtypeStruct((N_EXPERTS, n), F32),
                   jax.ShapeDtypeStruct((8, n), F32),
                   jax.ShapeDtypeStruct((nt, N_EXPERTS, T), jnp.int32),
                   jax.ShapeDtypeStruct((nt, N_EXPERTS, T), jnp.int32)],
        scratch_shapes=[pltpu.VMEM((N_EXPERTS, T), F32)],
        compiler_params=_cparams("parallel"),
    )(x, w_rt, bias)
    return gate, pos, dd, cnt[:, :, ::MOE_SUB].reshape(-1), start[:, :, ::MOE_SUB].reshape(-1)


MOE_SORTED_ROWS = -(-(2 * MOE_SUB + N_EXPERTS * MOE_ALIGN + MOE_SEG) // 32) * 32


def _moe_kernel(cnt_ref, start_ref, h_ref, gate_ref, pos_ref, dd_ref, wg_ref, wu_ref, wd_ref, g_ref, b_ref, o_ref,
                p_sc, xs_sc, *, router_subs):
    i = pl.program_id(0)
    step = pl.program_id(1)
    n_sub = p_sc.shape[0]
    R = p_sc.shape[1]
    half = R // 2

    @pl.when(step == 0)
    def _():
        for s in range(n_sub):
            tok = slice(s * MOE_SUB, (s + 1) * MOE_SUB)
            d_lo = dd_ref[0:1, tok].astype(jnp.int32)
            d_hi = dd_ref[1:2, tok].astype(jnp.int32)
            xb = h_ref[tok, :].astype(BF16)
            for r0 in (0, half):
                row = lax.broadcasted_iota(jnp.int32, (half, MOE_SUB), 0) + r0
                onehot = ((row == d_lo) | (row == d_hi)).astype(BF16)
                p_sc[s, r0:r0 + half, :] = onehot
                xs_sc[s, r0:r0 + half, :] = jnp.dot(onehot, xb, preferred_element_type=F32).astype(BF16)

    slot = lax.broadcasted_iota(jnp.int32, (MOE_SEG, MOE_SUB), 0)
    for k in range(MOE_EXPERTS_PER_STEP):
        e = step * MOE_EXPERTS_PER_STEP + k
        flat = [((i * (n_sub // router_subs) + s // router_subs) * N_EXPERTS + e) * router_subs + s % router_subs
                for s in range(n_sub)]
        n_rows = [cnt_ref[f] for f in flat]
        seg0 = [start_ref[f] for f in flat]
        most = n_rows[0]
        for c in n_rows[1:]:
            most = jnp.maximum(most, c)
        n_blocks = (most + MOE_SEG - 1) // MOE_SEG
        pos_row = pos_ref[pl.ds(e, 1), :].astype(jnp.int32)
        gate_row = gate_ref[pl.ds(e, 1), :]

        def block(jb, carry, k=k, n_rows=n_rows, seg0=seg0, pos_row=pos_row, gate_row=gate_row):
            base = [pl.multiple_of(jnp.minimum(seg0[s] + jb * MOE_SEG, R - MOE_SEG), MOE_ALIGN)
                    for s in range(n_sub)]
            dest = [pl.multiple_of(jnp.where(jb * MOE_SEG < n_rows[s], base[s], R), MOE_ALIGN)
                    for s in range(n_sub)]
            xin = [xs_sc[s, pl.ds(base[s], MOE_SEG), :] for s in range(n_sub)]
            xg = jnp.concatenate(xin, axis=0)
            hg = jnp.dot(xg, wg_ref[k], preferred_element_type=F32)
            hu = jnp.dot(xg, wu_ref[k], preferred_element_type=F32)
            act = (hg * jax.nn.sigmoid(hg) * hu).astype(BF16)
            y = jnp.dot(act, wd_ref[k], preferred_element_type=F32)
            for s in range(n_sub):
                tok = slice(s * MOE_SUB, (s + 1) * MOE_SUB)
                hit = (pos_row[:, tok] - base[s]) == slot
                g_rows = jnp.sum(jnp.where(hit, gate_row[:, tok], 0.0), axis=1, keepdims=True)
                own = jnp.sum(jnp.where(hit, 1.0, 0.0), axis=1, keepdims=True) > 0.0
                out_rows = (y[s * MOE_SEG:(s + 1) * MOE_SEG, :] * g_rows).astype(BF16)
                xs_sc[s, pl.ds(dest[s], MOE_SEG), :] = jnp.where(own, out_rows, xin[s])
            return carry

        lax.fori_loop(0, n_blocks, block, 0)

    @pl.when(step == N_EXPERTS // MOE_EXPERTS_PER_STEP - 1)
    def _():
        for s in range(n_sub):
            tok = slice(s * MOE_SUB, (s + 1) * MOE_SUB)
            moe = lax.dot_general(p_sc[s], xs_sc[s, 0:R, :], (((0,), (0,)), ((), ())), preferred_element_type=F32)
            o_ref[tok, :] = _ln_rows(DN_ALPHA * h_ref[tok, :] + moe, g_ref[...], b_ref[...])


def _moe(h, gate, pos, dd, cnt, start, lw, moe_w):
    n, d = h.shape
    layer = lw["layer"]
    T = _row_tile(n, MOE_TILE)
    n_sub = T // MOE_SUB
    router_subs = _row_tile(n, MOE_ROUTER_TILE) // MOE_SUB
    R = MOE_SORTED_ROWS
    f = D_FF_EXPERT
    ne = MOE_EXPERTS_PER_STEP
    grid_spec = pltpu.PrefetchScalarGridSpec(
        num_scalar_prefetch=2,
        grid=(n // T, N_EXPERTS // ne),
        in_specs=[pl.BlockSpec((T, d), lambda i, e, c, s: (i, 0)),
                  pl.BlockSpec((N_EXPERTS, T), lambda i, e, c, s: (0, i)),
                  pl.BlockSpec((N_EXPERTS, T), lambda i, e, c, s: (0, i)),
                  pl.BlockSpec((8, T), lambda i, e, c, s: (0, i)),
                  pl.BlockSpec((None, ne, d, f), lambda i, e, c, s: (layer, e, 0, 0)),
                  pl.BlockSpec((None, ne, d, f), lambda i, e, c, s: (layer, e, 0, 0)),
                  pl.BlockSpec((None, ne, f, d), lambda i, e, c, s: (layer, e, 0, 0)),
                  pl.BlockSpec((1, d), lambda i, e, c, s: (0, 0)),
                  pl.BlockSpec((1, d), lambda i, e, c, s: (0, 0))],
        out_specs=pl.BlockSpec((T, d), lambda i, e, c, s: (i, 0), pipeline_mode=pl.Buffered(1)),
        scratch_shapes=[pltpu.VMEM((n_sub, R, MOE_SUB), BF16), pltpu.VMEM((n_sub, R + MOE_SEG, d), BF16)],
    )
    return pl.pallas_call(
        functools.partial(_moe_kernel, router_subs=router_subs),
        grid_spec=grid_spec,
        out_shape=jax.ShapeDtypeStruct((n, d), F32),
        compiler_params=_cparams("parallel", "arbitrary"),
    )(cnt, start, h, gate, pos, dd, *moe_w, lw["ln3_g"], lw["ln3_b"])


def _prep_layer(l, w):
    ext_idx, ext_sgn = _w_ext_index()
    uq_idx, uq_sgn = _w_uq_index()
    mla_w = MLA_HEADS * MLA_V
    gq = w["gqa_q_norm"][l]
    gk = w["gqa_k_norm"][l]
    row = lambda v: v.reshape(1, -1).astype(F32)
    return {
        "w_ext": _take_cols(w["w_in"][l], ext_idx, ext_sgn).astype(BF16),
        "w_uq": _take_cols(w["mla_w_uq"][l], uq_idx, uq_sgn).astype(BF16),
        "w_ukv": _take_cols(w["mla_w_ukv"][l], _w_ukv_index()).astype(BF16),
        "qn": row(w["mla_q_norm"][l]),
        "kvn": row(w["mla_kv_norm"][l]),
        "gq": row(jnp.tile(gq, 2)),
        "gqp": row(jnp.tile(_perm_gain(gq, GQA_HD // 4), 2)),
        "gk": row(jnp.tile(gk, 2)),
        "gkp": row(jnp.tile(_perm_gain(gk, GQA_HD // 4), 2)),
        "conv_w": w["conv_w"][l].astype(F32),
        "conv_b": row(w["conv_b"][l]),
        "conv_ln_g": row(w["conv_ln_g"][l]),
        "conv_ln_b": row(w["conv_ln_b"][l]),
        "w_mix": jnp.concatenate([w["w_o"][l][:mla_w + CONV_CH],
                                  w["w_o"][l][mla_w + CONV_CH:][_w_oc_index()]], axis=0).astype(BF16),
        "ln1_g": row(w["ln1_g"][l]), "ln1_b": row(w["ln1_b"][l]),
        "w_cq": w["w_cq"][l].astype(BF16),
        "w_ckv": w["w_ckv"][l].astype(BF16),
        "w_co": w["w_co"][l].astype(BF16),
        "ln2_g": row(w["ln2_g"][l]), "ln2_b": row(w["ln2_b"][l]),
        "layer": l,
        "ln3_g": row(w["ln3_g"][l]), "ln3_b": row(w["ln3_b"][l]),
    }


def _trunk(x, mem, w, layers):
    B, S, D = x.shape
    n = B * S
    tab = _rope_tables(S)
    w_rt = w["w_router"].T.astype(F32)
    r_bias = w["router_bias"].reshape(N_EXPERTS, 1).astype(F32)
    moe_w = tuple(w[k].astype(BF16) for k in ("w_gate", "w_up", "w_down"))
    h = _layer_norm_rows(x.reshape(n, D), w["ln_in_g"], w["ln_in_b"])
    mem2d = mem.reshape(B * mem.shape[1], D)
    for lw in layers:
        qa, kva, glu, qg, kvg = _mixer_pre(h.reshape(B, S, D), tab, lw)
        oa = _mla_attention(qa, kva)
        ob = _conv_module(glu, lw)
        oc = _gqa_attention(qg, kvg)
        kvmem = _mem_kv(mem2d, lw["w_ckv"]).reshape(B, mem.shape[1], 2 * D)
        h = _mixout_cross_attention(h.reshape(B, S, D), oa, ob, oc, kvmem, lw).reshape(n, D)
        gate, pos, dd, cnt, start = _router(h, w_rt, r_bias)
        h = _moe(h, gate, pos, dd, cnt, start, lw, moe_w)
    return h.reshape(B, S, D)


def kernel(x_prompt, x_sample, mem_prompt, mem_sample, ln_in_g, ln_in_b, w_in, mla_q_norm, mla_w_uq,
           mla_kv_norm, mla_w_ukv, conv_w, conv_b, conv_ln_g, conv_ln_b, gqa_q_norm, gqa_k_norm, w_o,
           ln1_g, ln1_b, w_cq, w_ckv, w_co, ln2_g, ln2_b, w_router, router_bias, w_gate, w_up, w_down,
           ln3_g, ln3_b):
    w = dict(ln_in_g=ln_in_g, ln_in_b=ln_in_b, w_in=w_in, mla_q_norm=mla_q_norm, mla_w_uq=mla_w_uq,
             mla_kv_norm=mla_kv_norm, mla_w_ukv=mla_w_ukv, conv_w=conv_w, conv_b=conv_b,
             conv_ln_g=conv_ln_g, conv_ln_b=conv_ln_b, gqa_q_norm=gqa_q_norm, gqa_k_norm=gqa_k_norm,
             w_o=w_o, ln1_g=ln1_g, ln1_b=ln1_b, w_cq=w_cq, w_ckv=w_ckv, w_co=w_co, ln2_g=ln2_g,
             ln2_b=ln2_b, w_router=w_router, router_bias=router_bias, w_gate=w_gate, w_up=w_up,
             w_down=w_down, ln3_g=ln3_g, ln3_b=ln3_b)
    layers = [_prep_layer(l, w) for l in range(w_in.shape[0])]
    y_prompt = _trunk(x_prompt, mem_prompt, w, layers)
    y_sample = _trunk(x_sample, mem_sample, w, layers)
    return (y_prompt, y_sample)
```
